```python
import math
import jax, jax.numpy as jnp
from jax import lax
import numpy as np

D_MODEL = 1024
BATCH = 2
SEQ = 8192
DEPTH = 2
DEC_BATCH = 32
DEC_SEQ = 1
PAST_LEN = 16384
PAGE_SIZE = 128

N_EVEN = (DEPTH + 1) // 2
N_ODD = DEPTH // 2
H_A = 4
DK_A = D_MODEL // 8
DV_A = D_MODEL // 8
CONV_A = 4
CHUNK_A = 64
H_B = 8
DH_B = D_MODEL // 16
Q_BLOCK = 128
FORGET_BIAS = 9.0
D_CONV = D_MODEL
CONV_C = 31
N_MEM = 256
H_X = 4
DH_X = D_MODEL // H_X
D_FF = 4 * D_MODEL
EPS = 1e-6

W_QK_A = H_A * DK_A
W_V_A = H_A * DV_A
W_B = H_B * DH_B
W_CONV_A = 2 * W_QK_A + W_V_A
OFF_ZA = W_CONV_A
OFF_AA = OFF_ZA + W_V_A
OFF_BA = OFF_AA + H_A
OFF_QB = OFF_BA + H_A
OFF_KB = OFF_QB + W_B
OFF_VB = OFF_KB + W_B
OFF_FB = OFF_VB + W_B
P_EVEN = OFF_FB + H_B

kernel_name = 'hybrid_gdn_fox_conformer_step'

F32 = jnp.float32


def rmsnorm(x, g):
    xf = x.astype(F32)
    y = xf * lax.rsqrt(jnp.mean(xf * xf, axis=-1, keepdims=True) + EPS)
    return (y * g.astype(F32)).astype(x.dtype)


def layernorm(x, g, b):
    xf = x.astype(F32)
    mu = jnp.mean(xf, axis=-1, keepdims=True)
    xc = xf - mu
    y = xc * lax.rsqrt(jnp.mean(xc * xc, axis=-1, keepdims=True) + EPS)
    return (y * g.astype(F32) + b.astype(F32)).astype(x.dtype)


def l2norm(x):
    xf = x.astype(F32)
    return xf * lax.rsqrt(jnp.sum(xf * xf, axis=-1, keepdims=True) + EPS)


def causal_dwconv(buf, x, w):
    full = jnp.concatenate([buf.astype(x.dtype), x], axis=1)
    y = lax.conv_general_dilated(full, w[:, None, :].astype(x.dtype), window_strides=(1,), padding='VALID',
                                 dimension_numbers=('NWC', 'WIO', 'NWC'), feature_group_count=x.shape[-1])
    return y, full[:, -(w.shape[0] - 1):]


def gated_delta_rule(q, k, v, g, beta, s0):
    B, L, H, DK = q.shape
    DV = v.shape[-1]
    C = min(CHUNK_A, L)
    pad = (-L) % C
    N = (L + pad) // C

    def prep(t):
        t = jnp.pad(t, [(0, 0), (0, pad)] + [(0, 0)] * (t.ndim - 2))
        t = t.reshape((B, N, C) + t.shape[2:])
        return jnp.transpose(t, (1, 0, 3, 2, 4)) if t.ndim == 5 else jnp.transpose(t, (1, 0, 3, 2))

    qc, kc, vc, bc = prep(q), prep(k), prep(v), prep(beta)
    gc = jnp.cumsum(prep(g), axis=-1)
    idx = jnp.arange(C)
    causal = idx[:, None] >= idx[None, :]
    strict = idx[:, None] > idx[None, :]
    decay = jnp.exp(jnp.where(causal, gc[..., :, None] - gc[..., None, :], -jnp.inf))
    kb = kc * bc[..., None]
    vb = vc * bc[..., None]
    lmat = jnp.where(strict, jnp.einsum('nbhid,nbhjd->nbhij', kb, kc) * decay, 0.0)
    eye = jnp.eye(C, dtype=F32)
    t_inv = lax.linalg.triangular_solve(eye + lmat, jnp.broadcast_to(eye, lmat.shape), left_side=True,
                                        lower=True, unit_diagonal=True)
    u = jnp.einsum('nbhij,nbhje->nbhie', t_inv, vb)
    w = jnp.einsum('nbhij,nbhjd->nbhid', t_inv, kb * jnp.exp(gc)[..., None])
    a_qk = jnp.where(causal, jnp.einsum('nbhid,nbhjd->nbhij', qc, kc) * decay, 0.0)

    def step(s, xs):
        q_i, k_i, u_i, w_i, g_i, a_i = xs
        v_new = u_i - jnp.einsum('bhcd,bhde->bhce', w_i, s)
        o_i = jnp.einsum('bhcd,bhde->bhce', q_i * jnp.exp(g_i)[..., None], s) + jnp.einsum('bhij,bhje->bhie', a_i, v_new)
        g_last = g_i[..., -1]
        s = s * jnp.exp(g_last)[..., None, None] + jnp.einsum(
            'bhcd,bhce->bhde', k_i * jnp.exp(g_last[..., None] - g_i)[..., None], v_new)
        return s, o_i

    s_fin, o = lax.scan(step, s0, (qc, kc, u, w, gc, a_qk))
    o = jnp.transpose(o, (1, 0, 3, 2, 4)).reshape(B, N * C, H, DV)[:, :L]
    return o, s_fin


def fox_prompt(q, k, v, logf):
    B, L, H, D = q.shape
    nq = L // Q_BLOCK
    c = jnp.cumsum(logf, axis=1)
    c_t = jnp.transpose(c, (0, 2, 1))
    qs = jnp.transpose(q.reshape(B, nq, Q_BLOCK, H, D), (1, 0, 2, 3, 4))
    cs = jnp.transpose(c_t.reshape(B, H, nq, Q_BLOCK), (2, 0, 1, 3))
    kpos = jnp.arange(L)
    scale = D ** -0.5

    def block(args):
        q_blk, c_blk, i = args
        s = jnp.einsum('bqhd,bkhd->bhqk', q_blk, k, preferred_element_type=F32) * scale
        s = s + c_blk[..., :, None] - c_t[:, :, None, :]
        qpos = i * Q_BLOCK + jnp.arange(Q_BLOCK)
        s = jnp.where(kpos[None, :] <= qpos[:, None], s, -jnp.inf)
        p = jax.nn.softmax(s, axis=-1)
        return jnp.einsum('bhqk,bkhd->bqhd', p.astype(v.dtype), v)

    o = lax.map(block, (qs, cs, jnp.arange(nq)))
    return jnp.transpose(o, (1, 0, 2, 3, 4)).reshape(B, L, H, D)


def fox_sample(q, k, v, logf, ck, cv, clf, page_table):
    DB, T, H, D = q.shape
    kp = ck[page_table].reshape(DB, -1, H, D)
    vp = cv[page_table].reshape(DB, -1, H, D)
    lfp = clf[page_table].reshape(DB, -1, H).astype(F32)
    past = kp.shape[1]
    c_past = lfp - lax.cumsum(lfp, axis=1, reverse=True)
    c_new = jnp.cumsum(logf, axis=1)
    scale = D ** -0.5
    cq = jnp.transpose(c_new, (0, 2, 1))[..., :, None]
    s_past = jnp.einsum('bthd,bshd->bhts', q, kp, preferred_element_type=F32) * scale \
        + cq - jnp.transpose(c_past, (0, 2, 1))[:, :, None, :]
    s_new = jnp.einsum('bthd,bshd->bhts', q, k, preferred_element_type=F32) * scale \
        + cq - jnp.transpose(c_new, (0, 2, 1))[:, :, None, :]
    tpos = jnp.arange(T)
    s_new = jnp.where(tpos[None, :] <= tpos[:, None], s_new, -jnp.inf)
    p = jax.nn.softmax(jnp.concatenate([s_past, s_new], axis=-1), axis=-1)
    return jnp.einsum('bhts,bshd->bthd', p[..., :past].astype(v.dtype), vp) \
        + jnp.einsum('bhts,bshd->bthd', p[..., past:].astype(v.dtype), v)


def even_mixer(h, e, s0, buf0, fox_past, P):
    B, L, _ = h.shape
    proj = h @ P['w_in_e'][e]
    conv_out, new_buf = causal_dwconv(buf0, proj[..., :W_CONV_A], P['conv_a'][e])
    conv_out = jax.nn.silu(conv_out)
    qa = l2norm(conv_out[..., :W_QK_A].reshape(B, L, H_A, DK_A)) * (DK_A ** -0.5)
    ka = l2norm(conv_out[..., W_QK_A:2 * W_QK_A].reshape(B, L, H_A, DK_A))
    va = conv_out[..., 2 * W_QK_A:].reshape(B, L, H_A, DV_A).astype(F32)
    za = proj[..., OFF_ZA:OFF_AA].reshape(B, L, H_A, DV_A).astype(F32)
    a_in = proj[..., OFF_AA:OFF_BA].astype(F32)
    g = -jnp.exp(P['a_log'][e].astype(F32)) * jax.nn.softplus(a_in + P['dt_bias'][e].astype(F32))
    beta = jax.nn.sigmoid(proj[..., OFF_BA:OFF_QB].astype(F32))
    oa, s_fin = gated_delta_rule(qa, ka, va, g, beta, s0.astype(F32))
    oa = (rmsnorm(oa, P['gnorm_a'][e]) * jax.nn.silu(za)).astype(h.dtype).reshape(B, L, W_V_A)
    qb = proj[..., OFF_QB:OFF_KB].reshape(B, L, H_B, DH_B)
    kb = proj[..., OFF_KB:OFF_VB].reshape(B, L, H_B, DH_B)
    vb = proj[..., OFF_VB:OFF_FB].reshape(B, L, H_B, DH_B)
    logf = jax.nn.log_sigmoid(proj[..., OFF_FB:].astype(F32) + P['b_f'][e].astype(F32))
    if fox_past is None:
        ob = fox_prompt(qb, kb, vb, logf)
    else:
        ob = fox_sample(qb, kb, vb, logf, *fox_past)
    out = jnp.concatenate([oa, ob.reshape(B, L, W_B).astype(h.dtype)], axis=-1) @ P['w_out_e'][e]
    return out, s_fin.astype(h.dtype), new_buf, kb, vb, logf.astype(h.dtype)


def odd_mixer(h, o, buf0, P):
    u = h @ P['w_pw1'][o] + P['b_pw1'][o]
    glu = u[..., :D_CONV] * jax.nn.sigmoid(u[..., D_CONV:])
    y, new_buf = causal_dwconv(buf0, glu, P['w_dw'][o])
    y = jax.nn.silu(layernorm(y + P['b_dw'][o], P['ln_g'][o], P['ln_b'][o]))
    return y @ P['w_pw2'][o], new_buf


def mem_kv(mem, l, P):
    B = mem.shape[0]
    kv = rmsnorm(mem, P['norm_mem'][l]) @ P['w_xkv'][l]
    return kv[..., :D_MODEL].reshape(B, N_MEM, H_X, DH_X), kv[..., D_MODEL:].reshape(B, N_MEM, H_X, DH_X)


def cross_attn(h, mk, mv, l, P):
    B, L, _ = h.shape
    q = (h @ P['w_xq'][l]).reshape(B, L, H_X, DH_X)
    s = jnp.einsum('blhd,bmhd->bhlm', q, mk, preferred_element_type=F32) * (DH_X ** -0.5)
    p = jax.nn.softmax(s, axis=-1)
    o = jnp.einsum('bhlm,bmhd->blhd', p.astype(mv.dtype), mv)
    return o.reshape(B, L, D_MODEL) @ P['w_xo'][l]


def mlp(h, l, P):
    return jnp.square(jax.nn.relu(h @ P['w_up'][l])) @ P['w_down'][l]


def trunk(x, mem_k, mem_v, gdn_s, gdn_buf, cfm_buf, fox_past, P):
    s_l, gb_l, cb_l, fk_l, fv_l, flf_l = [], [], [], [], [], []
    for l in range(DEPTH):
        h = rmsnorm(x, P['norm_mix'][l])
        if l % 2 == 0:
            e = l // 2
            past = None if fox_past is None else (fox_past[0][e], fox_past[1][e], fox_past[2][e], fox_past[3])
            out, s_new, gb, fk, fv, flf = even_mixer(h, e, gdn_s[e], gdn_buf[e], past, P)
            s_l.append(s_new); gb_l.append(gb); fk_l.append(fk); fv_l.append(fv); flf_l.append(flf)
        else:
            o = l // 2
            out, cb = odd_mixer(h, o, cfm_buf[o], P)
            cb_l.append(cb)
        x = x + out
        x = x + cross_attn(rmsnorm(x, P['norm_x'][l]), mem_k[l], mem_v[l], l, P)
        x = x + mlp(rmsnorm(x, P['norm_f'][l]), l, P)
    y = rmsnorm(x, P['norm_out'])
    return y, jnp.stack(s_l), jnp.stack(gb_l), jnp.stack(cb_l), jnp.stack(fk_l), jnp.stack(fv_l), jnp.stack(flf_l)


def setup_inputs(seed: int = 0) -> dict:
    key = jax.random.key(seed)
    ks = iter(jax.random.split(key, 48))
    n_pages = PAST_LEN // PAGE_SIZE
    n_pool = (5 * DEC_BATCH * n_pages) // 4

    def nrm(shape, scale=1.0):
        return scale * jax.random.normal(next(ks), shape, F32)

    def gain(shape):
        return 1.0 + nrm(shape, 0.02)

    inp = {}
    inp['x_prompt'] = nrm((BATCH, SEQ, D_MODEL))
    inp['x_sample'] = nrm((DEC_BATCH, DEC_SEQ, D_MODEL))
    inp['mem_prompt'] = nrm((BATCH, N_MEM, D_MODEL))
    inp['cache_fox_k'] = nrm((N_EVEN, n_pool, PAGE_SIZE, H_B, DH_B))
    inp['cache_fox_v'] = nrm((N_EVEN, n_pool, PAGE_SIZE, H_B, DH_B))
    inp['cache_fox_logf'] = jax.nn.log_sigmoid(FORGET_BIAS + nrm((N_EVEN, n_pool, PAGE_SIZE, H_B)))
    inp['page_table'] = jax.random.permutation(next(ks), n_pool)[:DEC_BATCH * n_pages].reshape(
        DEC_BATCH, n_pages).astype(jnp.int32)
    inp['state_gdn'] = nrm((N_EVEN, DEC_BATCH, H_A, DK_A, DV_A), 0.1)
    inp['state_gdn_conv'] = nrm((N_EVEN, DEC_BATCH, CONV_A - 1, W_CONV_A))
    inp['state_cfm_conv'] = nrm((N_ODD, DEC_BATCH, CONV_C - 1, D_CONV), 0.5)
    inp['cache_mem_k'] = nrm((DEPTH, DEC_BATCH, N_MEM, H_X, DH_X))
    inp['cache_mem_v'] = nrm((DEPTH, DEC_BATCH, N_MEM, H_X, DH_X))
    inp['norm_mix'] = gain((DEPTH, D_MODEL))
    inp['w_in_e'] = nrm((N_EVEN, D_MODEL, P_EVEN), D_MODEL ** -0.5)
    inp['conv_a'] = nrm((N_EVEN, CONV_A, W_CONV_A), CONV_A ** -0.5)
    inp['a_log'] = jnp.log(jax.random.uniform(next(ks), (N_EVEN, H_A), F32, 1.0, 16.0))
    dt = jnp.exp(jax.random.uniform(next(ks), (N_EVEN, H_A), F32, math.log(1e-3), math.log(1e-1)))
    inp['dt_bias'] = dt + jnp.log(-jnp.expm1(-dt))
    inp['gnorm_a'] = gain((N_EVEN, DV_A))
    inp['b_f'] = FORGET_BIAS + nrm((N_EVEN, H_B), 0.1)
    inp['w_out_e'] = nrm((N_EVEN, W_V_A + W_B, D_MODEL), (W_V_A + W_B) ** -0.5)
    inp['w_pw1'] = nrm((N_ODD, D_MODEL, 2 * D_CONV), D_MODEL ** -0.5)
    inp['b_pw1'] = nrm((N_ODD, 2 * D_CONV), 0.02)
    inp['w_dw'] = nrm((N_ODD, CONV_C, D_CONV), CONV_C ** -0.5)
    inp['b_dw'] = nrm((N_ODD, D_CONV), 0.02)
    inp['ln_g'] = gain((N_ODD, D_CONV))
    inp['ln_b'] = nrm((N_ODD, D_CONV), 0.02)
    inp['w_pw2'] = nrm((N_ODD, D_CONV, D_MODEL), D_CONV ** -0.5)
    inp['norm_mem'] = gain((DEPTH, D_MODEL))
    inp['norm_x'] = gain((DEPTH, D_MODEL))
    inp['w_xq'] = nrm((DEPTH, D_MODEL, D_MODEL), D_MODEL ** -0.5)
    inp['w_xkv'] = nrm((DEPTH, D_MODEL, 2 * D_MODEL), D_MODEL ** -0.5)
    inp['w_xo'] = nrm((DEPTH, D_MODEL, D_MODEL), D_MODEL ** -0.5)
    inp['norm_f'] = gain((DEPTH, D_MODEL))
    inp['w_up'] = nrm((DEPTH, D_MODEL, D_FF), D_MODEL ** -0.5)
    inp['w_down'] = nrm((DEPTH, D_FF, D_MODEL), D_FF ** -0.5)
    inp['norm_out'] = gain((D_MODEL,))
    return inp


def reference(x_prompt, x_sample, mem_prompt, cache_fox_k, cache_fox_v, cache_fox_logf, page_table,
              state_gdn, state_gdn_conv, state_cfm_conv, cache_mem_k, cache_mem_v,
              norm_mix, w_in_e, conv_a, a_log, dt_bias, gnorm_a, b_f, w_out_e,
              w_pw1, b_pw1, w_dw, b_dw, ln_g, ln_b, w_pw2,
              norm_mem, norm_x, w_xq, w_xkv, w_xo, norm_f, w_up, w_down, norm_out):
    P = dict(norm_mix=norm_mix, w_in_e=w_in_e, conv_a=conv_a, a_log=a_log, dt_bias=dt_bias, gnorm_a=gnorm_a,
             b_f=b_f, w_out_e=w_out_e, w_pw1=w_pw1, b_pw1=b_pw1, w_dw=w_dw, b_dw=b_dw, ln_g=ln_g, ln_b=ln_b,
             w_pw2=w_pw2, norm_mem=norm_mem, norm_x=norm_x, w_xq=w_xq, w_xkv=w_xkv, w_xo=w_xo,
             norm_f=norm_f, w_up=w_up, w_down=w_down, norm_out=norm_out)
    mkv = [mem_kv(mem_prompt, l, P) for l in range(DEPTH)]
    mem_k_prompt = jnp.stack([kv[0] for kv in mkv])
    mem_v_prompt = jnp.stack([kv[1] for kv in mkv])
    dt = x_prompt.dtype
    s0 = jnp.zeros((N_EVEN, BATCH, H_A, DK_A, DV_A), dt)
    gb0 = jnp.zeros((N_EVEN, BATCH, CONV_A - 1, W_CONV_A), dt)
    cb0 = jnp.zeros((N_ODD, BATCH, CONV_C - 1, D_CONV), dt)
    y_prompt, gdn_p, gconv_p, cconv_p, fk_p, fv_p, flf_p = trunk(
        x_prompt, mem_k_prompt, mem_v_prompt, s0, gb0, cb0, None, P)
    y_sample, gdn_s, gconv_s, cconv_s, fk_s, fv_s, flf_s = trunk(
        x_sample, cache_mem_k, cache_mem_v, state_gdn, state_gdn_conv, state_cfm_conv,
        (cache_fox_k, cache_fox_v, cache_fox_logf, page_table), P)
    return (y_prompt, y_sample, fk_p, fv_p, flf_p, fk_s, fv_s, flf_s,
            gdn_p, gconv_p, gdn_s, gconv_s, cconv_p, cconv_s, mem_k_prompt, mem_v_prompt)
```

```python
import functools
import math

import jax
import jax.numpy as jnp
import numpy as np
from jax import lax
from jax.experimental import pallas as pl
from jax.experimental.pallas import tpu as pltpu

F32 = jnp.float32
BF16 = jnp.bfloat16
EPS = 1e-6
NEG = -1e30
LOG2E = 1.4426950408889634

H_A = 4
DK_A = 128
DV_A = 128
CONV_A = 4
CHUNK_A = 64
H_B = 8
DH_B = 64
H_X = 4
CONV_C = 31
PAGE = 128

W_QK_A = H_A * DK_A
W_V_A = H_A * DV_A
W_CONV_A = 2 * W_QK_A + W_V_A
W_B = H_B * DH_B
LANES = 128
SM_A = 0
SM_B = H_A
SM_F = 2 * H_A

VMEM_LIMIT = 56 * 1024 * 1024


def _cparams(*sem):
    return pltpu.CompilerParams(dimension_semantics=sem, vmem_limit_bytes=VMEM_LIMIT)


def _const_spec(shape):
    nd = len(shape)
    return pl.BlockSpec(shape, lambda *_: (0,) * nd)


def _rms(x, g):
    return x * lax.rsqrt(jnp.mean(x * x, axis=-1, keepdims=True) + EPS) * g


def _silu(x):
    return x * jax.nn.sigmoid(x)


def _softplus(x):
    return jnp.maximum(x, 0.0) + jnp.log1p(jnp.exp(-jnp.abs(x)))


def _split3(x):
    h = x.astype(BF16)
    r = x - h.astype(F32)
    m = r.astype(BF16)
    l = (r - m.astype(F32)).astype(BF16)
    return h, m, l


def _dot3_l(mat01, x):
    h, m, l = _split3(x)
    d = lambda t: jnp.dot(mat01, t, preferred_element_type=F32)
    return d(h) + d(m) + d(l)


def _dot3_r(x, mat01):
    h, m, l = _split3(x)
    d = lambda t: jnp.dot(t, mat01, preferred_element_type=F32)
    return d(h) + d(m) + d(l)


def _dot_nt(a, b):
    return lax.dot_general(a, b, (((1,), (1,)), ((), ())), preferred_element_type=F32)


def _dot_tn(a, b):
    return lax.dot_general(a, b, (((0,), (0,)), ((), ())), preferred_element_type=F32)


def _iota(shape, dim):
    return lax.broadcasted_iota(jnp.int32, shape, dim)


def _col_from_row(row, n):
    eye = _iota((n, n), 0) == _iota((n, n), 1)
    return jnp.sum(jnp.where(eye, jnp.broadcast_to(row, (n, n)), 0.0), axis=1, keepdims=True)


def _linear_kernel(*refs, n_in, has_gain, has_bias, has_res, act, out_widths, chunk):
    it = iter(refs)
    x_refs = [next(it) for _ in range(n_in)]
    w_refs = [next(it) for _ in range(n_in)]
    gain_ref = next(it) if has_gain else None
    bias_ref = next(it) if has_bias else None
    res_ref = next(it) if has_res else None
    out_refs = list(it)
    xs = []
    for i, xr in enumerate(x_refs):
        x = xr[...]
        if has_gain and i == 0:
            x = _rms(x, gain_ref[...])
        xs.append(x.astype(BF16))
    n_total = sum(out_widths)

    def mm(col0, cw):
        acc = None
        for xb, wr in zip(xs, w_refs):
            d = jnp.dot(xb, wr[:, col0:col0 + cw], preferred_element_type=F32)
            acc = d if acc is None else acc + d
        if has_bias:
            acc = acc + bias_ref[:, col0:col0 + cw]
        return acc

    off = 0
    for o_ref, width in zip(out_refs, out_widths):
        for c0 in range(0, width, chunk):
            cw = min(chunk, width - c0)
            y = mm(off + c0, cw)
            if act == "glu":
                y = y * jax.nn.sigmoid(mm(n_total + off + c0, cw))
            if has_res:
                y = y + res_ref[:, off + c0:off + c0 + cw]
            o_ref[:, c0:c0 + cw] = y
        off += width


def _linear(xs, ws, *, gain=None, bias=None, res=None, act=None, out_widths=None, tm=512,
            chunk=512, name="linear"):
    m = xs[0].shape[0]
    tm = min(tm, m)
    assert m % tm == 0
    n_mm = ws[0].shape[1]
    n_out = n_mm // 2 if act == "glu" else n_mm
    if out_widths is None:
        out_widths = (n_out,)
    assert sum(out_widths) == n_out
    row = lambda i: (i, 0)
    in_specs = [pl.BlockSpec((tm, x.shape[1]), row) for x in xs]
    in_specs += [_const_spec(w.shape) for w in ws]
    args = list(xs) + list(ws)
    if gain is not None:
        in_specs.append(_const_spec(gain.shape)); args.append(gain)
    if bias is not None:
        in_specs.append(_const_spec(bias.shape)); args.append(bias)
    if res is not None:
        in_specs.append(pl.BlockSpec((tm, res.shape[1]), row)); args.append(res)
    out_shape = [jax.ShapeDtypeStruct((m, w), F32) for w in out_widths]
    out_specs = [pl.BlockSpec((tm, w), row) for w in out_widths]
    kern = functools.partial(_linear_kernel, n_in=len(xs), has_gain=gain is not None,
                             has_bias=bias is not None, has_res=res is not None, act=act,
                             out_widths=tuple(out_widths), chunk=chunk)
    outs = pl.pallas_call(kern, grid=(m // tm,), in_specs=in_specs, out_specs=out_specs,
                          out_shape=out_shape, compiler_params=_cparams("parallel"), name=name)(*args)
    return outs if len(outs) > 1 else outs[0]


def _mlp_kernel(*refs, chunk, final_norm):
    if final_norm:
        x_ref, g_ref, wu_ref, wd_ref, go_ref, o_ref = refs
    else:
        x_ref, g_ref, wu_ref, wd_ref, o_ref = refs
    x = x_ref[...]
    xn = _rms(x, g_ref[...]).astype(BF16)
    acc = x
    d_ff = wu_ref.shape[1]
    for c0 in range(0, d_ff, chunk):
        h = jnp.dot(xn, wu_ref[:, c0:c0 + chunk], preferred_element_type=F32)
        h = jnp.square(jnp.maximum(h, 0.0)).astype(BF16)
        acc = acc + jnp.dot(h, wd_ref[c0:c0 + chunk, :], preferred_element_type=F32)
    if final_norm:
        acc = _rms(acc, go_ref[...])
    o_ref[...] = acc


def _mlp(x, gain, w_up, w_down, final_gain=None, tm=512, chunk=512):
    m, d = x.shape
    tm = min(tm, m)
    row = lambda i: (i, 0)
    in_specs = [pl.BlockSpec((tm, d), row), _const_spec(gain.shape), _const_spec(w_up.shape),
                _const_spec(w_down.shape)]
    args = [x, gain, w_up, w_down]
    if final_gain is not None:
        in_specs.append(_const_spec(final_gain.shape)); args.append(final_gain)
    kern = functools.partial(_mlp_kernel, chunk=chunk, final_norm=final_gain is not None)
    return pl.pallas_call(kern, grid=(m // tm,), in_specs=in_specs,
                          out_specs=pl.BlockSpec((tm, d), row),
                          out_shape=jax.ShapeDtypeStruct((m, d), F32),
                          compiler_params=_cparams("parallel"), name="mlp")(*args)


def _xattn_kernel(x_ref, g_ref, wq_ref, mk_ref, mv_ref, wo_ref, o_ref):
    x = x_ref[0]
    d = x.shape[1]
    dh = d // H_X
    xn = _rms(x, g_ref[...]).astype(BF16)
    q = jnp.dot(xn, wq_ref[...], preferred_element_type=F32) * (dh ** -0.5)
    q = q.astype(BF16)
    outs = []
    for h in range(H_X):
        sl = slice(h * dh, (h + 1) * dh)
        s = _dot_nt(q[:, sl], mk_ref[0, :, sl])
        p = jnp.exp(s - jnp.max(s, axis=-1, keepdims=True))
        l = jnp.sum(p, axis=-1, keepdims=True)
        o = jnp.dot(p.astype(BF16), mv_ref[0, :, sl], preferred_element_type=F32)
        outs.append((o * (1.0 / l)).astype(BF16))
    o = jnp.concatenate(outs, axis=1)
    o_ref[0] = x + jnp.dot(o, wo_ref[...], preferred_element_type=F32)


def _xattn_prompt(x, gain, wq, mk, mv, wo, tm=512):
    b, l, d = x.shape
    tm = min(tm, l)
    nm = mk.shape[1]
    blk = lambda bi, i: (bi, i, 0)
    mem = lambda bi, i: (bi, 0, 0)
    return pl.pallas_call(
        _xattn_kernel, grid=(b, l // tm),
        in_specs=[pl.BlockSpec((1, tm, d), blk), _const_spec(gain.shape), _const_spec(wq.shape),
                  pl.BlockSpec((1, nm, d), mem), pl.BlockSpec((1, nm, d), mem), _const_spec(wo.shape)],
        out_specs=pl.BlockSpec((1, tm, d), blk),
        out_shape=jax.ShapeDtypeStruct((b, l, d), F32),
        compiler_params=_cparams("parallel", "parallel"), name="xattn_prompt")(x, gain, wq, mk, mv, wo)


def _log_sigmoid(x):
    return jnp.minimum(x, 0.0) - jnp.log1p(jnp.exp(-jnp.abs(x)))


def _head_lanes(x, h):
    p = h // 2
    blk = x[:, p * LANES:(p + 1) * LANES]
    if h % 2:
        blk = pltpu.roll(blk, DH_B, 1)
    return blk


def _fox_prep_kernel(q_ref, k_ref, v_ref, sm_ref, bf_ref, qf_ref, kf_ref, vf_ref, lf_ref, carry_ref):
    t = q_ref.shape[1]

    @pl.when(pl.program_id(1) == 0)
    def _():
        carry_ref[...] = jnp.zeros_like(carry_ref)

    lane = _iota((t, LANES), 1)
    gate_lane = (lane >= SM_F) & (lane < SM_F + H_B)
    logf = jnp.where(gate_lane, _log_sigmoid(sm_ref[0] + bf_ref[...]), 0.0)
    lf_ref[0] = logf
    tri = (_iota((t, t), 0) >= _iota((t, t), 1)).astype(BF16)
    c = _dot3_l(tri, logf) + carry_ref[0:1, :]
    carry_ref[0:1, :] = c[t - 1:t, :]
    c1, c2, c3 = _split3(c * LOG2E)
    c1, c2, c3 = c1.astype(F32), c2.astype(F32), c3.astype(F32)
    q = q_ref[0] * (DH_B ** -0.5 * LOG2E)
    k = k_ref[0]
    v = v_ref[0]
    low = lane < DH_B
    for h in range(H_B):
        col = slice(SM_F + h, SM_F + h + 1)
        b1, b2, b3 = c1[:, col], c2[:, col], c3[:, col]
        qx = jnp.where(lane == DH_B, b1, jnp.where(lane == DH_B + 1, b2, jnp.where(
            lane == DH_B + 2, b3, jnp.where(lane < DH_B + 6, 1.0, 0.0))))
        kx = jnp.where(lane < DH_B + 3, 1.0, jnp.where(lane == DH_B + 3, -b1, jnp.where(
            lane == DH_B + 4, -b2, jnp.where(lane == DH_B + 5, -b3, 0.0))))
        vx = jnp.where(lane == DH_B, 1.0, 0.0)
        qf_ref[0, h] = jnp.where(low, _head_lanes(q, h), qx).astype(BF16)
        kf_ref[0, h] = jnp.where(low, _head_lanes(k, h), kx).astype(BF16)
        vf_ref[0, h] = jnp.where(low, _head_lanes(v, h), vx).astype(BF16)


def _fox_prep(q, k, v, small, bf_row, t=256):
    b, l, w = q.shape
    t = min(t, l)
    blk = lambda bi, i: (bi, i, 0)
    hblk = lambda bi, i: (bi, 0, i, 0)
    hshape = jax.ShapeDtypeStruct((b, H_B, l, LANES), BF16)
    return pl.pallas_call(
        _fox_prep_kernel, grid=(b, l // t),
        in_specs=[pl.BlockSpec((1, t, w), blk)] * 3 + [pl.BlockSpec((1, t, LANES), blk),
                                                       _const_spec(bf_row.shape)],
        out_specs=[pl.BlockSpec((1, H_B, t, LANES), hblk)] * 3 + [pl.BlockSpec((1, t, LANES), blk)],
        out_shape=[hshape, hshape, hshape, jax.ShapeDtypeStruct((b, l, LANES), F32)],
        scratch_shapes=[pltpu.VMEM((8, LANES), F32)],
        compiler_params=_cparams("parallel", "arbitrary"), name="fox_prep")(q, k, v, small, bf_row)


def _fox_flash_kernel(it_ref, jt_ref, qf_ref, kf_ref, vf_ref, o_ref, m_ref, acc_ref):
    p_id = pl.program_id(1)
    i = it_ref[p_id]
    j = jt_ref[p_id]
    tq = qf_ref.shape[2]
    tk = kf_ref.shape[2]

    @pl.when(j == 0)
    def _():
        m_ref[...] = jnp.full_like(m_ref, NEG)
        acc_ref[...] = jnp.zeros_like(acc_ref)

    def head(h, masked):
        s = _dot_nt(qf_ref[0, h], kf_ref[0, h])
        if masked:
            s = jnp.where(_iota((tq, tk), 1) <= _iota((tq, tk), 0), s, NEG)
        m_prev = m_ref[h]
        m_new = jnp.maximum(m_prev, jnp.max(s, axis=-1, keepdims=True))
        alpha = jnp.exp2(m_prev - m_new)
        p = jnp.exp2(s - m_new).astype(BF16)
        acc_ref[h] = alpha * acc_ref[h] + jnp.dot(p, vf_ref[0, h], preferred_element_type=F32)
        m_ref[h] = m_new

    @pl.when(j < i)
    def _():
        for h in range(H_B):
            head(h, False)

    @pl.when(j == i)
    def _():
        for h in range(H_B):
            head(h, True)
        lane = _iota((tq, LANES), 1)
        for pr in range(H_B // 2):
            a0 = acc_ref[2 * pr]
            a1 = acc_ref[2 * pr + 1]
            o0 = a0 * (1.0 / a0[:, DH_B:DH_B + 1])
            o1 = a1 * (1.0 / a1[:, DH_B:DH_B + 1])
            o_ref[0, :, pr * LANES:(pr + 1) * LANES] = jnp.where(lane < DH_B, o0, pltpu.roll(o1, DH_B, 1))


def _fox_flash(qf, kf, vf, t=512):
    b, hh, l, _ = qf.shape
    t = min(t, l)
    n = l // t
    pairs = [(i, j) for i in range(n) for j in range(i + 1)]
    it = jnp.asarray(np.array([p[0] for p in pairs], np.int32))
    jt = jnp.asarray(np.array([p[1] for p in pairs], np.int32))
    qmap = lambda bi, p, it, jt: (bi, 0, it[p], 0)
    kmap = lambda bi, p, it, jt: (bi, 0, jt[p], 0)
    omap = lambda bi, p, it, jt: (bi, it[p], 0)
    grid_spec = pltpu.PrefetchScalarGridSpec(
        num_scalar_prefetch=2, grid=(b, len(pairs)),
        in_specs=[pl.BlockSpec((1, hh, t, LANES), qmap), pl.BlockSpec((1, hh, t, LANES), kmap),
                  pl.BlockSpec((1, hh, t, LANES), kmap)],
        out_specs=pl.BlockSpec((1, t, hh * DH_B), omap),
        scratch_shapes=[pltpu.VMEM((hh, t, 1), F32), pltpu.VMEM((hh, t, LANES), F32)])
    return pl.pallas_call(
        _fox_flash_kernel, grid_spec=grid_spec,
        out_shape=jax.ShapeDtypeStruct((b, l, hh * DH_B), F32),
        compiler_params=_cparams("parallel", "arbitrary"), name="fox_flash")(it, jt, qf, kf, vf)


def _gdn_kernel(x_ref, z_ref, a_ref, b_ref, cw_ref, al_ref, dt_ref, gn_ref, oa_ref, s_ref,
                xs_ref, st_ref, *, nc):
    c = CHUNK_A
    r = H_A * c
    t = nc * c
    halo = 8

    @pl.when(pl.program_id(1) == 0)
    def _():
        xs_ref[0:halo, :] = jnp.zeros((halo, xs_ref.shape[1]), F32)
        st_ref[...] = jnp.zeros_like(st_ref)

    @pl.when(pl.program_id(1) != 0)
    def _():
        xs_ref[0:halo, :] = xs_ref[t:t + halo, :]

    xs_ref[halo:halo + t, :] = x_ref[0]
    y = None
    for jj in range(CONV_A):
        o = halo - (CONV_A - 1) + jj
        term = xs_ref[o:o + t, :] * cw_ref[jj:jj + 1, :]
        y = term if y is None else y + term
    y = _silu(y)

    g_rows = -jnp.exp(al_ref[...]) * _softplus(a_ref[0] + dt_ref[...])
    beta_rows = jax.nn.sigmoid(b_ref[0])
    ri = _iota((r, r), 0)
    ci = _iota((r, r), 1)
    same = (ri // c) == (ci // c)
    incl = same & (ri >= ci)
    strict = same & (ri > ci)
    gc_rows = _dot3_r(g_rows, (same & (ri <= ci)).astype(BF16))
    eye = ri == ci
    last = ((ci % c) == (c - 1)) & same
    lane4 = _iota((r, H_A * DV_A), 1) // DV_A
    row4 = _iota((r, H_A * DV_A), 0) // c
    diag4 = lane4 == row4

    def stack(a):
        return jnp.concatenate([a[:, h * DK_A:(h + 1) * DK_A] for h in range(H_A)], axis=0)

    for n in range(nc):
        rows = slice(n * c, (n + 1) * c)
        gc_r = gc_rows[n:n + 1, :]
        gcb = jnp.broadcast_to(gc_r, (r, r))
        gc_c = jnp.sum(jnp.where(eye, gcb, 0.0), axis=1, keepdims=True)
        gl_c = jnp.sum(jnp.where(last, gcb, 0.0), axis=1, keepdims=True)
        beta_c = jnp.sum(jnp.where(eye, jnp.broadcast_to(beta_rows[n:n + 1, :], (r, r)), 0.0),
                         axis=1, keepdims=True)
        q = stack(y[rows, 0:W_QK_A])
        k = stack(y[rows, W_QK_A:2 * W_QK_A])
        v = stack(y[rows, 2 * W_QK_A:W_CONV_A])
        q = q * lax.rsqrt(jnp.sum(q * q, axis=1, keepdims=True) + EPS) * (DK_A ** -0.5)
        k = k * lax.rsqrt(jnp.sum(k * k, axis=1, keepdims=True) + EPS)
        decay = jnp.exp(jnp.where(incl, gc_c - gc_r, -jnp.inf))
        kb = k * beta_c
        k16 = k.astype(BF16)
        lmat = jnp.where(strict, _dot_nt(kb.astype(BF16), k16) * decay, 0.0)
        a_qk = jnp.where(incl, _dot_nt(q.astype(BF16), k16) * decay, 0.0)
        mneg = -lmat
        tinv = jnp.where(eye, 1.0, 0.0) + mneg
        for _ in range(int(math.log2(c)) - 1):
            mneg = jnp.dot(mneg.astype(BF16), mneg.astype(BF16), preferred_element_type=F32)
            tinv = tinv + jnp.dot(tinv.astype(BF16), mneg.astype(BF16), preferred_element_type=F32)
        eg = jnp.exp(gc_c)
        rhs = jnp.concatenate([v * beta_c, kb * eg], axis=1).astype(BF16)
        uw = jnp.dot(tinv.astype(BF16), rhs, preferred_element_type=F32)
        u = uw[:, 0:DV_A]
        w = uw[:, DV_A:2 * DV_A]
        s_all = st_ref[...]
        s16 = s_all.astype(BF16)
        ws = jnp.dot(w.astype(BF16), s16, preferred_element_type=F32)
        qs = jnp.dot((q * eg).astype(BF16), s16, preferred_element_type=F32)

        def diag(a):
            a = jnp.where(diag4, a, 0.0)
            return (a[:, 0:DV_A] + a[:, DV_A:2 * DV_A]) + (a[:, 2 * DV_A:3 * DV_A] + a[:, 3 * DV_A:4 * DV_A])

        v_new = u - diag(ws)
        o = diag(qs) + jnp.dot(a_qk.astype(BF16), v_new.astype(BF16), preferred_element_type=F32)
        kdec = (k * jnp.exp(gl_c - gc_c)).astype(BF16)
        v_exp = jnp.where(diag4, jnp.concatenate([v_new] * H_A, axis=1), 0.0).astype(BF16)
        sel = ((_iota((r, H_A * DV_A), 0) % c) == (c - 1)) & diag4
        dec_l = jnp.sum(jnp.where(sel, jnp.broadcast_to(gl_c, (r, H_A * DV_A)), 0.0), axis=0, keepdims=True)
        st_ref[...] = s_all * jnp.exp(dec_l) + _dot_tn(kdec, v_exp)
        on = o * lax.rsqrt(jnp.mean(o * o, axis=1, keepdims=True) + EPS) * gn_ref[...]
        on = jnp.concatenate([on[h * c:(h + 1) * c, :] for h in range(H_A)], axis=1)
        oa_ref[0, rows, :] = on * _silu(z_ref[0, rows, :])

    @pl.when(pl.program_id(1) == pl.num_programs(1) - 1)
    def _():
        s_ref[0] = st_ref[...]


def _gdn_prompt(conv_in, z, a_rows, b_rows, conv_w, alog_row, dt_row, gnorm, nc=8):
    b, l, wc = conv_in.shape
    n = l // CHUNK_A
    nc = min(nc, n)
    t = nc * CHUNK_A
    r = H_A * CHUNK_A
    blk = lambda bi, i: (bi, i, 0)
    kern = functools.partial(_gdn_kernel, nc=nc)
    oa, s = pl.pallas_call(
        kern, grid=(b, n // nc),
        in_specs=[pl.BlockSpec((1, t, wc), blk), pl.BlockSpec((1, t, W_V_A), blk),
                  pl.BlockSpec((1, nc, r), blk), pl.BlockSpec((1, nc, r), blk),
                  _const_spec(conv_w.shape), _const_spec(alog_row.shape), _const_spec(dt_row.shape),
                  _const_spec(gnorm.shape)],
        out_specs=[pl.BlockSpec((1, t, W_V_A), blk),
                   pl.BlockSpec((1, DK_A, H_A * DV_A), lambda bi, i: (bi, 0, 0))],
        out_shape=[jax.ShapeDtypeStruct((b, l, W_V_A), F32),
                   jax.ShapeDtypeStruct((b, DK_A, H_A * DV_A), F32)],
        scratch_shapes=[pltpu.VMEM((t + 8, wc), F32), pltpu.VMEM((DK_A, H_A * DV_A), F32)],
        compiler_params=_cparams("parallel", "arbitrary"), name="gdn_prompt")(
            conv_in, z, a_rows, b_rows, conv_w, alog_row, dt_row, gnorm)
    return oa, s


def _cfm_tail(y, bd_ref, lg_ref, lb_ref, w2_ref, res):
    y = y + bd_ref[...]
    yc = y - jnp.mean(y, axis=-1, keepdims=True)
    yn = yc * lax.rsqrt(jnp.mean(yc * yc, axis=-1, keepdims=True) + EPS) * lg_ref[...] + lb_ref[...]
    return res + jnp.dot(_silu(yn).astype(BF16), w2_ref[...], preferred_element_type=F32)


def _cfm_kernel(u_ref, x_ref, wd_ref, bd_ref, lg_ref, lb_ref, w2_ref, o_ref, xs_ref):
    t = u_ref.shape[1]
    halo = 32

    @pl.when(pl.program_id(1) == 0)
    def _():
        xs_ref[0:halo, :] = jnp.zeros((halo, xs_ref.shape[1]), F32)

    @pl.when(pl.program_id(1) != 0)
    def _():
        xs_ref[0:halo, :] = xs_ref[t:t + halo, :]

    xs_ref[halo:halo + t, :] = u_ref[0]
    y = None
    for jj in range(CONV_C):
        o = halo - (CONV_C - 1) + jj
        term = xs_ref[o:o + t, :] * wd_ref[jj:jj + 1, :]
        y = term if y is None else y + term
    o_ref[0] = _cfm_tail(y, bd_ref, lg_ref, lb_ref, w2_ref, x_ref[0])


def _cfm_prompt(u, x, w_dw, b_dw, ln_g, ln_b, w2, t=256):
    b, l, d = u.shape
    t = min(t, l)
    blk = lambda bi, i: (bi, i, 0)
    return pl.pallas_call(
        _cfm_kernel, grid=(b, l // t),
        in_specs=[pl.BlockSpec((1, t, d), blk), pl.BlockSpec((1, t, d), blk), _const_spec(w_dw.shape),
                  _const_spec(b_dw.shape), _const_spec(ln_g.shape), _const_spec(ln_b.shape),
                  _const_spec(w2.shape)],
        out_specs=pl.BlockSpec((1, t, d), blk),
        out_shape=jax.ShapeDtypeStruct((b, l, d), F32),
        scratch_shapes=[pltpu.VMEM((t + 32, d), F32)],
        compiler_params=_cparams("parallel", "arbitrary"), name="cfm_prompt")(
            u, x, w_dw, b_dw, ln_g, ln_b, w2)


def _cfm_sample_kernel(buf_ref, u_ref, x_ref, wd_ref, bd_ref, lg_ref, lb_ref, w2_ref, o_ref):
    y = u_ref[...] * wd_ref[CONV_C - 1:CONV_C, :]
    for jj in range(CONV_C - 1):
        y = y + buf_ref[jj] * wd_ref[jj:jj + 1, :]
    o_ref[...] = _cfm_tail(y, bd_ref, lg_ref, lb_ref, w2_ref, x_ref[...])


def _cfm_sample(buf_t, u, x, w_dw, b_dw, ln_g, ln_b, w2):
    args = (buf_t, u, x, w_dw, b_dw, ln_g, ln_b, w2)
    return pl.pallas_call(
        _cfm_sample_kernel, grid=(1,), in_specs=[_const_spec(a.shape) for a in args],
        out_specs=_const_spec(x.shape), out_shape=jax.ShapeDtypeStruct(x.shape, F32),
        compiler_params=_cparams("arbitrary"), name="cfm_sample")(*args)


def _rows8(row):
    return jnp.broadcast_to(row, (8, row.shape[1]))


def _gdn_sample_kernel(f_ref, cw_ref, sm_ref, al_ref, dt_ref, z_ref, gn_ref, s_ref, oa_ref, so_ref):
    y = _silu(jnp.sum(f_ref[0] * cw_ref[...], axis=0, keepdims=True))
    sm = sm_ref[0]
    g = -jnp.exp(al_ref[...]) * _softplus(sm + dt_ref[...])
    beta = jax.nn.sigmoid(sm)
    row_id = _iota((8, DK_A), 0)
    outs = []
    for h in range(H_A):
        q = y[:, h * DK_A:(h + 1) * DK_A]
        k = y[:, W_QK_A + h * DK_A:W_QK_A + (h + 1) * DK_A]
        v = y[:, 2 * W_QK_A + h * DV_A:2 * W_QK_A + (h + 1) * DV_A]
        q = q * lax.rsqrt(jnp.sum(q * q, axis=1, keepdims=True) + EPS) * (DK_A ** -0.5)
        k = k * lax.rsqrt(jnp.sum(k * k, axis=1, keepdims=True) + EPS)
        eg = jnp.exp(g[:, SM_A + h:SM_A + h + 1])
        bh = beta[:, SM_B + h:SM_B + h + 1]
        s = s_ref[0, h]
        lhs = jnp.where(row_id == 0, _rows8(k), jnp.where(row_id == 1, _rows8(q), 0.0)).astype(BF16)
        rs = jnp.dot(lhs, s.astype(BF16), preferred_element_type=F32)
        v_new = bh * (v - eg * rs[0:1, :])
        o = eg * rs[1:2, :] + jnp.sum(q * k, axis=1, keepdims=True) * v_new
        so_ref[0, h] = s * eg + _col_from_row(k, DK_A) * v_new
        outs.append(o * lax.rsqrt(jnp.mean(o * o, axis=1, keepdims=True) + EPS) * gn_ref[...])
    oa_ref[0] = jnp.concatenate(outs, axis=1) * _silu(z_ref[0])


def _gdn_sample(full, conv_w, small, alog_row, dt_row, z, gnorm, state):
    db = full.shape[0]
    r3 = lambda i: (i, 0, 0)
    r4 = lambda i: (i, 0, 0, 0)
    return pl.pallas_call(
        _gdn_sample_kernel, grid=(db,),
        in_specs=[pl.BlockSpec((1,) + full.shape[1:], r3), _const_spec(conv_w.shape),
                  pl.BlockSpec((1, 1, LANES), r3), _const_spec(alog_row.shape), _const_spec(dt_row.shape),
                  pl.BlockSpec((1, 1, W_V_A), r3), _const_spec(gnorm.shape),
                  pl.BlockSpec((1,) + state.shape[1:], r4)],
        out_specs=[pl.BlockSpec((1, 1, W_V_A), r3), pl.BlockSpec((1,) + state.shape[1:], r4)],
        out_shape=[jax.ShapeDtypeStruct((db, 1, W_V_A), F32), jax.ShapeDtypeStruct(state.shape, F32)],
        compiler_params=_cparams("parallel"), name="gdn_sample")(
            full, conv_w, small, alog_row, dt_row, z, gnorm, state)


def _xattn_sample_kernel(q_ref, mk_ref, mv_ref, o_ref):
    d = q_ref.shape[2]
    dh = d // H_X
    q = (q_ref[0] * (dh ** -0.5)).astype(BF16)
    outs = []
    for h in range(H_X):
        sl = slice(h * dh, (h + 1) * dh)
        s = _dot_nt(_rows8(q[:, sl]), mk_ref[0, :, sl].astype(BF16))
        p = jnp.exp(s - jnp.max(s, axis=-1, keepdims=True))
        l = jnp.sum(p, axis=-1, keepdims=True)
        o = jnp.dot(p.astype(BF16), mv_ref[0, :, sl].astype(BF16), preferred_element_type=F32)
        outs.append((o * (1.0 / l))[0:1, :])
    o_ref[0] = jnp.concatenate(outs, axis=1)


def _xattn_sample(q, mk, mv):
    db, _, d = q.shape
    nm = mk.shape[1]
    r3 = lambda i: (i, 0, 0)
    return pl.pallas_call(
        _xattn_sample_kernel, grid=(db,),
        in_specs=[pl.BlockSpec((1, 1, d), r3), pl.BlockSpec((1, nm, d), r3), pl.BlockSpec((1, nm, d), r3)],
        out_specs=pl.BlockSpec((1, 1, d), r3), out_shape=jax.ShapeDtypeStruct((db, 1, d), F32),
        compiler_params=_cparams("parallel"), name="xattn_sample")(q, mk, mv)


def _fox_sample_kernel(pt_ref, q_ref, kn_ref, vn_ref, sm_ref, bf_ref, *refs, g_pages):
    k_refs = refs[0:g_pages]
    v_refs = refs[g_pages:2 * g_pages]
    lf_refs = refs[2 * g_pages:3 * g_pages]
    o_ref, lfn_ref, qmt_ref, m_ref, l_ref, acc_ref, carry_ref, pad_ref = refs[3 * g_pages:]
    del pt_ref
    w = q_ref.shape[2]
    gi = pl.program_id(1)
    expand = (_iota((LANES, w), 0) == _iota((LANES, w), 1) // DH_B).astype(BF16)

    @pl.when(gi == 0)
    def _():
        q = q_ref[0] * (DH_B ** -0.5)
        qcol = _col_from_row(q, w)
        head_of_row = _iota((w, LANES), 0) // DH_B
        qmt = jnp.where(_iota((w, LANES), 1) == head_of_row, qcol, 0.0).astype(BF16)
        qmt_ref[...] = qmt
        lane = _iota((1, LANES), 1)
        lfn = jnp.where(lane < H_B, _log_sigmoid(sm_ref[0] + bf_ref[...]), 0.0)
        lfn_ref[0] = lfn
        s_new = jnp.dot(_rows8(kn_ref[0]).astype(BF16), qmt, preferred_element_type=F32)
        m_ref[...] = s_new
        l_ref[...] = jnp.ones_like(l_ref)
        acc_ref[...] = jnp.where(_iota((8, w), 0) == 0, _rows8(vn_ref[0]), 0.0)
        carry_ref[...] = _rows8(lfn)
        pad_ref[...] = jnp.zeros_like(pad_ref)

    qmt = qmt_ref[...]
    later = (_iota((PAGE, PAGE), 1) > _iota((PAGE, PAGE), 0)).astype(BF16)
    carry = carry_ref[0:1, :]
    logits = [None] * g_pages
    m_new = m_ref[0:1, :]
    for jj in reversed(range(g_pages)):
        s = jnp.dot(k_refs[jj][0].astype(BF16), qmt, preferred_element_type=F32)
        pad_ref[:, 0:H_B] = lf_refs[jj][0]
        lf = pad_ref[...]
        lg = s + _dot3_l(later, lf) + carry
        carry = carry + jnp.sum(lf, axis=0, keepdims=True)
        logits[jj] = lg
        m_new = jnp.maximum(m_new, jnp.max(lg, axis=0, keepdims=True))
    alpha = jnp.exp(m_ref[0:1, :] - m_new)
    l_new = l_ref[0:1, :] * alpha
    acc = acc_ref[...] * _dot3_r(_rows8(alpha), expand)
    for jj in range(g_pages):
        p = jnp.exp(logits[jj] - m_new)
        l_new = l_new + jnp.sum(p, axis=0, keepdims=True)
        pe = jnp.dot(p.astype(BF16), expand, preferred_element_type=F32)
        acc = acc + jnp.sum((pe * v_refs[jj][0]).reshape(PAGE // 8, 8, w), axis=0)
    m_ref[...] = _rows8(m_new)
    l_ref[...] = _rows8(l_new)
    acc_ref[...] = acc
    carry_ref[...] = _rows8(carry)

    @pl.when(gi == pl.num_programs(1) - 1)
    def _():
        den = _dot3_r(_rows8(l_new), expand)[0:1, :]
        o_ref[0] = jnp.sum(acc, axis=0, keepdims=True) * (1.0 / den)


def _fox_sample(q, k_new, v_new, small, bf_row, cache_k, cache_v, cache_lf, page_table, g_pages=8):
    db, _, w = q.shape
    n_pages = page_table.shape[1]
    g_pages = min(g_pages, n_pages)
    ng = n_pages // g_pages
    r3 = lambda i, g, pt: (i, 0, 0)

    def page_map(jj):
        return lambda i, g, pt: (pt[i * n_pages + (ng - 1 - g) * g_pages + jj], 0, 0)

    in_specs = [pl.BlockSpec((1, 1, w), r3)] * 3 + [pl.BlockSpec((1, 1, LANES), r3),
                                                    pl.BlockSpec(bf_row.shape, lambda i, g, pt: (0, 0))]
    in_specs += [pl.BlockSpec((1, PAGE, w), page_map(jj)) for jj in range(g_pages)] * 2
    in_specs += [pl.BlockSpec((1, PAGE, H_B), page_map(jj)) for jj in range(g_pages)]
    grid_spec = pltpu.PrefetchScalarGridSpec(
        num_scalar_prefetch=1, grid=(db, ng), in_specs=in_specs,
        out_specs=[pl.BlockSpec((1, 1, w), r3), pl.BlockSpec((1, 1, LANES), r3)],
        scratch_shapes=[pltpu.VMEM((w, LANES), BF16), pltpu.VMEM((8, LANES), F32), pltpu.VMEM((8, LANES), F32),
                        pltpu.VMEM((8, w), F32), pltpu.VMEM((8, LANES), F32), pltpu.VMEM((PAGE, LANES), F32)])
    kern = functools.partial(_fox_sample_kernel, g_pages=g_pages)
    return pl.pallas_call(
        kern, grid_spec=grid_spec,
        out_shape=[jax.ShapeDtypeStruct((db, 1, w), F32), jax.ShapeDtypeStruct((db, 1, LANES), F32)],
        compiler_params=_cparams("parallel", "arbitrary"), name="fox_sample")(
            page_table.reshape(-1), q, k_new, v_new, small, bf_row,
            *([cache_k] * g_pages), *([cache_v] * g_pages), *([cache_lf] * g_pages))


def _row(v):
    return v.reshape(1, -1).astype(F32)


def _pad_lanes(v, start):
    return jnp.zeros((1, LANES), F32).at[0, start:start + v.shape[0]].set(v)


def _chunk_rows(cols, b, l):
    n = l // CHUNK_A
    return cols.reshape(b, n, CHUNK_A, H_A).transpose(0, 1, 3, 2).reshape(b, n, H_A * CHUNK_A)


def kernel(x_prompt, x_sample, mem_prompt, cache_fox_k, cache_fox_v, cache_fox_logf, page_table, state_gdn, state_gdn_conv, state_cfm_conv, cache_mem_k, cache_mem_v, norm_mix, w_in_e, conv_a, a_log, dt_bias, gnorm_a, b_f, w_out_e, w_pw1, b_pw1, w_dw, b_dw, ln_g, ln_b, w_pw2, norm_mem, norm_x, w_xq, w_xkv, w_xo, norm_f, w_up, w_down, norm_out):
    b, l, d = x_prompt.shape
    db = x_sample.shape[0]
    n_mem = mem_prompt.shape[1]
    depth = norm_mix.shape[0]
    dh_x = d // H_X
    bf = lambda w: w.astype(BF16)

    w_in = w_in_e[0]
    off_aa = W_CONV_A + W_V_A
    off_qb = off_aa + 2 * H_A
    off_fb = off_qb + 3 * W_B
    w_small = jnp.concatenate([w_in[:, off_aa:off_qb], w_in[:, off_fb:]], axis=1)
    w_small = jnp.pad(w_small, ((0, 0), (0, LANES - w_small.shape[1])))
    w_in_r = bf(jnp.concatenate([w_in[:, :off_aa], w_in[:, off_qb:off_fb], w_small], axis=1))
    in_widths = (W_CONV_A, W_V_A, W_B, W_B, W_B, LANES)
    w_out_a = bf(w_out_e[0][:W_V_A])
    w_out_b = bf(w_out_e[0][W_V_A:])
    w_xq_b = [bf(w_xq[i]) for i in range(depth)]
    w_xo_b = [bf(w_xo[i]) for i in range(depth)]
    w_up_b = [bf(w_up[i]) for i in range(depth)]
    w_down_b = [bf(w_down[i]) for i in range(depth)]
    w_pw1_b = bf(w_pw1[0])
    w_pw2_b = bf(w_pw2[0])
    alog_rows = _row(jnp.repeat(a_log[0], CHUNK_A))
    dt_rows = _row(jnp.repeat(dt_bias[0], CHUNK_A))
    gn = _row(gnorm_a[0])

    memf = mem_prompt.reshape(b * n_mem, d)
    mem_k, mem_v = [], []
    for i in range(depth):
        mk, mv = _linear([memf], [bf(w_xkv[i])], gain=_row(norm_mem[i]), out_widths=(d, d), name="mem_kv")
        mem_k.append(mk.reshape(b, n_mem, d))
        mem_v.append(mv.reshape(b, n_mem, d))
    mem_k_prompt = jnp.stack(mem_k).reshape(depth, b, n_mem, H_X, dh_x)
    mem_v_prompt = jnp.stack(mem_v).reshape(depth, b, n_mem, H_X, dh_x)

    xp = x_prompt.reshape(b * l, d)
    conv_in, z, q, k, v, small = _linear([xp], [w_in_r], gain=_row(norm_mix[0]), out_widths=in_widths,
                                         name="in_proj")
    fox_k_prompt = k.reshape(1, b, l, H_B, DH_B)
    fox_v_prompt = v.reshape(1, b, l, H_B, DH_B)
    conv_in3 = conv_in.reshape(b, l, W_CONV_A)
    gdn_conv_prompt = conv_in3[:, l - (CONV_A - 1):, :][None]
    oa, s_fin = _gdn_prompt(conv_in3, z.reshape(b, l, W_V_A),
                            _chunk_rows(small[:, SM_A:SM_A + H_A], b, l),
                            _chunk_rows(small[:, SM_B:SM_B + H_A], b, l),
                            conv_a[0], alog_rows, dt_rows, gn)
    gdn_state_prompt = s_fin.reshape(b, DK_A, H_A, DV_A).transpose(0, 2, 1, 3)[None]
    qf, kf, vf, lf = _fox_prep(q.reshape(b, l, W_B), k.reshape(b, l, W_B), v.reshape(b, l, W_B),
                               small.reshape(b, l, LANES), _pad_lanes(b_f[0], SM_F))
    fox_logf_prompt = lf[:, :, SM_F:SM_F + H_B][None]
    ob = _fox_flash(qf, kf, vf)
    x = _linear([oa.reshape(b * l, W_V_A), ob.reshape(b * l, W_B)], [w_out_a, w_out_b], res=xp, name="out_proj")
    x = _xattn_prompt(x.reshape(b, l, d), _row(norm_x[0]), w_xq_b[0], bf(mem_k[0]), bf(mem_v[0]), w_xo_b[0])
    x = _mlp(x.reshape(b * l, d), _row(norm_f[0]), w_up_b[0], w_down_b[0])
    glu = _linear([x], [w_pw1_b], gain=_row(norm_mix[1]), bias=_row(b_pw1[0]), act="glu", name="pw1_glu")
    glu3 = glu.reshape(b, l, d)
    cfm_conv_prompt = glu3[:, l - (CONV_C - 1):, :][None]
    x = _cfm_prompt(glu3, x.reshape(b, l, d), w_dw[0], _row(b_dw[0]), _row(ln_g[0]), _row(ln_b[0]), w_pw2_b)
    x = _xattn_prompt(x, _row(norm_x[1]), w_xq_b[1], bf(mem_k[1]), bf(mem_v[1]), w_xo_b[1])
    y_prompt = _mlp(x.reshape(b * l, d), _row(norm_f[1]), w_up_b[1], w_down_b[1],
                    final_gain=_row(norm_out)).reshape(b, l, d)

    xs = x_sample.reshape(db, d)
    conv_s, z_s, q_s, k_s, v_s, small_s = _linear([xs], [w_in_r], gain=_row(norm_mix[0]),
                                                  out_widths=in_widths, name="in_proj_s")
    fox_k_sample = k_s.reshape(1, db, 1, H_B, DH_B)
    fox_v_sample = v_s.reshape(1, db, 1, H_B, DH_B)
    full = jnp.concatenate([state_gdn_conv[0], conv_s[:, None, :]], axis=1)
    gdn_conv_sample = full[:, 1:, :][None]
    oa_s, s_new = _gdn_sample(full, conv_a[0], small_s.reshape(db, 1, LANES), _pad_lanes(a_log[0], SM_A),
                              _pad_lanes(dt_bias[0], SM_A), z_s.reshape(db, 1, W_V_A), gn, state_gdn[0])
    gdn_state_sample = s_new[None]
    n_pool = cache_fox_k.shape[1]
    f_s = jnp.pad(small_s[:, SM_F:SM_F + H_B], ((0, 0), (0, LANES - H_B))).reshape(db, 1, LANES)
    ob_s, lf_s = _fox_sample(q_s.reshape(db, 1, W_B), k_s.reshape(db, 1, W_B), v_s.reshape(db, 1, W_B),
                             f_s, _pad_lanes(b_f[0], 0),
                             cache_fox_k[0].reshape(n_pool, PAGE, W_B), cache_fox_v[0].reshape(n_pool, PAGE, W_B),
                             cache_fox_logf[0], page_table)
    fox_logf_sample = lf_s[:, :, 0:H_B].reshape(1, db, 1, H_B)
    x = _linear([oa_s.reshape(db, W_V_A), ob_s.reshape(db, W_B)], [w_out_a, w_out_b], res=xs, name="out_proj_s")

    def xattn_s(x, i):
        qx = _linear([x], [w_xq_b[i]], gain=_row(norm_x[i]), name="xq_s")
        o = _xattn_sample(qx.reshape(db, 1, d), cache_mem_k[i].reshape(db, n_mem, d),
                          cache_mem_v[i].reshape(db, n_mem, d))
        return _linear([o.reshape(db, d)], [w_xo_b[i]], res=x, name="xo_s")

    x = xattn_s(x, 0)
    x = _mlp(x, _row(norm_f[0]), w_up_b[0], w_down_b[0])
    glu_s = _linear([x], [w_pw1_b], gain=_row(norm_mix[1]), bias=_row(b_pw1[0]), act="glu", name="pw1_glu_s")
    cfm_conv_sample = jnp.concatenate([state_cfm_conv[0][:, 1:, :], glu_s[:, None, :]], axis=1)[None]
    x = _cfm_sample(state_cfm_conv[0].transpose(1, 0, 2), glu_s, x, w_dw[0], _row(b_dw[0]), _row(ln_g[0]),
                    _row(ln_b[0]), w_pw2_b)
    x = xattn_s(x, 1)
    y_sample = _mlp(x, _row(norm_f[1]), w_up_b[1], w_down_b[1], final_gain=_row(norm_out)).reshape(db, 1, d)

    return (y_prompt, y_sample, fox_k_prompt, fox_v_prompt, fox_logf_prompt, fox_k_sample, fox_v_sample,
            fox_logf_sample, gdn_state_prompt, gdn_conv_prompt, gdn_state_sample, gdn_conv_sample,
            cfm_conv_prompt, cfm_conv_sample, mem_k_prompt, mem_v_prompt)
```

```python
import functools
import math

import jax
import jax.numpy as jnp
import numpy as np
from jax import lax
from jax.experimental import pallas as pl
from jax.experimental.pallas import tpu as pltpu

F32 = jnp.float32
BF16 = jnp.bfloat16
EPS = 1e-6
NEG = -1e30
LOG2E = 1.4426950408889634

H_A = 4
DK_A = 128
DV_A = 128
CONV_A = 4
CHUNK_A = 64
H_B = 8
DH_B = 64
H_X = 4
CONV_C = 31
PAGE = 128

W_QK_A = H_A * DK_A
W_V_A = H_A * DV_A
W_CONV_A = 2 * W_QK_A + W_V_A
W_B = H_B * DH_B
LANES = 128
SM_A = 0
SM_B = H_A
SM_F = 2 * H_A

VMEM_LIMIT = 56 * 1024 * 1024


def _cparams(*sem):
    return pltpu.CompilerParams(dimension_semantics=sem, vmem_limit_bytes=VMEM_LIMIT)


def _const_spec(shape):
    nd = len(shape)
    return pl.BlockSpec(shape, lambda *_: (0,) * nd)


def _rms(x, g):
    return x * lax.rsqrt(jnp.mean(x * x, axis=-1, keepdims=True) + EPS) * g


def _silu(x):
    return x * jax.nn.sigmoid(x)


def _softplus(x):
    return jnp.maximum(x, 0.0) + jnp.log1p(jnp.exp(-jnp.abs(x)))


def _split3(x):
    h = x.astype(BF16)
    r = x - h.astype(F32)
    m = r.astype(BF16)
    l = (r - m.astype(F32)).astype(BF16)
    return h, m, l


def _dot3_l(mat01, x):
    h, m, l = _split3(x)
    d = lambda t: jnp.dot(mat01, t, preferred_element_type=F32)
    return d(h) + d(m) + d(l)


def _dot3_r(x, mat01):
    h, m, l = _split3(x)
    d = lambda t: jnp.dot(t, mat01, preferred_element_type=F32)
    return d(h) + d(m) + d(l)


def _dot_nt(a, b):
    return lax.dot_general(a, b, (((1,), (1,)), ((), ())), preferred_element_type=F32)


def _dot_tn(a, b):
    return lax.dot_general(a, b, (((0,), (0,)), ((), ())), preferred_element_type=F32)


def _iota(shape, dim):
    return lax.broadcasted_iota(jnp.int32, shape, dim)


def _col_from_row(row, n):
    eye = _iota((n, n), 0) == _iota((n, n), 1)
    return jnp.sum(jnp.where(eye, jnp.broadcast_to(row, (n, n)), 0.0), axis=1, keepdims=True)


def _linear_kernel(*refs, n_in, has_gain, has_bias, has_res, act, out_widths, chunk):
    it = iter(refs)
    x_refs = [next(it) for _ in range(n_in)]
    w_refs = [next(it) for _ in range(n_in)]
    gain_ref = next(it) if has_gain else None
    bias_ref = next(it) if has_bias else None
    res_ref = next(it) if has_res else None
    out_refs = list(it)
    xs = []
    for i, xr in enumerate(x_refs):
        x = xr[...]
        if has_gain and i == 0:
            x = _rms(x, gain_ref[...])
        xs.append(x.astype(BF16))
    n_total = sum(out_widths)

    def mm(col0, cw):
        acc = None
        for xb, wr in zip(xs, w_refs):
            d = jnp.dot(xb, wr[:, col0:col0 + cw], preferred_element_type=F32)
            acc = d if acc is None else acc + d
        if has_bias:
            acc = acc + bias_ref[:, col0:col0 + cw]
        return acc

    off = 0
    for o_ref, width in zip(out_refs, out_widths):
        for c0 in range(0, width, chunk):
            cw = min(chunk, width - c0)
            y = mm(off + c0, cw)
            if act == "glu":
                y = y * jax.nn.sigmoid(mm(n_total + off + c0, cw))
            if has_res:
                y = y + res_ref[:, off + c0:off + c0 + cw]
            o_ref[:, c0:c0 + cw] = y
        off += width


def _linear(xs, ws, *, gain=None, bias=None, res=None, act=None, out_widths=None, tm=512,
            chunk=512, name="linear"):
    m = xs[0].shape[0]
    tm = min(tm, m)
    assert m % tm == 0
    n_mm = ws[0].shape[1]
    n_out = n_mm // 2 if act == "glu" else n_mm
    if out_widths is None:
        out_widths = (n_out,)
    assert sum(out_widths) == n_out
    row = lambda i: (i, 0)
    in_specs = [pl.BlockSpec((tm, x.shape[1]), row) for x in xs]
    in_specs += [_const_spec(w.shape) for w in ws]
    args = list(xs) + list(ws)
    if gain is not None:
        in_specs.append(_const_spec(gain.shape)); args.append(gain)
    if bias is not None:
        in_specs.append(_const_spec(bias.shape)); args.append(bias)
    if res is not None:
        in_specs.append(pl.BlockSpec((tm, res.shape[1]), row)); args.append(res)
    out_shape = [jax.ShapeDtypeStruct((m, w), F32) for w in out_widths]
    out_specs = [pl.BlockSpec((tm, w), row) for w in out_widths]
    kern = functools.partial(_linear_kernel, n_in=len(xs), has_gain=gain is not None,
                             has_bias=bias is not None, has_res=res is not None, act=act,
                             out_widths=tuple(out_widths), chunk=chunk)
    outs = pl.pallas_call(kern, grid=(m // tm,), in_specs=in_specs, out_specs=out_specs,
                          out_shape=out_shape, compiler_params=_cparams("parallel"), name=name)(*args)
    return outs if len(outs) > 1 else outs[0]


def _mlp_kernel(*refs, chunk, final_norm):
    if final_norm:
        x_ref, g_ref, wu_ref, wd_ref, go_ref, o_ref = refs
    else:
        x_ref, g_ref, wu_ref, wd_ref, o_ref = refs
    x = x_ref[...]
    xn = _rms(x, g_ref[...]).astype(BF16)
    acc = x
    d_ff = wu_ref.shape[1]
    for c0 in range(0, d_ff, chunk):
        h = jnp.dot(xn, wu_ref[:, c0:c0 + chunk], preferred_element_type=F32)
        h = jnp.square(jnp.maximum(h, 0.0)).astype(BF16)
        acc = acc + jnp.dot(h, wd_ref[c0:c0 + chunk, :], preferred_element_type=F32)
    if final_norm:
        acc = _rms(acc, go_ref[...])
    o_ref[...] = acc


def _mlp(x, gain, w_up, w_down, final_gain=None, tm=512, chunk=512):
    m, d = x.shape
    tm = min(tm, m)
    row = lambda i: (i, 0)
    in_specs = [pl.BlockSpec((tm, d), row), _const_spec(gain.shape), _const_spec(w_up.shape),
                _const_spec(w_down.shape)]
    args = [x, gain, w_up, w_down]
    if final_gain is not None:
        in_specs.append(_const_spec(final_gain.shape)); args.append(final_gain)
    kern = functools.partial(_mlp_kernel, chunk=chunk, final_norm=final_gain is not None)
    return pl.pallas_call(kern, grid=(m // tm,), in_specs=in_specs,
                          out_specs=pl.BlockSpec((tm, d), row),
                          out_shape=jax.ShapeDtypeStruct((m, d), F32),
                          compiler_params=_cparams("parallel"), name="mlp")(*args)


def _xattn_kernel(x_ref, g_ref, wq_ref, mk_ref, mv_ref, wo_ref, o_ref):
    x = x_ref[0]
    d = x.shape[1]
    dh = d // H_X
    xn = _rms(x, g_ref[...]).astype(BF16)
    q = jnp.dot(xn, wq_ref[...], preferred_element_type=F32) * (dh ** -0.5)
    q = q.astype(BF16)
    outs = []
    for h in range(H_X):
        sl = slice(h * dh, (h + 1) * dh)
        s = _dot_nt(q[:, sl], mk_ref[0, :, sl])
        p = jnp.exp(s - jnp.max(s, axis=-1, keepdims=True))
        l = jnp.sum(p, axis=-1, keepdims=True)
        o = jnp.dot(p.astype(BF16), mv_ref[0, :, sl], preferred_element_type=F32)
        outs.append((o * (1.0 / l)).astype(BF16))
    o = jnp.concatenate(outs, axis=1)
    o_ref[0] = x + jnp.dot(o, wo_ref[...], preferred_element_type=F32)


def _xattn_prompt(x, gain, wq, mk, mv, wo, tm=512):
    b, l, d = x.shape
    tm = min(tm, l)
    nm = mk.shape[1]
    blk = lambda bi, i: (bi, i, 0)
    mem = lambda bi, i: (bi, 0, 0)
    return pl.pallas_call(
        _xattn_kernel, grid=(b, l // tm),
        in_specs=[pl.BlockSpec((1, tm, d), blk), _const_spec(gain.shape), _const_spec(wq.shape),
                  pl.BlockSpec((1, nm, d), mem), pl.BlockSpec((1, nm, d), mem), _const_spec(wo.shape)],
        out_specs=pl.BlockSpec((1, tm, d), blk),
        out_shape=jax.ShapeDtypeStruct((b, l, d), F32),
        compiler_params=_cparams("parallel", "parallel"), name="xattn_prompt")(x, gain, wq, mk, mv, wo)


def _log_sigmoid(x):
    return jnp.minimum(x, 0.0) - jnp.log1p(jnp.exp(-jnp.abs(x)))


def _head_lanes(x, h):
    p = h // 2
    blk = x[:, p * LANES:(p + 1) * LANES]
    if h % 2:
        blk = pltpu.roll(blk, DH_B, 1)
    return blk


def _fox_prep_kernel(q_ref, k_ref, v_ref, sm_ref, bf_ref, qf_ref, kf_ref, vf_ref, lf_ref, carry_ref):
    t = q_ref.shape[1]

    @pl.when(pl.program_id(1) == 0)
    def _():
        carry_ref[...] = jnp.zeros_like(carry_ref)

    lane = _iota((t, LANES), 1)
    gate_lane = (lane >= SM_F) & (lane < SM_F + H_B)
    logf = jnp.where(gate_lane, _log_sigmoid(sm_ref[0] + bf_ref[...]), 0.0)
    lf_ref[0] = logf
    tri = (_iota((t, t), 0) >= _iota((t, t), 1)).astype(BF16)
    c = _dot3_l(tri, logf) + carry_ref[0:1, :]
    carry_ref[0:1, :] = c[t - 1:t, :]
    c1, c2, c3 = _split3(c * LOG2E)
    c1, c2, c3 = c1.astype(F32), c2.astype(F32), c3.astype(F32)
    q = q_ref[0] * (DH_B ** -0.5 * LOG2E)
    k = k_ref[0]
    v = v_ref[0]
    low = lane < DH_B
    for h in range(H_B):
        col = slice(SM_F + h, SM_F + h + 1)
        b1, b2, b3 = c1[:, col], c2[:, col], c3[:, col]
        qx = jnp.where(lane == DH_B, b1, jnp.where(lane == DH_B + 1, b2, jnp.where(
            lane == DH_B + 2, b3, jnp.where(lane < DH_B + 6, 1.0, 0.0))))
        kx = jnp.where(lane < DH_B + 3, 1.0, jnp.where(lane == DH_B + 3, -b1, jnp.where(
            lane == DH_B + 4, -b2, jnp.where(lane == DH_B + 5, -b3, 0.0))))
        vx = jnp.where(lane == DH_B, 1.0, 0.0)
        qf_ref[0, h] = jnp.where(low, _head_lanes(q, h), qx).astype(BF16)
        kf_ref[0, h] = jnp.where(low, _head_lanes(k, h), kx).astype(BF16)
        vf_ref[0, h] = jnp.where(low, _head_lanes(v, h), vx).astype(BF16)


def _fox_prep(q, k, v, small, bf_row, t=256):
    b, l, w = q.shape
    t = min(t, l)
    blk = lambda bi, i: (bi, i, 0)
    hblk = lambda bi, i: (bi, 0, i, 0)
    hshape = jax.ShapeDtypeStruct((b, H_B, l, LANES), BF16)
    return pl.pallas_call(
        _fox_prep_kernel, grid=(b, l // t),
        in_specs=[pl.BlockSpec((1, t, w), blk)] * 3 + [pl.BlockSpec((1, t, LANES), blk),
                                                       _const_spec(bf_row.shape)],
        out_specs=[pl.BlockSpec((1, H_B, t, LANES), hblk)] * 3 + [pl.BlockSpec((1, t, LANES), blk)],
        out_shape=[hshape, hshape, hshape, jax.ShapeDtypeStruct((b, l, LANES), F32)],
        scratch_shapes=[pltpu.VMEM((8, LANES), F32)],
        compiler_params=_cparams("parallel", "arbitrary"), name="fox_prep")(q, k, v, small, bf_row)


def _fox_flash_kernel(it_ref, jt_ref, qf_ref, kf_ref, vf_ref, o_ref, m_ref, acc_ref):
    p_id = pl.program_id(1)
    i = it_ref[p_id]
    j = jt_ref[p_id]
    tq = qf_ref.shape[2]
    tk = kf_ref.shape[2]

    @pl.when(j == 0)
    def _():
        m_ref[...] = jnp.full_like(m_ref, NEG)
        acc_ref[...] = jnp.zeros_like(acc_ref)

    def heads(masked):
        def scores(h):
            s = _dot_nt(qf_ref[0, h], kf_ref[0, h])
            if masked:
                s = jnp.where(_iota((tq, tk), 1) <= _iota((tq, tk), 0), s, NEG)
            return s

        s_next = scores(0)
        for h in range(H_B):
            s = s_next
            if h + 1 < H_B:
                s_next = scores(h + 1)
            m_prev = m_ref[h]
            m_new = jnp.maximum(m_prev, jnp.max(s, axis=-1, keepdims=True))
            alpha = jnp.exp2(m_prev - m_new)
            p = jnp.exp2(s - pltpu.repeat(m_new, tk // LANES, axis=1)).astype(BF16)
            acc_ref[h] = alpha * acc_ref[h] + jnp.dot(p, vf_ref[0, h], preferred_element_type=F32)
            m_ref[h] = m_new

    @pl.when(j < i)
    def _():
        heads(False)

    @pl.when(j == i)
    def _():
        heads(True)
        lane = _iota((tq, LANES), 1)
        for pr in range(H_B // 2):
            a0 = acc_ref[2 * pr]
            a1 = acc_ref[2 * pr + 1]
            o0 = a0 * (1.0 / a0[:, DH_B:DH_B + 1])
            o1 = a1 * (1.0 / a1[:, DH_B:DH_B + 1])
            o_ref[0, :, pr * LANES:(pr + 1) * LANES] = jnp.where(lane < DH_B, o0, pltpu.roll(o1, DH_B, 1))


def _fox_flash(qf, kf, vf, t=512):
    b, hh, l, _ = qf.shape
    t = min(t, l)
    n = l // t
    pairs = [(i, j) for i in range(n) for j in range(i + 1)]
    it = jnp.asarray(np.array([p[0] for p in pairs], np.int32))
    jt = jnp.asarray(np.array([p[1] for p in pairs], np.int32))
    qmap = lambda bi, p, it, jt: (bi, 0, it[p], 0)
    kmap = lambda bi, p, it, jt: (bi, 0, jt[p], 0)
    omap = lambda bi, p, it, jt: (bi, it[p], 0)
    grid_spec = pltpu.PrefetchScalarGridSpec(
        num_scalar_prefetch=2, grid=(b, len(pairs)),
        in_specs=[pl.BlockSpec((1, hh, t, LANES), qmap), pl.BlockSpec((1, hh, t, LANES), kmap),
                  pl.BlockSpec((1, hh, t, LANES), kmap)],
        out_specs=pl.BlockSpec((1, t, hh * DH_B), omap),
        scratch_shapes=[pltpu.VMEM((hh, t, LANES), F32), pltpu.VMEM((hh, t, LANES), F32)])
    return pl.pallas_call(
        _fox_flash_kernel, grid_spec=grid_spec,
        out_shape=jax.ShapeDtypeStruct((b, l, hh * DH_B), F32),
        compiler_params=_cparams("parallel", "arbitrary"), name="fox_flash")(it, jt, qf, kf, vf)


def _gdn_kernel(x_ref, z_ref, a_ref, b_ref, cw_ref, al_ref, dt_ref, gn_ref, oa_ref, s_ref,
                xs_ref, st_ref, *, nc):
    c = CHUNK_A
    r = H_A * c
    t = nc * c
    halo = 8

    @pl.when(pl.program_id(1) == 0)
    def _():
        xs_ref[0:halo, :] = jnp.zeros((halo, xs_ref.shape[1]), F32)
        st_ref[...] = jnp.zeros_like(st_ref)

    @pl.when(pl.program_id(1) != 0)
    def _():
        xs_ref[0:halo, :] = xs_ref[t:t + halo, :]

    xs_ref[halo:halo + t, :] = x_ref[0]
    y = None
    for jj in range(CONV_A):
        o = halo - (CONV_A - 1) + jj
        term = xs_ref[o:o + t, :] * cw_ref[jj:jj + 1, :]
        y = term if y is None else y + term
    y = _silu(y)

    g_rows = -jnp.exp(al_ref[...]) * _softplus(a_ref[0] + dt_ref[...])
    beta_rows = jax.nn.sigmoid(b_ref[0])
    ri = _iota((r, r), 0)
    ci = _iota((r, r), 1)
    same = (ri // c) == (ci // c)
    incl = same & (ri >= ci)
    strict = same & (ri > ci)
    gc_rows = _dot3_r(g_rows, (same & (ri <= ci)).astype(BF16))
    eye = ri == ci
    last = ((ci % c) == (c - 1)) & same
    lane4 = _iota((r, H_A * DV_A), 1) // DV_A
    row4 = _iota((r, H_A * DV_A), 0) // c
    diag4 = lane4 == row4

    def stack(a):
        return jnp.concatenate([a[:, h * DK_A:(h + 1) * DK_A] for h in range(H_A)], axis=0)

    sel = ((_iota((r, H_A * DV_A), 0) % c) == (c - 1)) & diag4
    pre = []
    for n in range(nc):
        rows = slice(n * c, (n + 1) * c)
        gc_r = gc_rows[n:n + 1, :]
        gcb = jnp.broadcast_to(gc_r, (r, r))
        gc_c = jnp.sum(jnp.where(eye, gcb, 0.0), axis=1, keepdims=True)
        gl_c = jnp.sum(jnp.where(last, gcb, 0.0), axis=1, keepdims=True)
        beta_c = jnp.sum(jnp.where(eye, jnp.broadcast_to(beta_rows[n:n + 1, :], (r, r)), 0.0),
                         axis=1, keepdims=True)
        q = stack(y[rows, 0:W_QK_A])
        k = stack(y[rows, W_QK_A:2 * W_QK_A])
        v = stack(y[rows, 2 * W_QK_A:W_CONV_A])
        q = q * lax.rsqrt(jnp.sum(q * q, axis=1, keepdims=True) + EPS) * (DK_A ** -0.5)
        k = k * lax.rsqrt(jnp.sum(k * k, axis=1, keepdims=True) + EPS)
        decay = jnp.exp(jnp.where(incl, gc_c - gc_r, -jnp.inf))
        kb = k * beta_c
        k16 = k.astype(BF16)
        lmat = jnp.where(strict, _dot_nt(kb.astype(BF16), k16) * decay, 0.0)
        a_qk = jnp.where(incl, _dot_nt(q.astype(BF16), k16) * decay, 0.0)
        mneg = -lmat
        tinv = jnp.where(eye, 1.0, 0.0) + mneg
        for _ in range(int(math.log2(c)) - 1):
            mneg = jnp.dot(mneg.astype(BF16), mneg.astype(BF16), preferred_element_type=F32)
            tinv = tinv + jnp.dot(tinv.astype(BF16), mneg.astype(BF16), preferred_element_type=F32)
        eg = jnp.exp(gc_c)
        rhs = jnp.concatenate([v * beta_c, kb * eg], axis=1).astype(BF16)
        uw = jnp.dot(tinv.astype(BF16), rhs, preferred_element_type=F32)
        dec_l = jnp.sum(jnp.where(sel, jnp.broadcast_to(gl_c, (r, H_A * DV_A)), 0.0), axis=0, keepdims=True)
        pre.append(dict(u=uw[:, 0:DV_A], w=uw[:, DV_A:2 * DV_A].astype(BF16), qg=(q * eg).astype(BF16),
                        a_qk=a_qk.astype(BF16), kdec=(k * jnp.exp(gl_c - gc_c)).astype(BF16),
                        dec=jnp.exp(dec_l)))

    def diag(a):
        a = jnp.where(diag4, a, 0.0)
        return (a[:, 0:DV_A] + a[:, DV_A:2 * DV_A]) + (a[:, 2 * DV_A:3 * DV_A] + a[:, 3 * DV_A:4 * DV_A])

    s_all = st_ref[...]
    for n in range(nc):
        rows = slice(n * c, (n + 1) * c)
        pc = pre[n]
        s16 = s_all.astype(BF16)
        ws = jnp.dot(pc["w"], s16, preferred_element_type=F32)
        qs = jnp.dot(pc["qg"], s16, preferred_element_type=F32)
        v_new = pc["u"] - diag(ws)
        o = diag(qs) + jnp.dot(pc["a_qk"], v_new.astype(BF16), preferred_element_type=F32)
        v_exp = jnp.where(diag4, jnp.concatenate([v_new] * H_A, axis=1), 0.0).astype(BF16)
        s_all = s_all * pc["dec"] + _dot_tn(pc["kdec"], v_exp)
        on = o * lax.rsqrt(jnp.mean(o * o, axis=1, keepdims=True) + EPS) * gn_ref[...]
        on = jnp.concatenate([on[h * c:(h + 1) * c, :] for h in range(H_A)], axis=1)
        oa_ref[0, rows, :] = on * _silu(z_ref[0, rows, :])
    st_ref[...] = s_all

    @pl.when(pl.program_id(1) == pl.num_programs(1) - 1)
    def _():
        s_ref[0] = st_ref[...]


def _gdn_prompt(conv_in, z, a_rows, b_rows, conv_w, alog_row, dt_row, gnorm, nc=8):
    b, l, wc = conv_in.shape
    n = l // CHUNK_A
    nc = min(nc, n)
    t = nc * CHUNK_A
    r = H_A * CHUNK_A
    blk = lambda bi, i: (bi, i, 0)
    kern = functools.partial(_gdn_kernel, nc=nc)
    oa, s = pl.pallas_call(
        kern, grid=(b, n // nc),
        in_specs=[pl.BlockSpec((1, t, wc), blk), pl.BlockSpec((1, t, W_V_A), blk),
                  pl.BlockSpec((1, nc, r), blk), pl.BlockSpec((1, nc, r), blk),
                  _const_spec(conv_w.shape), _const_spec(alog_row.shape), _const_spec(dt_row.shape),
                  _const_spec(gnorm.shape)],
        out_specs=[pl.BlockSpec((1, t, W_V_A), blk),
                   pl.BlockSpec((1, DK_A, H_A * DV_A), lambda bi, i: (bi, 0, 0))],
        out_shape=[jax.ShapeDtypeStruct((b, l, W_V_A), F32),
                   jax.ShapeDtypeStruct((b, DK_A, H_A * DV_A), F32)],
        scratch_shapes=[pltpu.VMEM((t + 8, wc), F32), pltpu.VMEM((DK_A, H_A * DV_A), F32)],
        compiler_params=_cparams("parallel", "arbitrary"), name="gdn_prompt")(
            conv_in, z, a_rows, b_rows, conv_w, alog_row, dt_row, gnorm)
    return oa, s


def _cfm_tail(y, bd_ref, lg_ref, lb_ref, w2_ref, res):
    y = y + bd_ref[...]
    yc = y - jnp.mean(y, axis=-1, keepdims=True)
    yn = yc * lax.rsqrt(jnp.mean(yc * yc, axis=-1, keepdims=True) + EPS) * lg_ref[...] + lb_ref[...]
    return res + jnp.dot(_silu(yn).astype(BF16), w2_ref[...], preferred_element_type=F32)


def _cfm_kernel(u_ref, x_ref, wd_ref, bd_ref, lg_ref, lb_ref, w2_ref, o_ref, xs_ref, sh_ref):
    t = u_ref.shape[1]
    halo = 32

    @pl.when(pl.program_id(1) == 0)
    def _():
        xs_ref[0:halo, :] = jnp.zeros((halo, xs_ref.shape[1]), F32)

    @pl.when(pl.program_id(1) != 0)
    def _():
        xs_ref[0:halo, :] = xs_ref[t:t + halo, :]

    xs_ref[halo:halo + t, :] = u_ref[0]
    y = None
    for r in range(8):
        offs = [o for o in range(halo - (CONV_C - 1), halo + 1) if o % 8 == r]
        if not offs:
            continue
        if r:
            n_rows = max(offs) - r + t
            sh_ref[0:n_rows, :] = xs_ref[r:r + n_rows, :]
        win_ref = sh_ref if r else xs_ref
        for o in offs:
            jj = o - (halo - (CONV_C - 1))
            term = win_ref[o - r:o - r + t, :] * wd_ref[jj:jj + 1, :]
            y = term if y is None else y + term
    o_ref[0] = _cfm_tail(y, bd_ref, lg_ref, lb_ref, w2_ref, x_ref[0])


def _cfm_prompt(u, x, w_dw, b_dw, ln_g, ln_b, w2, t=256):
    b, l, d = u.shape
    t = min(t, l)
    blk = lambda bi, i: (bi, i, 0)
    return pl.pallas_call(
        _cfm_kernel, grid=(b, l // t),
        in_specs=[pl.BlockSpec((1, t, d), blk), pl.BlockSpec((1, t, d), blk), _const_spec(w_dw.shape),
                  _const_spec(b_dw.shape), _const_spec(ln_g.shape), _const_spec(ln_b.shape),
                  _const_spec(w2.shape)],
        out_specs=pl.BlockSpec((1, t, d), blk),
        out_shape=jax.ShapeDtypeStruct((b, l, d), F32),
        scratch_shapes=[pltpu.VMEM((t + 32, d), F32), pltpu.VMEM((t + 32, d), F32)],
        compiler_params=_cparams("parallel", "arbitrary"), name="cfm_prompt")(
            u, x, w_dw, b_dw, ln_g, ln_b, w2)


def _cfm_sample_kernel(buf_ref, u_ref, x_ref, wd_ref, bd_ref, lg_ref, lb_ref, w2_ref, o_ref):
    y = u_ref[...] * wd_ref[CONV_C - 1:CONV_C, :]
    for jj in range(CONV_C - 1):
        y = y + buf_ref[jj] * wd_ref[jj:jj + 1, :]
    o_ref[...] = _cfm_tail(y, bd_ref, lg_ref, lb_ref, w2_ref, x_ref[...])


def _cfm_sample(buf_t, u, x, w_dw, b_dw, ln_g, ln_b, w2):
    args = (buf_t, u, x, w_dw, b_dw, ln_g, ln_b, w2)
    return pl.pallas_call(
        _cfm_sample_kernel, grid=(1,), in_specs=[_const_spec(a.shape) for a in args],
        out_specs=_const_spec(x.shape), out_shape=jax.ShapeDtypeStruct(x.shape, F32),
        compiler_params=_cparams("arbitrary"), name="cfm_sample")(*args)


def _rows8(row):
    return jnp.broadcast_to(row, (8, row.shape[1]))


def _gdn_sample_kernel(f_ref, cw_ref, sm_ref, al_ref, dt_ref, z_ref, gn_ref, s_ref, oa_ref, so_ref):
    y = _silu(jnp.sum(f_ref[0] * cw_ref[...], axis=0, keepdims=True))
    sm = sm_ref[0]
    g = -jnp.exp(al_ref[...]) * _softplus(sm + dt_ref[...])
    beta = jax.nn.sigmoid(sm)
    row_id = _iota((8, DK_A), 0)
    outs = []
    for h in range(H_A):
        q = y[:, h * DK_A:(h + 1) * DK_A]
        k = y[:, W_QK_A + h * DK_A:W_QK_A + (h + 1) * DK_A]
        v = y[:, 2 * W_QK_A + h * DV_A:2 * W_QK_A + (h + 1) * DV_A]
        q = q * lax.rsqrt(jnp.sum(q * q, axis=1, keepdims=True) + EPS) * (DK_A ** -0.5)
        k = k * lax.rsqrt(jnp.sum(k * k, axis=1, keepdims=True) + EPS)
        eg = jnp.exp(g[:, SM_A + h:SM_A + h + 1])
        bh = beta[:, SM_B + h:SM_B + h + 1]
        s = s_ref[0, h]
        lhs = jnp.where(row_id == 0, _rows8(k), jnp.where(row_id == 1, _rows8(q), 0.0)).astype(BF16)
        rs = jnp.dot(lhs, s.astype(BF16), preferred_element_type=F32)
        v_new = bh * (v - eg * rs[0:1, :])
        o = eg * rs[1:2, :] + jnp.sum(q * k, axis=1, keepdims=True) * v_new
        so_ref[0, h] = s * eg + _col_from_row(k, DK_A) * v_new
        outs.append(o * lax.rsqrt(jnp.mean(o * o, axis=1, keepdims=True) + EPS) * gn_ref[...])
    oa_ref[0] = jnp.concatenate(outs, axis=1) * _silu(z_ref[0])


def _gdn_sample(full, conv_w, small, alog_row, dt_row, z, gnorm, state):
    db = full.shape[0]
    r3 = lambda i: (i, 0, 0)
    r4 = lambda i: (i, 0, 0, 0)
    return pl.pallas_call(
        _gdn_sample_kernel, grid=(db,),
        in_specs=[pl.BlockSpec((1,) + full.shape[1:], r3), _const_spec(conv_w.shape),
                  pl.BlockSpec((1, 1, LANES), r3), _const_spec(alog_row.shape), _const_spec(dt_row.shape),
                  pl.BlockSpec((1, 1, W_V_A), r3), _const_spec(gnorm.shape),
                  pl.BlockSpec((1,) + state.shape[1:], r4)],
        out_specs=[pl.BlockSpec((1, 1, W_V_A), r3), pl.BlockSpec((1,) + state.shape[1:], r4)],
        out_shape=[jax.ShapeDtypeStruct((db, 1, W_V_A), F32), jax.ShapeDtypeStruct(state.shape, F32)],
        compiler_params=_cparams("parallel"), name="gdn_sample")(
            full, conv_w, small, alog_row, dt_row, z, gnorm, state)


def _xattn_sample_kernel(q_ref, mk_ref, mv_ref, o_ref):
    d = q_ref.shape[2]
    dh = d // H_X
    q = (q_ref[0] * (dh ** -0.5)).astype(BF16)
    outs = []
    for h in range(H_X):
        sl = slice(h * dh, (h + 1) * dh)
        s = _dot_nt(_rows8(q[:, sl]), mk_ref[0, :, sl].astype(BF16))
        p = jnp.exp(s - jnp.max(s, axis=-1, keepdims=True))
        l = jnp.sum(p, axis=-1, keepdims=True)
        o = jnp.dot(p.astype(BF16), mv_ref[0, :, sl].astype(BF16), preferred_element_type=F32)
        outs.append((o * (1.0 / l))[0:1, :])
    o_ref[0] = jnp.concatenate(outs, axis=1)


def _xattn_sample(q, mk, mv):
    db, _, d = q.shape
    nm = mk.shape[1]
    r3 = lambda i: (i, 0, 0)
    return pl.pallas_call(
        _xattn_sample_kernel, grid=(db,),
        in_specs=[pl.BlockSpec((1, 1, d), r3), pl.BlockSpec((1, nm, d), r3), pl.BlockSpec((1, nm, d), r3)],
        out_specs=pl.BlockSpec((1, 1, d), r3), out_shape=jax.ShapeDtypeStruct((db, 1, d), F32),
        compiler_params=_cparams("parallel"), name="xattn_sample")(q, mk, mv)


def _fox_sample_kernel(pt_ref, q_ref, kn_ref, vn_ref, f_ref, bf_ref, *refs, g_pages):
    k_refs = refs[0:g_pages]
    v_refs = refs[g_pages:2 * g_pages]
    lf_refs = refs[2 * g_pages:3 * g_pages]
    o_ref, lfn_ref, m_ref, l_ref, acc_ref, carry_ref, pad_ref = refs[3 * g_pages:]
    del pt_ref
    gi = pl.program_id(1)
    q = q_ref[0] * (DH_B ** -0.5)
    bcast = lambda col: jnp.broadcast_to(col, (H_B, LANES))

    @pl.when(gi == 0)
    def _():
        lfn = _log_sigmoid(f_ref[0] + bf_ref[...])
        lfn_ref[0] = lfn
        m_ref[...] = bcast(jnp.sum(q * kn_ref[0], axis=-1, keepdims=True))
        l_ref[...] = jnp.ones_like(l_ref)
        acc_ref[...] = vn_ref[0]
        carry_ref[...] = bcast(lfn)
        pad_ref[...] = jnp.zeros_like(pad_ref)

    later = (_iota((PAGE, PAGE), 0) > _iota((PAGE, PAGE), 1)).astype(BF16)
    ones = jnp.ones((LANES, LANES), BF16)
    own_lane = _iota((PAGE, H_B, LANES), 0) == _iota((PAGE, H_B, LANES), 2)
    m = m_ref[:, 0:1]
    l = l_ref[:, 0:1]
    carry = carry_ref[:, 0:1]
    logits = [None] * g_pages
    top = None
    for jj in reversed(range(g_pages)):
        red = jnp.sum(k_refs[jj][...] * q[None], axis=-1, keepdims=True)
        s_t = jnp.sum(jnp.where(own_lane, red, 0.0), axis=0)
        pad_ref[:, 0:H_B] = lf_refs[jj][...]
        lf_t = pad_ref[...].T[0:H_B, :]
        logits[jj] = s_t + _dot3_r(lf_t, later) + carry
        carry = carry + jnp.sum(lf_t, axis=1, keepdims=True)
        top = logits[jj] if top is None else jnp.maximum(top, logits[jj])
    m_new = jnp.maximum(m, jnp.max(top, axis=1, keepdims=True))
    alpha = jnp.exp(m - m_new)
    p_sum = None
    pv = None
    for jj in range(g_pages):
        p_t = jnp.exp(logits[jj] - m_new)
        p_sum = p_t if p_sum is None else p_sum + p_t
        z = jnp.where(own_lane, p_t[None], 0.0).reshape(PAGE * H_B, LANES).astype(BF16)
        p3 = jnp.dot(z, ones, preferred_element_type=F32).reshape(PAGE, H_B, LANES)[:, :, 0:DH_B]
        part = jnp.sum(p3 * v_refs[jj][...], axis=0)
        pv = part if pv is None else pv + part
    l = alpha * l + jnp.sum(p_sum, axis=1, keepdims=True)
    acc = alpha * acc_ref[...] + pv
    m = m_new
    m_ref[...] = bcast(m)
    l_ref[...] = bcast(l)
    carry_ref[...] = bcast(carry)
    acc_ref[...] = acc

    @pl.when(gi == pl.num_programs(1) - 1)
    def _():
        o_ref[0] = acc * (1.0 / l)


def _fox_sample(q, k_new, v_new, f_col, bf_col, cache_k, cache_v, cache_lf, page_table, g_pages=8):
    db = q.shape[0]
    n_pages = page_table.shape[1]
    g_pages = min(g_pages, n_pages)
    ng = n_pages // g_pages
    r3 = lambda i, g, pt: (i, 0, 0)

    def page_map(nd):
        return [(lambda i, g, pt, jj=jj: (pt[i * n_pages + (ng - 1 - g) * g_pages + jj],) + (0,) * nd)
                for jj in range(g_pages)]

    in_specs = [pl.BlockSpec((1, H_B, DH_B), r3)] * 3 + [pl.BlockSpec((1, H_B, 1), r3),
                                                         pl.BlockSpec(bf_col.shape, lambda i, g, pt: (0, 0))]
    in_specs += [pl.BlockSpec((None, PAGE, H_B, DH_B), mp) for mp in page_map(3)] * 2
    in_specs += [pl.BlockSpec((None, PAGE, H_B), mp) for mp in page_map(2)]
    grid_spec = pltpu.PrefetchScalarGridSpec(
        num_scalar_prefetch=1, grid=(db, ng), in_specs=in_specs,
        out_specs=[pl.BlockSpec((1, H_B, DH_B), r3), pl.BlockSpec((1, H_B, 1), r3)],
        scratch_shapes=[pltpu.VMEM((H_B, LANES), F32), pltpu.VMEM((H_B, LANES), F32),
                        pltpu.VMEM((H_B, DH_B), F32), pltpu.VMEM((H_B, LANES), F32),
                        pltpu.VMEM((PAGE, LANES), F32)])
    kern = functools.partial(_fox_sample_kernel, g_pages=g_pages)
    return pl.pallas_call(
        kern, grid_spec=grid_spec,
        out_shape=[jax.ShapeDtypeStruct((db, H_B, DH_B), F32), jax.ShapeDtypeStruct((db, H_B, 1), F32)],
        compiler_params=_cparams("parallel", "arbitrary"), name="fox_sample")(
            page_table.reshape(-1), q, k_new, v_new, f_col, bf_col,
            *([cache_k] * g_pages), *([cache_v] * g_pages), *([cache_lf] * g_pages))


def _row(v):
    return v.reshape(1, -1).astype(F32)


def _pad_lanes(v, start):
    return jnp.zeros((1, LANES), F32).at[0, start:start + v.shape[0]].set(v)


def _chunk_rows(cols, b, l):
    n = l // CHUNK_A
    return cols.reshape(b, n, CHUNK_A, H_A).transpose(0, 1, 3, 2).reshape(b, n, H_A * CHUNK_A)


def kernel(x_prompt, x_sample, mem_prompt, cache_fox_k, cache_fox_v, cache_fox_logf, page_table, state_gdn, state_gdn_conv, state_cfm_conv, cache_mem_k, cache_mem_v, norm_mix, w_in_e, conv_a, a_log, dt_bias, gnorm_a, b_f, w_out_e, w_pw1, b_pw1, w_dw, b_dw, ln_g, ln_b, w_pw2, norm_mem, norm_x, w_xq, w_xkv, w_xo, norm_f, w_up, w_down, norm_out):
    b, l, d = x_prompt.shape
    db = x_sample.shape[0]
    n_mem = mem_prompt.shape[1]
    depth = norm_mix.shape[0]
    dh_x = d // H_X
    bf = lambda w: w.astype(BF16)

    w_in = w_in_e[0]
    off_aa = W_CONV_A + W_V_A
    off_qb = off_aa + 2 * H_A
    off_fb = off_qb + 3 * W_B
    w_small = jnp.concatenate([w_in[:, off_aa:off_qb], w_in[:, off_fb:]], axis=1)
    w_small = jnp.pad(w_small, ((0, 0), (0, LANES - w_small.shape[1])))
    w_in_r = bf(jnp.concatenate([w_in[:, :off_aa], w_in[:, off_qb:off_fb], w_small], axis=1))
    in_widths = (W_CONV_A, W_V_A, W_B, W_B, W_B, LANES)
    w_out_a = bf(w_out_e[0][:W_V_A])
    w_out_b = bf(w_out_e[0][W_V_A:])
    w_xq_b = [bf(w_xq[i]) for i in range(depth)]
    w_xo_b = [bf(w_xo[i]) for i in range(depth)]
    w_up_b = [bf(w_up[i]) for i in range(depth)]
    w_down_b = [bf(w_down[i]) for i in range(depth)]
    w_pw1_b = bf(w_pw1[0])
    w_pw2_b = bf(w_pw2[0])
    alog_rows = _row(jnp.repeat(a_log[0], CHUNK_A))
    dt_rows = _row(jnp.repeat(dt_bias[0], CHUNK_A))
    gn = _row(gnorm_a[0])

    memf = mem_prompt.reshape(b * n_mem, d)
    mem_k, mem_v = [], []
    for i in range(depth):
        mk, mv = _linear([memf], [bf(w_xkv[i])], gain=_row(norm_mem[i]), out_widths=(d, d), name="mem_kv")
        mem_k.append(mk.reshape(b, n_mem, d))
        mem_v.append(mv.reshape(b, n_mem, d))
    mem_k_prompt = jnp.stack(mem_k).reshape(depth, b, n_mem, H_X, dh_x)
    mem_v_prompt = jnp.stack(mem_v).reshape(depth, b, n_mem, H_X, dh_x)

    xp = x_prompt.reshape(b * l, d)
    conv_in, z, q, k, v, small = _linear([xp], [w_in_r], gain=_row(norm_mix[0]), out_widths=in_widths,
                                         name="in_proj")
    fox_k_prompt = k.reshape(1, b, l, H_B, DH_B)
    fox_v_prompt = v.reshape(1, b, l, H_B, DH_B)
    conv_in3 = conv_in.reshape(b, l, W_CONV_A)
    gdn_conv_prompt = conv_in3[:, l - (CONV_A - 1):, :][None]
    oa, s_fin = _gdn_prompt(conv_in3, z.reshape(b, l, W_V_A),
                            _chunk_rows(small[:, SM_A:SM_A + H_A], b, l),
                            _chunk_rows(small[:, SM_B:SM_B + H_A], b, l),
                            conv_a[0], alog_rows, dt_rows, gn)
    gdn_state_prompt = s_fin.reshape(b, DK_A, H_A, DV_A).transpose(0, 2, 1, 3)[None]
    qf, kf, vf, lf = _fox_prep(q.reshape(b, l, W_B), k.reshape(b, l, W_B), v.reshape(b, l, W_B),
                               small.reshape(b, l, LANES), _pad_lanes(b_f[0], SM_F))
    fox_logf_prompt = lf[:, :, SM_F:SM_F + H_B][None]
    ob = _fox_flash(qf, kf, vf)
    x = _linear([oa.reshape(b * l, W_V_A), ob.reshape(b * l, W_B)], [w_out_a, w_out_b], res=xp, name="out_proj")
    x = _xattn_prompt(x.reshape(b, l, d), _row(norm_x[0]), w_xq_b[0], bf(mem_k[0]), bf(mem_v[0]), w_xo_b[0])
    x = _mlp(x.reshape(b * l, d), _row(norm_f[0]), w_up_b[0], w_down_b[0])
    glu = _linear([x], [w_pw1_b], gain=_row(norm_mix[1]), bias=_row(b_pw1[0]), act="glu", name="pw1_glu")
    glu3 = glu.reshape(b, l, d)
    cfm_conv_prompt = glu3[:, l - (CONV_C - 1):, :][None]
    x = _cfm_prompt(glu3, x.reshape(b, l, d), w_dw[0], _row(b_dw[0]), _row(ln_g[0]), _row(ln_b[0]), w_pw2_b)
    x = _xattn_prompt(x, _row(norm_x[1]), w_xq_b[1], bf(mem_k[1]), bf(mem_v[1]), w_xo_b[1])
    y_prompt = _mlp(x.reshape(b * l, d), _row(norm_f[1]), w_up_b[1], w_down_b[1],
                    final_gain=_row(norm_out)).reshape(b, l, d)

    xs = x_sample.reshape(db, d)
    conv_s, z_s, q_s, k_s, v_s, small_s = _linear([xs], [w_in_r], gain=_row(norm_mix[0]),
                                                  out_widths=in_widths, name="in_proj_s")
    fox_k_sample = k_s.reshape(1, db, 1, H_B, DH_B)
    fox_v_sample = v_s.reshape(1, db, 1, H_B, DH_B)
    full = jnp.concatenate([state_gdn_conv[0], conv_s[:, None, :]], axis=1)
    gdn_conv_sample = full[:, 1:, :][None]
    oa_s, s_new = _gdn_sample(full, conv_a[0], small_s.reshape(db, 1, LANES), _pad_lanes(a_log[0], SM_A),
                              _pad_lanes(dt_bias[0], SM_A), z_s.reshape(db, 1, W_V_A), gn, state_gdn[0])
    gdn_state_sample = s_new[None]
    ob_s, lf_s = _fox_sample(q_s.reshape(db, H_B, DH_B), k_s.reshape(db, H_B, DH_B), v_s.reshape(db, H_B, DH_B),
                             small_s[:, SM_F:SM_F + H_B].reshape(db, H_B, 1), b_f[0].reshape(H_B, 1),
                             cache_fox_k[0], cache_fox_v[0], cache_fox_logf[0], page_table)
    fox_logf_sample = lf_s.reshape(1, db, 1, H_B)
    x = _linear([oa_s.reshape(db, W_V_A), ob_s.reshape(db, W_B)], [w_out_a, w_out_b], res=xs, name="out_proj_s")

    def xattn_s(x, i):
        qx = _linear([x], [w_xq_b[i]], gain=_row(norm_x[i]), name="xq_s")
        o = _xattn_sample(qx.reshape(db, 1, d), cache_mem_k[i].reshape(db, n_mem, d),
                          cache_mem_v[i].reshape(db, n_mem, d))
        return _linear([o.reshape(db, d)], [w_xo_b[i]], res=x, name="xo_s")

    x = xattn_s(x, 0)
    x = _mlp(x, _row(norm_f[0]), w_up_b[0], w_down_b[0])
    glu_s = _linear([x], [w_pw1_b], gain=_row(norm_mix[1]), bias=_row(b_pw1[0]), act="glu", name="pw1_glu_s")
    cfm_conv_sample = jnp.concatenate([state_cfm_conv[0][:, 1:, :], glu_s[:, None, :]], axis=1)[None]
    x = _cfm_sample(state_cfm_conv[0].transpose(1, 0, 2), glu_s, x, w_dw[0], _row(b_dw[0]), _row(ln_g[0]),
                    _row(ln_b[0]), w_pw2_b)
    x = xattn_s(x, 1)
    y_sample = _mlp(x, _row(norm_f[1]), w_up_b[1], w_down_b[1], final_gain=_row(norm_out)).reshape(db, 1, d)

    return (y_prompt, y_sample, fox_k_prompt, fox_v_prompt, fox_logf_prompt, fox_k_sample, fox_v_sample,
            fox_logf_sample, gdn_state_prompt, gdn_conv_prompt, gdn_state_sample, gdn_conv_sample,
            cfm_conv_prompt, cfm_conv_sample, mem_k_prompt, mem_v_prompt)
```

```python
import functools
import math

import jax
import jax.numpy as jnp
import numpy as np
from jax import lax
from jax.experimental import pallas as pl
from jax.experimental.pallas import tpu as pltpu

F32 = jnp.float32
BF16 = jnp.bfloat16
EPS = 1e-6
NEG = -1e30
LOG2E = 1.4426950408889634

H_A = 4
DK_A = 128
DV_A = 128
CONV_A = 4
CHUNK_A = 64
H_B = 8
DH_B = 64
H_X = 4
CONV_C = 31
PAGE = 128

W_QK_A = H_A * DK_A
W_V_A = H_A * DV_A
W_CONV_A = 2 * W_QK_A + W_V_A
W_B = H_B * DH_B
LANES = 128
SM_A = 0
SM_B = H_A
SM_F = 2 * H_A

VMEM_LIMIT = 56 * 1024 * 1024


def _cparams(*sem):
    return pltpu.CompilerParams(dimension_semantics=sem, vmem_limit_bytes=VMEM_LIMIT)


def _const_spec(shape):
    nd = len(shape)
    return pl.BlockSpec(shape, lambda *_: (0,) * nd)


def _rms(x, g):
    return x * lax.rsqrt(jnp.mean(x * x, axis=-1, keepdims=True) + EPS) * g


def _silu(x):
    return x * jax.nn.sigmoid(x)


def _softplus(x):
    return jnp.maximum(x, 0.0) + jnp.log1p(jnp.exp(-jnp.abs(x)))


def _split3(x):
    h = x.astype(BF16)
    r = x - h.astype(F32)
    m = r.astype(BF16)
    l = (r - m.astype(F32)).astype(BF16)
    return h, m, l


def _dot3_l(mat01, x):
    h, m, l = _split3(x)
    d = lambda t: jnp.dot(mat01, t, preferred_element_type=F32)
    return d(h) + d(m) + d(l)


def _dot3_r(x, mat01):
    h, m, l = _split3(x)
    d = lambda t: jnp.dot(t, mat01, preferred_element_type=F32)
    return d(h) + d(m) + d(l)


def _dot_nt(a, b):
    return lax.dot_general(a, b, (((1,), (1,)), ((), ())), preferred_element_type=F32)


def _dot_tn(a, b):
    return lax.dot_general(a, b, (((0,), (0,)), ((), ())), preferred_element_type=F32)


def _iota(shape, dim):
    return lax.broadcasted_iota(jnp.int32, shape, dim)


def _col_from_row(row, n):
    eye = _iota((n, n), 0) == _iota((n, n), 1)
    return jnp.sum(jnp.where(eye, jnp.broadcast_to(row, (n, n)), 0.0), axis=1, keepdims=True)


def _linear_kernel(*refs, n_in, has_gain, has_bias, has_res, act, out_widths, chunk):
    it = iter(refs)
    x_refs = [next(it) for _ in range(n_in)]
    w_refs = [next(it) for _ in range(n_in)]
    gain_ref = next(it) if has_gain else None
    bias_ref = next(it) if has_bias else None
    res_ref = next(it) if has_res else None
    out_refs = list(it)
    xs = []
    for i, xr in enumerate(x_refs):
        x = xr[...]
        if has_gain and i == 0:
            x = _rms(x, gain_ref[...])
        xs.append(x.astype(BF16))
    n_total = sum(out_widths)

    def mm(col0, cw):
        acc = None
        for xb, wr in zip(xs, w_refs):
            d = jnp.dot(xb, wr[:, col0:col0 + cw], preferred_element_type=F32)
            acc = d if acc is None else acc + d
        if has_bias:
            acc = acc + bias_ref[:, col0:col0 + cw]
        return acc

    off = 0
    for o_ref, width in zip(out_refs, out_widths):
        for c0 in range(0, width, chunk):
            cw = min(chunk, width - c0)
            y = mm(off + c0, cw)
            if act == "glu":
                y = y * jax.nn.sigmoid(mm(n_total + off + c0, cw))
            if has_res:
                y = y + res_ref[:, off + c0:off + c0 + cw]
            o_ref[:, c0:c0 + cw] = y
        off += width


def _linear(xs, ws, *, gain=None, bias=None, res=None, act=None, out_widths=None, tm=512,
            chunk=512, name="linear"):
    m = xs[0].shape[0]
    tm = min(tm, m)
    assert m % tm == 0
    n_mm = ws[0].shape[1]
    n_out = n_mm // 2 if act == "glu" else n_mm
    if out_widths is None:
        out_widths = (n_out,)
    assert sum(out_widths) == n_out
    row = lambda i: (i, 0)
    in_specs = [pl.BlockSpec((tm, x.shape[1]), row) for x in xs]
    in_specs += [_const_spec(w.shape) for w in ws]
    args = list(xs) + list(ws)
    if gain is not None:
        in_specs.append(_const_spec(gain.shape)); args.append(gain)
    if bias is not None:
        in_specs.append(_const_spec(bias.shape)); args.append(bias)
    if res is not None:
        in_specs.append(pl.BlockSpec((tm, res.shape[1]), row)); args.append(res)
    out_shape = [jax.ShapeDtypeStruct((m, w), F32) for w in out_widths]
    out_specs = [pl.BlockSpec((tm, w), row) for w in out_widths]
    kern = functools.partial(_linear_kernel, n_in=len(xs), has_gain=gain is not None,
                             has_bias=bias is not None, has_res=res is not None, act=act,
                             out_widths=tuple(out_widths), chunk=chunk)
    outs = pl.pallas_call(kern, grid=(m // tm,), in_specs=in_specs, out_specs=out_specs,
                          out_shape=out_shape, compiler_params=_cparams("parallel"), name=name)(*args)
    return outs if len(outs) > 1 else outs[0]


def _mlp_kernel(*refs, chunk, final_norm):
    if final_norm:
        x_ref, g_ref, wu_ref, wd_ref, go_ref, o_ref = refs
    else:
        x_ref, g_ref, wu_ref, wd_ref, o_ref = refs
    x = x_ref[...]
    xn = _rms(x, g_ref[...]).astype(BF16)
    acc = x
    d_ff = wu_ref.shape[1]
    for c0 in range(0, d_ff, chunk):
        h = jnp.dot(xn, wu_ref[:, c0:c0 + chunk], preferred_element_type=F32)
        h = jnp.square(jnp.maximum(h, 0.0)).astype(BF16)
        acc = acc + jnp.dot(h, wd_ref[c0:c0 + chunk, :], preferred_element_type=F32)
    if final_norm:
        acc = _rms(acc, go_ref[...])
    o_ref[...] = acc


def _mlp(x, gain, w_up, w_down, final_gain=None, tm=512, chunk=512):
    m, d = x.shape
    tm = min(tm, m)
    row = lambda i: (i, 0)
    in_specs = [pl.BlockSpec((tm, d), row), _const_spec(gain.shape), _const_spec(w_up.shape),
                _const_spec(w_down.shape)]
    args = [x, gain, w_up, w_down]
    if final_gain is not None:
        in_specs.append(_const_spec(final_gain.shape)); args.append(final_gain)
    kern = functools.partial(_mlp_kernel, chunk=chunk, final_norm=final_gain is not None)
    return pl.pallas_call(kern, grid=(m // tm,), in_specs=in_specs,
                          out_specs=pl.BlockSpec((tm, d), row),
                          out_shape=jax.ShapeDtypeStruct((m, d), F32),
                          compiler_params=_cparams("parallel"), name="mlp")(*args)


def _xattn_kernel(x_ref, g_ref, wq_ref, mk_ref, mv_ref, wo_ref, o_ref):
    x = x_ref[0]
    d = x.shape[1]
    dh = d // H_X
    xn = _rms(x, g_ref[...]).astype(BF16)
    q = jnp.dot(xn, wq_ref[...], preferred_element_type=F32) * (dh ** -0.5)
    q = q.astype(BF16)
    outs = []
    for h in range(H_X):
        sl = slice(h * dh, (h + 1) * dh)
        s = _dot_nt(q[:, sl], mk_ref[0, :, sl])
        p = jnp.exp(s - jnp.max(s, axis=-1, keepdims=True))
        l = jnp.sum(p, axis=-1, keepdims=True)
        o = jnp.dot(p.astype(BF16), mv_ref[0, :, sl], preferred_element_type=F32)
        outs.append((o * (1.0 / l)).astype(BF16))
    o = jnp.concatenate(outs, axis=1)
    o_ref[0] = x + jnp.dot(o, wo_ref[...], preferred_element_type=F32)


def _xattn_prompt(x, gain, wq, mk, mv, wo, tm=512):
    b, l, d = x.shape
    tm = min(tm, l)
    nm = mk.shape[1]
    blk = lambda bi, i: (bi, i, 0)
    mem = lambda bi, i: (bi, 0, 0)
    return pl.pallas_call(
        _xattn_kernel, grid=(b, l // tm),
        in_specs=[pl.BlockSpec((1, tm, d), blk), _const_spec(gain.shape), _const_spec(wq.shape),
                  pl.BlockSpec((1, nm, d), mem), pl.BlockSpec((1, nm, d), mem), _const_spec(wo.shape)],
        out_specs=pl.BlockSpec((1, tm, d), blk),
        out_shape=jax.ShapeDtypeStruct((b, l, d), F32),
        compiler_params=_cparams("parallel", "parallel"), name="xattn_prompt")(x, gain, wq, mk, mv, wo)


def _log_sigmoid(x):
    return jnp.minimum(x, 0.0) - jnp.log1p(jnp.exp(-jnp.abs(x)))


def _head_lanes(x, h):
    p = h // 2
    blk = x[:, p * LANES:(p + 1) * LANES]
    if h % 2:
        blk = pltpu.roll(blk, DH_B, 1)
    return blk


def _fox_prep_kernel(q_ref, k_ref, v_ref, sm_ref, bf_ref, qf_ref, kf_ref, vf_ref, lf_ref, carry_ref):
    t = q_ref.shape[1]

    @pl.when(pl.program_id(1) == 0)
    def _():
        carry_ref[...] = jnp.zeros_like(carry_ref)

    lane = _iota((t, LANES), 1)
    gate_lane = (lane >= SM_F) & (lane < SM_F + H_B)
    logf = jnp.where(gate_lane, _log_sigmoid(sm_ref[0] + bf_ref[...]), 0.0)
    lf_ref[0] = logf
    tri = (_iota((t, t), 0) >= _iota((t, t), 1)).astype(BF16)
    c = _dot3_l(tri, logf) + carry_ref[0:1, :]
    carry_ref[0:1, :] = c[t - 1:t, :]
    c1, c2, c3 = _split3(c * LOG2E)
    c1, c2, c3 = c1.astype(F32), c2.astype(F32), c3.astype(F32)
    q = q_ref[0] * (DH_B ** -0.5 * LOG2E)
    k = k_ref[0]
    v = v_ref[0]
    low = lane < DH_B
    for h in range(H_B):
        col = slice(SM_F + h, SM_F + h + 1)
        b1, b2, b3 = c1[:, col], c2[:, col], c3[:, col]
        qx = jnp.where(lane == DH_B, b1, jnp.where(lane == DH_B + 1, b2, jnp.where(
            lane == DH_B + 2, b3, jnp.where(lane < DH_B + 6, 1.0, 0.0))))
        kx = jnp.where(lane < DH_B + 3, 1.0, jnp.where(lane == DH_B + 3, -b1, jnp.where(
            lane == DH_B + 4, -b2, jnp.where(lane == DH_B + 5, -b3, 0.0))))
        vx = jnp.where(lane == DH_B, 1.0, 0.0)
        qf_ref[0, h] = jnp.where(low, _head_lanes(q, h), qx).astype(BF16)
        kf_ref[0, h] = jnp.where(low, _head_lanes(k, h), kx).astype(BF16)
        vf_ref[0, h] = jnp.where(low, _head_lanes(v, h), vx).astype(BF16)


def _fox_prep(q, k, v, small, bf_row, t=256):
    b, l, w = q.shape
    t = min(t, l)
    blk = lambda bi, i: (bi, i, 0)
    hblk = lambda bi, i: (bi, 0, i, 0)
    hshape = jax.ShapeDtypeStruct((b, H_B, l, LANES), BF16)
    return pl.pallas_call(
        _fox_prep_kernel, grid=(b, l // t),
        in_specs=[pl.BlockSpec((1, t, w), blk)] * 3 + [pl.BlockSpec((1, t, LANES), blk),
                                                       _const_spec(bf_row.shape)],
        out_specs=[pl.BlockSpec((1, H_B, t, LANES), hblk)] * 3 + [pl.BlockSpec((1, t, LANES), blk)],
        out_shape=[hshape, hshape, hshape, jax.ShapeDtypeStruct((b, l, LANES), F32)],
        scratch_shapes=[pltpu.VMEM((8, LANES), F32)],
        compiler_params=_cparams("parallel", "arbitrary"), name="fox_prep")(q, k, v, small, bf_row)


def _fox_flash_kernel(it_ref, jt_ref, qf_ref, kf_ref, vf_ref, o_ref, m_ref, acc_ref):
    p_id = pl.program_id(1)
    i = it_ref[p_id]
    j = jt_ref[p_id]
    tq = qf_ref.shape[2]
    tk = kf_ref.shape[2]

    @pl.when(j == 0)
    def _():
        m_ref[...] = jnp.full_like(m_ref, NEG)
        acc_ref[...] = jnp.zeros_like(acc_ref)

    def heads(masked):
        def scores(h):
            s = _dot_nt(qf_ref[0, h], kf_ref[0, h])
            if masked:
                s = jnp.where(_iota((tq, tk), 1) <= _iota((tq, tk), 0), s, NEG)
            return s

        s_next = scores(0)
        for h in range(H_B):
            s = s_next
            if h + 1 < H_B:
                s_next = scores(h + 1)
            m_prev = m_ref[h]
            m_new = jnp.maximum(m_prev, jnp.max(s, axis=-1, keepdims=True))
            alpha = jnp.exp2(m_prev - m_new)
            p = jnp.exp2(s - jnp.concatenate([m_new] * (tk // LANES), axis=1)).astype(BF16)
            acc_ref[h] = alpha * acc_ref[h] + jnp.dot(p, vf_ref[0, h], preferred_element_type=F32)
            m_ref[h] = m_new

    @pl.when(j < i)
    def _():
        heads(False)

    @pl.when(j == i)
    def _():
        heads(True)
        lane = _iota((tq, LANES), 1)
        for pr in range(H_B // 2):
            a0 = acc_ref[2 * pr]
            a1 = acc_ref[2 * pr + 1]
            o0 = a0 * (1.0 / a0[:, DH_B:DH_B + 1])
            o1 = a1 * (1.0 / a1[:, DH_B:DH_B + 1])
            o_ref[0, :, pr * LANES:(pr + 1) * LANES] = jnp.where(lane < DH_B, o0, pltpu.roll(o1, DH_B, 1))


def _fox_flash(qf, kf, vf, t=512):
    b, hh, l, _ = qf.shape
    t = min(t, l)
    n = l // t
    pairs = [(i, j) for i in range(n) for j in range(i + 1)]
    it = jnp.asarray(np.array([p[0] for p in pairs], np.int32))
    jt = jnp.asarray(np.array([p[1] for p in pairs], np.int32))
    qmap = lambda bi, p, it, jt: (bi, 0, it[p], 0)
    kmap = lambda bi, p, it, jt: (bi, 0, jt[p], 0)
    omap = lambda bi, p, it, jt: (bi, it[p], 0)
    grid_spec = pltpu.PrefetchScalarGridSpec(
        num_scalar_prefetch=2, grid=(b, len(pairs)),
        in_specs=[pl.BlockSpec((1, hh, t, LANES), qmap), pl.BlockSpec((1, hh, t, LANES), kmap),
                  pl.BlockSpec((1, hh, t, LANES), kmap)],
        out_specs=pl.BlockSpec((1, t, hh * DH_B), omap),
        scratch_shapes=[pltpu.VMEM((hh, t, LANES), F32), pltpu.VMEM((hh, t, LANES), F32)])
    return pl.pallas_call(
        _fox_flash_kernel, grid_spec=grid_spec,
        out_shape=jax.ShapeDtypeStruct((b, l, hh * DH_B), F32),
        compiler_params=_cparams("parallel", "arbitrary"), name="fox_flash")(it, jt, qf, kf, vf)


def _gdn_kernel(x_ref, z_ref, a_ref, b_ref, cw_ref, al_ref, dt_ref, gn_ref, oa_ref, s_ref,
                xs_ref, st_ref, *, nc):
    c = CHUNK_A
    r = H_A * c
    t = nc * c
    halo = 8

    @pl.when(pl.program_id(1) == 0)
    def _():
        xs_ref[0:halo, :] = jnp.zeros((halo, xs_ref.shape[1]), F32)
        st_ref[...] = jnp.zeros_like(st_ref)

    @pl.when(pl.program_id(1) != 0)
    def _():
        xs_ref[0:halo, :] = xs_ref[t:t + halo, :]

    xs_ref[halo:halo + t, :] = x_ref[0]
    y = None
    for jj in range(CONV_A):
        o = halo - (CONV_A - 1) + jj
        term = xs_ref[o:o + t, :] * cw_ref[jj:jj + 1, :]
        y = term if y is None else y + term
    y = _silu(y)

    g_rows = -jnp.exp(al_ref[...]) * _softplus(a_ref[0] + dt_ref[...])
    beta_rows = jax.nn.sigmoid(b_ref[0])
    ri = _iota((r, r), 0)
    ci = _iota((r, r), 1)
    same = (ri // c) == (ci // c)
    incl = same & (ri >= ci)
    strict = same & (ri > ci)
    gc_rows = _dot3_r(g_rows, (same & (ri <= ci)).astype(BF16))
    eye = ri == ci
    last = ((ci % c) == (c - 1)) & same
    lane4 = _iota((r, H_A * DV_A), 1) // DV_A
    row4 = _iota((r, H_A * DV_A), 0) // c
    diag4 = lane4 == row4

    def stack(a):
        return jnp.concatenate([a[:, h * DK_A:(h + 1) * DK_A] for h in range(H_A)], axis=0)

    sel = ((_iota((r, H_A * DV_A), 0) % c) == (c - 1)) & diag4
    pre = []
    for n in range(nc):
        rows = slice(n * c, (n + 1) * c)
        gc_r = gc_rows[n:n + 1, :]
        gcb = jnp.broadcast_to(gc_r, (r, r))
        gc_c = jnp.sum(jnp.where(eye, gcb, 0.0), axis=1, keepdims=True)
        gl_c = jnp.sum(jnp.where(last, gcb, 0.0), axis=1, keepdims=True)
        beta_c = jnp.sum(jnp.where(eye, jnp.broadcast_to(beta_rows[n:n + 1, :], (r, r)), 0.0),
                         axis=1, keepdims=True)
        q = stack(y[rows, 0:W_QK_A])
        k = stack(y[rows, W_QK_A:2 * W_QK_A])
        v = stack(y[rows, 2 * W_QK_A:W_CONV_A])
        q = q * lax.rsqrt(jnp.sum(q * q, axis=1, keepdims=True) + EPS) * (DK_A ** -0.5)
        k = k * lax.rsqrt(jnp.sum(k * k, axis=1, keepdims=True) + EPS)
        decay = jnp.exp(jnp.where(incl, gc_c - gc_r, -jnp.inf))
        kb = k * beta_c
        k16 = k.astype(BF16)
        lmat = jnp.where(strict, _dot_nt(kb.astype(BF16), k16) * decay, 0.0)
        a_qk = jnp.where(incl, _dot_nt(q.astype(BF16), k16) * decay, 0.0)
        mneg = -lmat
        tinv = jnp.where(eye, 1.0, 0.0) + mneg
        for _ in range(int(math.log2(c)) - 1):
            mneg = jnp.dot(mneg.astype(BF16), mneg.astype(BF16), preferred_element_type=F32)
            tinv = tinv + jnp.dot(tinv.astype(BF16), mneg.astype(BF16), preferred_element_type=F32)
        eg = jnp.exp(gc_c)
        rhs = jnp.concatenate([v * beta_c, kb * eg], axis=1).astype(BF16)
        uw = jnp.dot(tinv.astype(BF16), rhs, preferred_element_type=F32)
        dec_l = jnp.sum(jnp.where(sel, jnp.broadcast_to(gl_c, (r, H_A * DV_A)), 0.0), axis=0, keepdims=True)
        pre.append(dict(u=uw[:, 0:DV_A], w=uw[:, DV_A:2 * DV_A].astype(BF16), qg=(q * eg).astype(BF16),
                        a_qk=a_qk.astype(BF16), kdec=(k * jnp.exp(gl_c - gc_c)).astype(BF16),
                        dec=jnp.exp(dec_l)))

    def diag(a):
        a = jnp.where(diag4, a, 0.0)
        return (a[:, 0:DV_A] + a[:, DV_A:2 * DV_A]) + (a[:, 2 * DV_A:3 * DV_A] + a[:, 3 * DV_A:4 * DV_A])

    s_all = st_ref[...]
    for n in range(nc):
        rows = slice(n * c, (n + 1) * c)
        pc = pre[n]
        s16 = s_all.astype(BF16)
        ws = jnp.dot(pc["w"], s16, preferred_element_type=F32)
        qs = jnp.dot(pc["qg"], s16, preferred_element_type=F32)
        v_new = pc["u"] - diag(ws)
        o = diag(qs) + jnp.dot(pc["a_qk"], v_new.astype(BF16), preferred_element_type=F32)
        v_exp = jnp.where(diag4, jnp.concatenate([v_new] * H_A, axis=1), 0.0).astype(BF16)
        s_all = s_all * pc["dec"] + _dot_tn(pc["kdec"], v_exp)
        on = o * lax.rsqrt(jnp.mean(o * o, axis=1, keepdims=True) + EPS) * gn_ref[...]
        on = jnp.concatenate([on[h * c:(h + 1) * c, :] for h in range(H_A)], axis=1)
        oa_ref[0, rows, :] = on * _silu(z_ref[0, rows, :])
    st_ref[...] = s_all

    @pl.when(pl.program_id(1) == pl.num_programs(1) - 1)
    def _():
        s_ref[0] = st_ref[...]


def _gdn_prompt(conv_in, z, a_rows, b_rows, conv_w, alog_row, dt_row, gnorm, nc=8):
    b, l, wc = conv_in.shape
    n = l // CHUNK_A
    nc = min(nc, n)
    t = nc * CHUNK_A
    r = H_A * CHUNK_A
    blk = lambda bi, i: (bi, i, 0)
    kern = functools.partial(_gdn_kernel, nc=nc)
    oa, s = pl.pallas_call(
        kern, grid=(b, n // nc),
        in_specs=[pl.BlockSpec((1, t, wc), blk), pl.BlockSpec((1, t, W_V_A), blk),
                  pl.BlockSpec((1, nc, r), blk), pl.BlockSpec((1, nc, r), blk),
                  _const_spec(conv_w.shape), _const_spec(alog_row.shape), _const_spec(dt_row.shape),
                  _const_spec(gnorm.shape)],
        out_specs=[pl.BlockSpec((1, t, W_V_A), blk),
                   pl.BlockSpec((1, DK_A, H_A * DV_A), lambda bi, i: (bi, 0, 0))],
        out_shape=[jax.ShapeDtypeStruct((b, l, W_V_A), F32),
                   jax.ShapeDtypeStruct((b, DK_A, H_A * DV_A), F32)],
        scratch_shapes=[pltpu.VMEM((t + 8, wc), F32), pltpu.VMEM((DK_A, H_A * DV_A), F32)],
        compiler_params=_cparams("parallel", "arbitrary"), name="gdn_prompt")(
            conv_in, z, a_rows, b_rows, conv_w, alog_row, dt_row, gnorm)
    return oa, s


def _cfm_tail(y, bd_ref, lg_ref, lb_ref, w2_ref, res):
    y = y + bd_ref[...]
    yc = y - jnp.mean(y, axis=-1, keepdims=True)
    yn = yc * lax.rsqrt(jnp.mean(yc * yc, axis=-1, keepdims=True) + EPS) * lg_ref[...] + lb_ref[...]
    return res + jnp.dot(_silu(yn).astype(BF16), w2_ref[...], preferred_element_type=F32)


def _cfm_kernel(u_ref, x_ref, wd_ref, bd_ref, lg_ref, lb_ref, w2_ref, o_ref, xs_ref, sh_ref):
    t = u_ref.shape[1]
    halo = 32

    @pl.when(pl.program_id(1) == 0)
    def _():
        xs_ref[0:halo, :] = jnp.zeros((halo, xs_ref.shape[1]), F32)

    @pl.when(pl.program_id(1) != 0)
    def _():
        xs_ref[0:halo, :] = xs_ref[t:t + halo, :]

    xs_ref[halo:halo + t, :] = u_ref[0]
    y = None
    for r in range(8):
        offs = [o for o in range(halo - (CONV_C - 1), halo + 1) if o % 8 == r]
        if not offs:
            continue
        if r:
            n_rows = max(offs) - r + t
            sh_ref[0:n_rows, :] = xs_ref[r:r + n_rows, :]
        win_ref = sh_ref if r else xs_ref
        for o in offs:
            jj = o - (halo - (CONV_C - 1))
            term = win_ref[o - r:o - r + t, :] * wd_ref[jj:jj + 1, :]
            y = term if y is None else y + term
    o_ref[0] = _cfm_tail(y, bd_ref, lg_ref, lb_ref, w2_ref, x_ref[0])


def _cfm_prompt(u, x, w_dw, b_dw, ln_g, ln_b, w2, t=256):
    b, l, d = u.shape
    t = min(t, l)
    blk = lambda bi, i: (bi, i, 0)
    return pl.pallas_call(
        _cfm_kernel, grid=(b, l // t),
        in_specs=[pl.BlockSpec((1, t, d), blk), pl.BlockSpec((1, t, d), blk), _const_spec(w_dw.shape),
                  _const_spec(b_dw.shape), _const_spec(ln_g.shape), _const_spec(ln_b.shape),
                  _const_spec(w2.shape)],
        out_specs=pl.BlockSpec((1, t, d), blk),
        out_shape=jax.ShapeDtypeStruct((b, l, d), F32),
        scratch_shapes=[pltpu.VMEM((t + 32, d), F32), pltpu.VMEM((t + 32, d), F32)],
        compiler_params=_cparams("parallel", "arbitrary"), name="cfm_prompt")(
            u, x, w_dw, b_dw, ln_g, ln_b, w2)


def _cfm_sample_kernel(buf_ref, u_ref, x_ref, wd_ref, bd_ref, lg_ref, lb_ref, w2_ref, o_ref):
    y = u_ref[...] * wd_ref[CONV_C - 1:CONV_C, :]
    for jj in range(CONV_C - 1):
        y = y + buf_ref[jj] * wd_ref[jj:jj + 1, :]
    o_ref[...] = _cfm_tail(y, bd_ref, lg_ref, lb_ref, w2_ref, x_ref[...])


def _cfm_sample(buf_t, u, x, w_dw, b_dw, ln_g, ln_b, w2):
    args = (buf_t, u, x, w_dw, b_dw, ln_g, ln_b, w2)
    return pl.pallas_call(
        _cfm_sample_kernel, grid=(1,), in_specs=[_const_spec(a.shape) for a in args],
        out_specs=_const_spec(x.shape), out_shape=jax.ShapeDtypeStruct(x.shape, F32),
        compiler_params=_cparams("arbitrary"), name="cfm_sample")(*args)


def _rows8(row):
    return jnp.broadcast_to(row, (8, row.shape[1]))


def _gdn_sample_kernel(f_ref, cw_ref, sm_ref, al_ref, dt_ref, z_ref, gn_ref, s_ref, oa_ref, so_ref):
    y = _silu(jnp.sum(f_ref[0] * cw_ref[...], axis=0, keepdims=True))
    sm = sm_ref[0]
    g = -jnp.exp(al_ref[...]) * _softplus(sm + dt_ref[...])
    beta = jax.nn.sigmoid(sm)
    row_id = _iota((8, DK_A), 0)
    outs = []
    for h in range(H_A):
        q = y[:, h * DK_A:(h + 1) * DK_A]
        k = y[:, W_QK_A + h * DK_A:W_QK_A + (h + 1) * DK_A]
        v = y[:, 2 * W_QK_A + h * DV_A:2 * W_QK_A + (h + 1) * DV_A]
        q = q * lax.rsqrt(jnp.sum(q * q, axis=1, keepdims=True) + EPS) * (DK_A ** -0.5)
        k = k * lax.rsqrt(jnp.sum(k * k, axis=1, keepdims=True) + EPS)
        eg = jnp.exp(g[:, SM_A + h:SM_A + h + 1])
        bh = beta[:, SM_B + h:SM_B + h + 1]
        s = s_ref[0, h]
        lhs = jnp.where(row_id == 0, _rows8(k), jnp.where(row_id == 1, _rows8(q), 0.0)).astype(BF16)
        rs = jnp.dot(lhs, s.astype(BF16), preferred_element_type=F32)
        v_new = bh * (v - eg * rs[0:1, :])
        o = eg * rs[1:2, :] + jnp.sum(q * k, axis=1, keepdims=True) * v_new
        so_ref[0, h] = s * eg + _col_from_row(k, DK_A) * v_new
        outs.append(o * lax.rsqrt(jnp.mean(o * o, axis=1, keepdims=True) + EPS) * gn_ref[...])
    oa_ref[0] = jnp.concatenate(outs, axis=1) * _silu(z_ref[0])


def _gdn_sample(full, conv_w, small, alog_row, dt_row, z, gnorm, state):
    db = full.shape[0]
    r3 = lambda i: (i, 0, 0)
    r4 = lambda i: (i, 0, 0, 0)
    return pl.pallas_call(
        _gdn_sample_kernel, grid=(db,),
        in_specs=[pl.BlockSpec((1,) + full.shape[1:], r3), _const_spec(conv_w.shape),
                  pl.BlockSpec((1, 1, LANES), r3), _const_spec(alog_row.shape), _const_spec(dt_row.shape),
                  pl.BlockSpec((1, 1, W_V_A), r3), _const_spec(gnorm.shape),
                  pl.BlockSpec((1,) + state.shape[1:], r4)],
        out_specs=[pl.BlockSpec((1, 1, W_V_A), r3), pl.BlockSpec((1,) + state.shape[1:], r4)],
        out_shape=[jax.ShapeDtypeStruct((db, 1, W_V_A), F32), jax.ShapeDtypeStruct(state.shape, F32)],
        compiler_params=_cparams("parallel"), name="gdn_sample")(
            full, conv_w, small, alog_row, dt_row, z, gnorm, state)


def _xattn_sample_kernel(q_ref, mk_ref, mv_ref, o_ref):
    n_rows = mk_ref.shape[0]
    n_mem = n_rows // 8
    dh = 2 * LANES
    q8 = q_ref[0] * (dh ** -0.5)
    red = jnp.sum(mk_ref[...].reshape(n_mem, 8, LANES) * q8[None], axis=-1, keepdims=True)
    own_lane = _iota((LANES, 8, LANES), 0) == _iota((LANES, 8, LANES), 2)
    n_t = n_mem // LANES
    tiles = [jnp.sum(jnp.where(own_lane, red[j * LANES:(j + 1) * LANES], 0.0), axis=0) for j in range(n_t)]
    s8 = jnp.concatenate(tiles, axis=1)
    s = s8[0:H_X] + s8[H_X:2 * H_X]
    p = jnp.exp(s - jnp.max(s, axis=-1, keepdims=True))
    inv_l = 1.0 / jnp.sum(p, axis=-1, keepdims=True)
    p8 = jnp.concatenate([p, p], axis=0)
    ones = jnp.ones((LANES, LANES), BF16)
    acc = None
    for j in range(n_t):
        z = jnp.where(own_lane, p8[None, :, j * LANES:(j + 1) * LANES], 0.0).reshape(LANES * 8, LANES)
        p3 = jnp.dot(z.astype(BF16), ones, preferred_element_type=F32).reshape(LANES, 8, LANES)
        v3 = mv_ref[j * LANES * 8:(j + 1) * LANES * 8, :].reshape(LANES, 8, LANES)
        part = jnp.sum(p3 * v3, axis=0)
        acc = part if acc is None else acc + part
    o_ref[0] = acc * jnp.concatenate([inv_l, inv_l], axis=0)


def _mem_rows(cache):
    dp, db, nm = cache.shape[:3]
    x = cache.reshape(dp, db, nm, H_X, 2, LANES)
    return jnp.transpose(x, (0, 1, 2, 4, 3, 5)).reshape(dp, db, nm * 8, LANES)


def _xattn_sample(q, mk_rows, mv_rows, layer):
    db, d = q.shape
    n_rows = mk_rows.shape[2]
    q8 = jnp.transpose(q.reshape(db, H_X, 2, LANES), (0, 2, 1, 3)).reshape(db, 8, LANES)
    r3 = lambda i: (i, 0, 0)
    mem = lambda i: (layer, i, 0, 0)
    o8 = pl.pallas_call(
        _xattn_sample_kernel, grid=(db,),
        in_specs=[pl.BlockSpec((1, 8, LANES), r3), pl.BlockSpec((None, None, n_rows, LANES), mem),
                  pl.BlockSpec((None, None, n_rows, LANES), mem)],
        out_specs=pl.BlockSpec((1, 8, LANES), r3), out_shape=jax.ShapeDtypeStruct((db, 8, LANES), F32),
        compiler_params=_cparams("parallel"), name="xattn_sample")(q8, mk_rows, mv_rows)
    return jnp.transpose(o8.reshape(db, 2, H_X, LANES), (0, 2, 1, 3)).reshape(db, d)


def _head_rows(h):
    return slice(h * DH_B, (h + 1) * DH_B)


def _fox_sample_kernel(pt_ref, q_ref, kn_ref, vn_ref, f_ref, bf_ref, *refs, g_pages):
    k_refs = refs[0:g_pages]
    v_refs = refs[g_pages:2 * g_pages]
    lf_refs = refs[2 * g_pages:3 * g_pages]
    o_ref, lfn_ref, qb_ref, m_ref, l_ref, acc_ref, carry_ref = refs[3 * g_pages:]
    del pt_ref
    w = H_B * DH_B
    gi = pl.program_id(1)
    bcast = lambda col: jnp.broadcast_to(col, (H_B, LANES))
    head_id = _iota((H_B, LANES), 0)

    def rows_to_tile(rows):
        n = rows[0].shape[1]
        out = jnp.broadcast_to(rows[0], (H_B, n))
        for h in range(1, H_B):
            out = jnp.where(head_id[:, 0:n] == h, rows[h], out)
        return out

    @pl.when(gi == 0)
    def _():
        q = q_ref[0] * (DH_B ** -0.5)
        qb_ref[...] = jnp.broadcast_to(_col_from_row(q, w), (w, LANES))
        lfn = _log_sigmoid(f_ref[0] + bf_ref[...])
        lfn_ref[0] = lfn
        qk = q * kn_ref[0]
        s_new = rows_to_tile([jnp.sum(qk[:, _head_rows(h)], axis=1, keepdims=True) for h in range(H_B)])
        m_ref[...] = bcast(s_new)
        l_ref[...] = jnp.ones_like(l_ref)
        acc_ref[...] = jnp.where(_iota((w, LANES), 1) == 0, _col_from_row(vn_ref[0], w), 0.0)
        carry_ref[...] = bcast(lfn)

    later = (_iota((PAGE, PAGE), 0) > _iota((PAGE, PAGE), 1)).astype(BF16)
    m = m_ref[:, 0:1]
    l = l_ref[:, 0:1]
    carry = carry_ref[:, 0:1]
    logits = [None] * g_pages
    top = None
    for jj in reversed(range(g_pages)):
        s_t = rows_to_tile([jnp.sum(k_refs[jj][h] * qb_ref[_head_rows(h), :], axis=0, keepdims=True)
                            for h in range(H_B)])
        lf_t = lf_refs[jj][...]
        logits[jj] = s_t + _dot3_r(lf_t, later) + carry
        carry = carry + jnp.sum(lf_t, axis=1, keepdims=True)
        top = logits[jj] if top is None else jnp.maximum(top, logits[jj])
    m_new = jnp.maximum(m, jnp.max(top, axis=1, keepdims=True))
    alpha = jnp.exp(m - m_new)
    probs = [jnp.exp(lg - m_new) for lg in logits]
    p_sum = probs[0]
    for p_t in probs[1:]:
        p_sum = p_sum + p_t
    for h in range(H_B):
        acc_h = acc_ref[_head_rows(h), :] * alpha[h:h + 1, :]
        for jj in range(g_pages):
            acc_h = acc_h + v_refs[jj][h] * probs[jj][h:h + 1, :]
        acc_ref[_head_rows(h), :] = acc_h
    l = alpha * l + jnp.sum(p_sum, axis=1, keepdims=True)
    m_ref[...] = bcast(m_new)
    l_ref[...] = bcast(l)
    carry_ref[...] = bcast(carry)

    @pl.when(gi == pl.num_programs(1) - 1)
    def _():
        inv = 1.0 / l
        den = jnp.concatenate([jnp.broadcast_to(inv[h:h + 1, :], (DH_B, 1)) for h in range(H_B)], axis=0)
        col = jnp.sum(acc_ref[...], axis=1, keepdims=True) * den
        eye = _iota((w, w), 0) == _iota((w, w), 1)
        o_ref[0] = jnp.sum(jnp.where(eye, jnp.broadcast_to(col, (w, w)), 0.0), axis=0, keepdims=True)


def _fox_sample(q, k_new, v_new, f_col, bf_col, cache_kt, cache_vt, cache_lft, page_table, g_pages=8):
    db = q.shape[0]
    w = H_B * DH_B
    n_pages = page_table.shape[1]
    g_pages = min(g_pages, n_pages)
    ng = n_pages // g_pages
    r3 = lambda i, g, pt: (i, 0, 0)

    def page_map(nd):
        return [(lambda i, g, pt, jj=jj: (pt[i * n_pages + (ng - 1 - g) * g_pages + jj],) + (0,) * nd)
                for jj in range(g_pages)]

    in_specs = [pl.BlockSpec((1, 1, w), r3)] * 3 + [pl.BlockSpec((1, H_B, 1), r3),
                                                    pl.BlockSpec(bf_col.shape, lambda i, g, pt: (0, 0))]
    in_specs += [pl.BlockSpec((None, H_B, DH_B, PAGE), mp) for mp in page_map(3)] * 2
    in_specs += [pl.BlockSpec((None, H_B, PAGE), mp) for mp in page_map(2)]
    grid_spec = pltpu.PrefetchScalarGridSpec(
        num_scalar_prefetch=1, grid=(db, ng), in_specs=in_specs,
        out_specs=[pl.BlockSpec((1, 1, w), r3), pl.BlockSpec((1, H_B, 1), r3)],
        scratch_shapes=[pltpu.VMEM((w, LANES), F32), pltpu.VMEM((H_B, LANES), F32), pltpu.VMEM((H_B, LANES), F32),
                        pltpu.VMEM((w, LANES), F32), pltpu.VMEM((H_B, LANES), F32)])
    kern = functools.partial(_fox_sample_kernel, g_pages=g_pages)
    return pl.pallas_call(
        kern, grid_spec=grid_spec,
        out_shape=[jax.ShapeDtypeStruct((db, 1, w), F32), jax.ShapeDtypeStruct((db, H_B, 1), F32)],
        compiler_params=_cparams("parallel", "arbitrary"), name="fox_sample")(
            page_table.reshape(-1), q, k_new, v_new, f_col, bf_col,
            *([cache_kt] * g_pages), *([cache_vt] * g_pages), *([cache_lft] * g_pages))


def _row(v):
    return v.reshape(1, -1).astype(F32)


def _pad_lanes(v, start):
    return jnp.zeros((1, LANES), F32).at[0, start:start + v.shape[0]].set(v)


def _chunk_rows(cols, b, l):
    n = l // CHUNK_A
    return cols.reshape(b, n, CHUNK_A, H_A).transpose(0, 1, 3, 2).reshape(b, n, H_A * CHUNK_A)


def kernel(x_prompt, x_sample, mem_prompt, cache_fox_k, cache_fox_v, cache_fox_logf, page_table, state_gdn, state_gdn_conv, state_cfm_conv, cache_mem_k, cache_mem_v, norm_mix, w_in_e, conv_a, a_log, dt_bias, gnorm_a, b_f, w_out_e, w_pw1, b_pw1, w_dw, b_dw, ln_g, ln_b, w_pw2, norm_mem, norm_x, w_xq, w_xkv, w_xo, norm_f, w_up, w_down, norm_out):
    b, l, d = x_prompt.shape
    db = x_sample.shape[0]
    n_mem = mem_prompt.shape[1]
    depth = norm_mix.shape[0]
    dh_x = d // H_X
    bf = lambda w: w.astype(BF16)

    w_in = w_in_e[0]
    off_aa = W_CONV_A + W_V_A
    off_qb = off_aa + 2 * H_A
    off_fb = off_qb + 3 * W_B
    w_small = jnp.concatenate([w_in[:, off_aa:off_qb], w_in[:, off_fb:]], axis=1)
    w_small = jnp.pad(w_small, ((0, 0), (0, LANES - w_small.shape[1])))
    w_in_r = bf(jnp.concatenate([w_in[:, :off_aa], w_in[:, off_qb:off_fb], w_small], axis=1))
    in_widths = (W_CONV_A, W_V_A, W_B, W_B, W_B, LANES)
    w_out_a = bf(w_out_e[0][:W_V_A])
    w_out_b = bf(w_out_e[0][W_V_A:])
    w_xq_b = [bf(w_xq[i]) for i in range(depth)]
    w_xo_b = [bf(w_xo[i]) for i in range(depth)]
    w_up_b = [bf(w_up[i]) for i in range(depth)]
    w_down_b = [bf(w_down[i]) for i in range(depth)]
    w_pw1_b = bf(w_pw1[0])
    w_pw2_b = bf(w_pw2[0])
    alog_rows = _row(jnp.repeat(a_log[0], CHUNK_A))
    dt_rows = _row(jnp.repeat(dt_bias[0], CHUNK_A))
    gn = _row(gnorm_a[0])

    memf = mem_prompt.reshape(b * n_mem, d)
    mem_k, mem_v = [], []
    for i in range(depth):
        mk, mv = _linear([memf], [bf(w_xkv[i])], gain=_row(norm_mem[i]), out_widths=(d, d), name="mem_kv")
        mem_k.append(mk.reshape(b, n_mem, d))
        mem_v.append(mv.reshape(b, n_mem, d))
    mem_k_prompt = jnp.stack(mem_k).reshape(depth, b, n_mem, H_X, dh_x)
    mem_v_prompt = jnp.stack(mem_v).reshape(depth, b, n_mem, H_X, dh_x)

    xp = x_prompt.reshape(b * l, d)
    conv_in, z, q, k, v, small = _linear([xp], [w_in_r], gain=_row(norm_mix[0]), out_widths=in_widths,
                                         name="in_proj")
    fox_k_prompt = k.reshape(1, b, l, H_B, DH_B)
    fox_v_prompt = v.reshape(1, b, l, H_B, DH_B)
    conv_in3 = conv_in.reshape(b, l, W_CONV_A)
    gdn_conv_prompt = conv_in3[:, l - (CONV_A - 1):, :][None]
    oa, s_fin = _gdn_prompt(conv_in3, z.reshape(b, l, W_V_A),
                            _chunk_rows(small[:, SM_A:SM_A + H_A], b, l),
                            _chunk_rows(small[:, SM_B:SM_B + H_A], b, l),
                            conv_a[0], alog_rows, dt_rows, gn)
    gdn_state_prompt = s_fin.reshape(b, DK_A, H_A, DV_A).transpose(0, 2, 1, 3)[None]
    qf, kf, vf, lf = _fox_prep(q.reshape(b, l, W_B), k.reshape(b, l, W_B), v.reshape(b, l, W_B),
                               small.reshape(b, l, LANES), _pad_lanes(b_f[0], SM_F))
    fox_logf_prompt = lf[:, :, SM_F:SM_F + H_B][None]
    ob = _fox_flash(qf, kf, vf)
    x = _linear([oa.reshape(b * l, W_V_A), ob.reshape(b * l, W_B)], [w_out_a, w_out_b], res=xp, name="out_proj")
    x = _xattn_prompt(x.reshape(b, l, d), _row(norm_x[0]), w_xq_b[0], bf(mem_k[0]), bf(mem_v[0]), w_xo_b[0])
    x = _mlp(x.reshape(b * l, d), _row(norm_f[0]), w_up_b[0], w_down_b[0])
    glu = _linear([x], [w_pw1_b], gain=_row(norm_mix[1]), bias=_row(b_pw1[0]), act="glu", name="pw1_glu")
    glu3 = glu.reshape(b, l, d)
    cfm_conv_prompt = glu3[:, l - (CONV_C - 1):, :][None]
    x = _cfm_prompt(glu3, x.reshape(b, l, d), w_dw[0], _row(b_dw[0]), _row(ln_g[0]), _row(ln_b[0]), w_pw2_b)
    x = _xattn_prompt(x, _row(norm_x[1]), w_xq_b[1], bf(mem_k[1]), bf(mem_v[1]), w_xo_b[1])
    y_prompt = _mlp(x.reshape(b * l, d), _row(norm_f[1]), w_up_b[1], w_down_b[1],
                    final_gain=_row(norm_out)).reshape(b, l, d)

    xs = x_sample.reshape(db, d)
    conv_s, z_s, q_s, k_s, v_s, small_s = _linear([xs], [w_in_r], gain=_row(norm_mix[0]),
                                                  out_widths=in_widths, name="in_proj_s")
    fox_k_sample = k_s.reshape(1, db, 1, H_B, DH_B)
    fox_v_sample = v_s.reshape(1, db, 1, H_B, DH_B)
    full = jnp.concatenate([state_gdn_conv[0], conv_s[:, None, :]], axis=1)
    gdn_conv_sample = full[:, 1:, :][None]
    oa_s, s_new = _gdn_sample(full, conv_a[0], small_s.reshape(db, 1, LANES), _pad_lanes(a_log[0], SM_A),
                              _pad_lanes(dt_bias[0], SM_A), z_s.reshape(db, 1, W_V_A), gn, state_gdn[0])
    gdn_state_sample = s_new[None]
    ob_s, lf_s = _fox_sample(q_s.reshape(db, 1, W_B), k_s.reshape(db, 1, W_B), v_s.reshape(db, 1, W_B),
                             small_s[:, SM_F:SM_F + H_B].reshape(db, H_B, 1), b_f[0].reshape(H_B, 1),
                             jnp.transpose(cache_fox_k[0], (0, 2, 3, 1)), jnp.transpose(cache_fox_v[0], (0, 2, 3, 1)),
                             jnp.transpose(cache_fox_logf[0], (0, 2, 1)), page_table)
    fox_logf_sample = lf_s.reshape(1, db, 1, H_B)
    x = _linear([oa_s.reshape(db, W_V_A), ob_s.reshape(db, W_B)], [w_out_a, w_out_b], res=xs, name="out_proj_s")

    mk_rows = _mem_rows(cache_mem_k)
    mv_rows = _mem_rows(cache_mem_v)

    def xattn_s(x, i):
        qx = _linear([x], [w_xq_b[i]], gain=_row(norm_x[i]), name="xq_s")
        o = _xattn_sample(qx, mk_rows, mv_rows, i)
        return _linear([o], [w_xo_b[i]], res=x, name="xo_s")

    x = xattn_s(x, 0)
    x = _mlp(x, _row(norm_f[0]), w_up_b[0], w_down_b[0])
    glu_s = _linear([x], [w_pw1_b], gain=_row(norm_mix[1]), bias=_row(b_pw1[0]), act="glu", name="pw1_glu_s")
    cfm_conv_sample = jnp.concatenate([state_cfm_conv[0][:, 1:, :], glu_s[:, None, :]], axis=1)[None]
    x = _cfm_sample(state_cfm_conv[0].transpose(1, 0, 2), glu_s, x, w_dw[0], _row(b_dw[0]), _row(ln_g[0]),
                    _row(ln_b[0]), w_pw2_b)
    x = xattn_s(x, 1)
    y_sample = _mlp(x, _row(norm_f[1]), w_up_b[1], w_down_b[1], final_gain=_row(norm_out)).reshape(db, 1, d)

    return (y_prompt, y_sample, fox_k_prompt, fox_v_prompt, fox_logf_prompt, fox_k_sample, fox_v_sample,
            fox_logf_sample, gdn_state_prompt, gdn_conv_prompt, gdn_state_sample, gdn_conv_sample,
            cfm_conv_prompt, cfm_conv_sample, mem_k_prompt, mem_v_prompt)
```

```python
import functools
import math

import jax
import jax.numpy as jnp
import numpy as np
from jax import lax
from jax.experimental import pallas as pl
from jax.experimental.pallas import tpu as pltpu

F32 = jnp.float32
BF16 = jnp.bfloat16
EPS = 1e-6
NEG = -1e30
LOG2E = 1.4426950408889634

H_A = 4
DK_A = 128
DV_A = 128
CONV_A = 4
CHUNK_A = 64
H_B = 8
DH_B = 64
H_X = 4
CONV_C = 31
PAGE = 128

W_QK_A = H_A * DK_A
W_V_A = H_A * DV_A
W_CONV_A = 2 * W_QK_A + W_V_A
W_B = H_B * DH_B
LANES = 128
SM_A = 0
SM_B = H_A
SM_F = 2 * H_A

VMEM_LIMIT = 56 * 1024 * 1024


def _cparams(*sem):
    return pltpu.CompilerParams(dimension_semantics=sem, vmem_limit_bytes=VMEM_LIMIT)


def _const_spec(shape):
    nd = len(shape)
    return pl.BlockSpec(shape, lambda *_: (0,) * nd)


def _rms(x, g):
    return x * lax.rsqrt(jnp.mean(x * x, axis=-1, keepdims=True) + EPS) * g


def _silu(x):
    return x * jax.nn.sigmoid(x)


def _softplus(x):
    return jnp.maximum(x, 0.0) + jnp.log1p(jnp.exp(-jnp.abs(x)))


def _split3(x):
    h = x.astype(BF16)
    r = x - h.astype(F32)
    m = r.astype(BF16)
    l = (r - m.astype(F32)).astype(BF16)
    return h, m, l


def _dot3_l(mat01, x):
    h, m, l = _split3(x)
    d = lambda t: jnp.dot(mat01, t, preferred_element_type=F32)
    return d(h) + d(m) + d(l)


def _dot3_r(x, mat01):
    h, m, l = _split3(x)
    d = lambda t: jnp.dot(t, mat01, preferred_element_type=F32)
    return d(h) + d(m) + d(l)


def _dot_nt(a, b):
    return lax.dot_general(a, b, (((1,), (1,)), ((), ())), preferred_element_type=F32)


def _dot_tn(a, b):
    return lax.dot_general(a, b, (((0,), (0,)), ((), ())), preferred_element_type=F32)


def _iota(shape, dim):
    return lax.broadcasted_iota(jnp.int32, shape, dim)


def _col_from_row(row, n):
    eye = _iota((n, n), 0) == _iota((n, n), 1)
    return jnp.sum(jnp.where(eye, jnp.broadcast_to(row, (n, n)), 0.0), axis=1, keepdims=True)


def _linear_kernel(*refs, n_in, n_t, has_gain, has_bias, has_res, act, out_widths, chunk):
    it = iter(refs)
    x_refs = [next(it) for _ in range(n_in)]
    w_refs = [next(it) for _ in range(n_in)]
    wt_refs = [next(it) for _ in range(n_t)]
    gain_ref = next(it) if has_gain else None
    bias_ref = next(it) if has_bias else None
    res_ref = next(it) if has_res else None
    out_refs = [next(it) for _ in out_widths]
    outt_refs = [next(it) for _ in range(n_t)]
    xs = []
    for i, xr in enumerate(x_refs):
        x = xr[...]
        if has_gain and i == 0:
            x = _rms(x, gain_ref[...])
        xs.append(x.astype(BF16))
    n_total = sum(out_widths)

    def mm(col0, cw):
        acc = None
        for xb, wr in zip(xs, w_refs):
            d = jnp.dot(xb, wr[:, col0:col0 + cw], preferred_element_type=F32)
            acc = d if acc is None else acc + d
        if has_bias:
            acc = acc + bias_ref[:, col0:col0 + cw]
        return acc

    off = 0
    for o_ref, width in zip(out_refs, out_widths):
        for c0 in range(0, width, chunk):
            cw = min(chunk, width - c0)
            y = mm(off + c0, cw)
            if act == "glu":
                y = y * jax.nn.sigmoid(mm(n_total + off + c0, cw))
            if has_res:
                y = y + res_ref[:, off + c0:off + c0 + cw]
            o_ref[:, c0:c0 + cw] = y
        off += width
    for wt_ref, ot_ref in zip(wt_refs, outt_refs):
        ot_ref[0] = _dot_nt(wt_ref[...], xs[0])


def _linear(xs, ws, *, gain=None, bias=None, res=None, act=None, out_widths=None, wts=(), t_batch=None,
            tm=512, chunk=512, name="linear"):
    m = xs[0].shape[0]
    tm = min(tm, m)
    assert m % tm == 0
    n_mm = ws[0].shape[1]
    n_out = n_mm // 2 if act == "glu" else n_mm
    if out_widths is None:
        out_widths = (n_out,)
    assert sum(out_widths) == n_out
    row = lambda i: (i, 0)
    in_specs = [pl.BlockSpec((tm, x.shape[1]), row) for x in xs]
    in_specs += [_const_spec(w.shape) for w in ws]
    in_specs += [_const_spec(w.shape) for w in wts]
    args = list(xs) + list(ws) + list(wts)
    if gain is not None:
        in_specs.append(_const_spec(gain.shape)); args.append(gain)
    if bias is not None:
        in_specs.append(_const_spec(bias.shape)); args.append(bias)
    if res is not None:
        in_specs.append(pl.BlockSpec((tm, res.shape[1]), row)); args.append(res)
    out_shape = [jax.ShapeDtypeStruct((m, w), F32) for w in out_widths]
    out_specs = [pl.BlockSpec((tm, w), row) for w in out_widths]
    if wts:
        tb, tl = t_batch
        assert tb * tl == m and tl % tm == 0
        per = tl // tm
        out_shape += [jax.ShapeDtypeStruct((tb, w.shape[0], tl), F32) for w in wts]
        out_specs += [pl.BlockSpec((1, w.shape[0], tm), lambda i: (i // per, 0, i % per)) for w in wts]
    kern = functools.partial(_linear_kernel, n_in=len(xs), n_t=len(wts), has_gain=gain is not None,
                             has_bias=bias is not None, has_res=res is not None, act=act,
                             out_widths=tuple(out_widths), chunk=chunk)
    outs = pl.pallas_call(kern, grid=(m // tm,), in_specs=in_specs, out_specs=out_specs,
                          out_shape=out_shape, compiler_params=_cparams("parallel"), name=name)(*args)
    return outs if len(outs) > 1 else outs[0]


def _mlp_kernel(*refs, chunk, final_norm):
    if final_norm:
        x_ref, g_ref, wu_ref, wd_ref, go_ref, o_ref = refs
    else:
        x_ref, g_ref, wu_ref, wd_ref, o_ref = refs
    x = x_ref[...]
    xn = _rms(x, g_ref[...]).astype(BF16)
    acc = x
    d_ff = wu_ref.shape[1]
    for c0 in range(0, d_ff, chunk):
        h = jnp.dot(xn, wu_ref[:, c0:c0 + chunk], preferred_element_type=F32)
        h = jnp.square(jnp.maximum(h, 0.0)).astype(BF16)
        acc = acc + jnp.dot(h, wd_ref[c0:c0 + chunk, :], preferred_element_type=F32)
    if final_norm:
        acc = _rms(acc, go_ref[...])
    o_ref[...] = acc


def _mlp(x, gain, w_up, w_down, final_gain=None, tm=512, chunk=512):
    m, d = x.shape
    tm = min(tm, m)
    row = lambda i: (i, 0)
    in_specs = [pl.BlockSpec((tm, d), row), _const_spec(gain.shape), _const_spec(w_up.shape),
                _const_spec(w_down.shape)]
    args = [x, gain, w_up, w_down]
    if final_gain is not None:
        in_specs.append(_const_spec(final_gain.shape)); args.append(final_gain)
    kern = functools.partial(_mlp_kernel, chunk=chunk, final_norm=final_gain is not None)
    return pl.pallas_call(kern, grid=(m // tm,), in_specs=in_specs,
                          out_specs=pl.BlockSpec((tm, d), row),
                          out_shape=jax.ShapeDtypeStruct((m, d), F32),
                          compiler_params=_cparams("parallel"), name="mlp")(*args)


def _xattn_kernel(x_ref, g_ref, wq_ref, mk_ref, mv_ref, wo_ref, o_ref):
    x = x_ref[0]
    d = x.shape[1]
    dh = d // H_X
    xn = _rms(x, g_ref[...]).astype(BF16)
    q = jnp.dot(xn, wq_ref[...], preferred_element_type=F32) * (dh ** -0.5)
    q = q.astype(BF16)
    outs = []
    for h in range(H_X):
        sl = slice(h * dh, (h + 1) * dh)
        s = _dot_nt(q[:, sl], mk_ref[0, :, sl])
        p = jnp.exp(s - jnp.max(s, axis=-1, keepdims=True))
        l = jnp.sum(p, axis=-1, keepdims=True)
        o = jnp.dot(p.astype(BF16), mv_ref[0, :, sl], preferred_element_type=F32)
        outs.append((o * (1.0 / l)).astype(BF16))
    o = jnp.concatenate(outs, axis=1)
    o_ref[0] = x + jnp.dot(o, wo_ref[...], preferred_element_type=F32)


def _xattn_prompt(x, gain, wq, mk, mv, wo, tm=512):
    b, l, d = x.shape
    tm = min(tm, l)
    nm = mk.shape[1]
    blk = lambda bi, i: (bi, i, 0)
    mem = lambda bi, i: (bi, 0, 0)
    return pl.pallas_call(
        _xattn_kernel, grid=(b, l // tm),
        in_specs=[pl.BlockSpec((1, tm, d), blk), _const_spec(gain.shape), _const_spec(wq.shape),
                  pl.BlockSpec((1, nm, d), mem), pl.BlockSpec((1, nm, d), mem), _const_spec(wo.shape)],
        out_specs=pl.BlockSpec((1, tm, d), blk),
        out_shape=jax.ShapeDtypeStruct((b, l, d), F32),
        compiler_params=_cparams("parallel", "parallel"), name="xattn_prompt")(x, gain, wq, mk, mv, wo)


def _log_sigmoid(x):
    return jnp.minimum(x, 0.0) - jnp.log1p(jnp.exp(-jnp.abs(x)))


def _head_lanes(x, h):
    p = h // 2
    blk = x[:, p * LANES:(p + 1) * LANES]
    if h % 2:
        blk = pltpu.roll(blk, DH_B, 1)
    return blk


def _fox_prep_kernel(q_ref, kt_ref, v_ref, sm_ref, smt_ref, bf_ref, bfc_ref, qf_ref, kft_ref, vf_ref, lft_ref,
                     carry_ref, carryt_ref):
    t = q_ref.shape[1]

    @pl.when(pl.program_id(1) == 0)
    def _():
        carry_ref[...] = jnp.zeros_like(carry_ref)
        carryt_ref[...] = jnp.zeros_like(carryt_ref)

    lane = _iota((t, LANES), 1)
    gate_lane = (lane >= SM_F) & (lane < SM_F + H_B)
    logf = jnp.where(gate_lane, _log_sigmoid(sm_ref[0] + bf_ref[...]), 0.0)
    before = _iota((t, t), 0) >= _iota((t, t), 1)
    c = _dot3_l(before.astype(BF16), logf) + carry_ref[0:1, :]
    carry_ref[0:1, :] = c[t - 1:t, :]
    c1, c2, c3 = _split3(c * LOG2E)
    c1, c2, c3 = c1.astype(F32), c2.astype(F32), c3.astype(F32)
    lft = _log_sigmoid(smt_ref[0] + bfc_ref[...])
    lft_ref[0] = lft
    ct = _dot3_r(lft, (_iota((t, t), 0) <= _iota((t, t), 1)).astype(BF16)) + carryt_ref[:, 0:1]
    carryt_ref[...] = jnp.broadcast_to(ct[:, t - 1:t], carryt_ref.shape)
    r1, r2, r3 = _split3(ct * LOG2E)
    r1, r2, r3 = r1.astype(F32), r2.astype(F32), r3.astype(F32)
    q = q_ref[0] * (DH_B ** -0.5 * LOG2E)
    v = v_ref[0]
    low = lane < DH_B
    row = _iota((DH_B, t), 0)
    for h in range(H_B):
        col = slice(SM_F + h, SM_F + h + 1)
        b1, b2, b3 = c1[:, col], c2[:, col], c3[:, col]
        qx = jnp.where(lane == DH_B, b1, jnp.where(lane == DH_B + 1, b2, jnp.where(
            lane == DH_B + 2, b3, jnp.where(lane < DH_B + 6, 1.0, 0.0))))
        kx = jnp.where(row < 3, 1.0, jnp.where(row == 3, -r1[h:h + 1, :], jnp.where(
            row == 4, -r2[h:h + 1, :], jnp.where(row == 5, -r3[h:h + 1, :], 0.0))))
        vx = jnp.where(lane == DH_B, 1.0, 0.0)
        qf_ref[0, h] = jnp.where(low, _head_lanes(q, h), qx).astype(BF16)
        kft_ref[0, h] = jnp.concatenate([kt_ref[0, h * DH_B:(h + 1) * DH_B, :], kx], axis=0).astype(BF16)
        vf_ref[0, h] = jnp.where(low, _head_lanes(v, h), vx).astype(BF16)


def _fox_prep(q, kt, v, small, small_t, bf_row, bf_col, t=256):
    b, l, w = q.shape
    t = min(t, l)
    blk = lambda bi, i: (bi, i, 0)
    tblk = lambda bi, i: (bi, 0, i)
    hblk = lambda bi, i: (bi, 0, i, 0)
    hshape = jax.ShapeDtypeStruct((b, H_B, l, LANES), BF16)
    return pl.pallas_call(
        _fox_prep_kernel, grid=(b, l // t),
        in_specs=[pl.BlockSpec((1, t, w), blk), pl.BlockSpec((1, w, t), tblk), pl.BlockSpec((1, t, w), blk),
                  pl.BlockSpec((1, t, LANES), blk),
                  pl.BlockSpec((1, H_B, t), lambda bi, i: (bi, SM_F // H_B, i)),
                  _const_spec(bf_row.shape), _const_spec(bf_col.shape)],
        out_specs=[pl.BlockSpec((1, H_B, t, LANES), hblk),
                   pl.BlockSpec((1, H_B, LANES, t), lambda bi, i: (bi, 0, 0, i)),
                   pl.BlockSpec((1, H_B, t, LANES), hblk), pl.BlockSpec((1, H_B, t), tblk)],
        out_shape=[hshape, jax.ShapeDtypeStruct((b, H_B, LANES, l), BF16), hshape,
                   jax.ShapeDtypeStruct((b, H_B, l), F32)],
        scratch_shapes=[pltpu.VMEM((8, LANES), F32), pltpu.VMEM((H_B, LANES), F32)],
        compiler_params=_cparams("parallel", "arbitrary"), name="fox_prep")(
            q, kt, v, small, small_t, bf_row, bf_col)


def _fox_flash_kernel(it_ref, jt_ref, qf_ref, kft_ref, vf_ref, o_ref, m_ref, acc_ref):
    p_id = pl.program_id(1)
    i = it_ref[p_id]
    j = jt_ref[p_id]
    tq = qf_ref.shape[2]
    tk = kft_ref.shape[3]

    @pl.when(j == 0)
    def _():
        m_ref[...] = jnp.full_like(m_ref, NEG)
        acc_ref[...] = jnp.zeros_like(acc_ref)

    def heads(masked):
        def scores(h):
            s = jnp.dot(qf_ref[0, h], kft_ref[0, h], preferred_element_type=F32)
            if masked:
                s = jnp.where(_iota((tq, tk), 1) <= _iota((tq, tk), 0), s, NEG)
            return s

        s_next = scores(0)
        for h in range(H_B):
            s = s_next
            if h + 1 < H_B:
                s_next = scores(h + 1)
            m_prev = m_ref[h]
            m_new = jnp.maximum(m_prev, jnp.max(s, axis=-1, keepdims=True))
            alpha = jnp.exp2(m_prev - m_new)
            p = jnp.exp2(s - jnp.concatenate([m_new] * (tk // LANES), axis=1)).astype(BF16)
            acc_ref[h] = alpha * acc_ref[h] + jnp.dot(p, vf_ref[0, h], preferred_element_type=F32)
            m_ref[h] = m_new

    @pl.when(j < i)
    def _():
        heads(False)

    @pl.when(j == i)
    def _():
        heads(True)
        lane = _iota((tq, LANES), 1)
        for pr in range(H_B // 2):
            a0 = acc_ref[2 * pr]
            a1 = acc_ref[2 * pr + 1]
            o0 = a0 * (1.0 / a0[:, DH_B:DH_B + 1])
            o1 = a1 * (1.0 / a1[:, DH_B:DH_B + 1])
            o_ref[0, :, pr * LANES:(pr + 1) * LANES] = jnp.where(lane < DH_B, o0, pltpu.roll(o1, DH_B, 1))


def _fox_flash(qf, kft, vf, t=512):
    b, hh, l, _ = qf.shape
    t = min(t, l)
    n = l // t
    pairs = [(i, j) for i in range(n) for j in range(i + 1)]
    it = jnp.asarray(np.array([p[0] for p in pairs], np.int32))
    jt = jnp.asarray(np.array([p[1] for p in pairs], np.int32))
    qmap = lambda bi, p, it, jt: (bi, 0, it[p], 0)
    kmap = lambda bi, p, it, jt: (bi, 0, jt[p], 0)
    ktmap = lambda bi, p, it, jt: (bi, 0, 0, jt[p])
    omap = lambda bi, p, it, jt: (bi, it[p], 0)
    grid_spec = pltpu.PrefetchScalarGridSpec(
        num_scalar_prefetch=2, grid=(b, len(pairs)),
        in_specs=[pl.BlockSpec((1, hh, t, LANES), qmap), pl.BlockSpec((1, hh, LANES, t), ktmap),
                  pl.BlockSpec((1, hh, t, LANES), kmap)],
        out_specs=pl.BlockSpec((1, t, hh * DH_B), omap),
        scratch_shapes=[pltpu.VMEM((hh, t, LANES), F32), pltpu.VMEM((hh, t, LANES), F32)])
    return pl.pallas_call(
        _fox_flash_kernel, grid_spec=grid_spec,
        out_shape=jax.ShapeDtypeStruct((b, l, hh * DH_B), F32),
        compiler_params=_cparams("parallel", "arbitrary"), name="fox_flash")(it, jt, qf, kft, vf)


def _gdn_kernel(x_ref, z_ref, a_ref, b_ref, cw_ref, al_ref, dt_ref, gn_ref, oa_ref, s_ref,
                xs_ref, st_ref, *, nc):
    c = CHUNK_A
    r = H_A * c
    t = nc * c
    halo = 8

    @pl.when(pl.program_id(1) == 0)
    def _():
        xs_ref[0:halo, :] = jnp.zeros((halo, xs_ref.shape[1]), F32)
        st_ref[...] = jnp.zeros_like(st_ref)

    @pl.when(pl.program_id(1) != 0)
    def _():
        xs_ref[0:halo, :] = xs_ref[t:t + halo, :]

    xs_ref[halo:halo + t, :] = x_ref[0]
    y = None
    for jj in range(CONV_A):
        o = halo - (CONV_A - 1) + jj
        term = xs_ref[o:o + t, :] * cw_ref[jj:jj + 1, :]
        y = term if y is None else y + term
    y = _silu(y)

    g_rows = -jnp.exp(al_ref[...]) * _softplus(a_ref[0] + dt_ref[...])
    beta_rows = jax.nn.sigmoid(b_ref[0])
    ri = _iota((r, r), 0)
    ci = _iota((r, r), 1)
    same = (ri // c) == (ci // c)
    incl = same & (ri >= ci)
    strict = same & (ri > ci)
    gc_rows = _dot3_r(g_rows, (same & (ri <= ci)).astype(BF16))
    eye = ri == ci
    last = ((ci % c) == (c - 1)) & same

    def stack(a):
        return jnp.concatenate([a[:, h * DK_A:(h + 1) * DK_A] for h in range(H_A)], axis=0)

    pre = []
    for n in range(nc):
        rows = slice(n * c, (n + 1) * c)
        gc_r = gc_rows[n:n + 1, :]
        gcb = jnp.broadcast_to(gc_r, (r, r))
        gc_c = jnp.sum(jnp.where(eye, gcb, 0.0), axis=1, keepdims=True)
        gl_c = jnp.sum(jnp.where(last, gcb, 0.0), axis=1, keepdims=True)
        beta_c = jnp.sum(jnp.where(eye, jnp.broadcast_to(beta_rows[n:n + 1, :], (r, r)), 0.0),
                         axis=1, keepdims=True)
        q = stack(y[rows, 0:W_QK_A])
        k = stack(y[rows, W_QK_A:2 * W_QK_A])
        v = stack(y[rows, 2 * W_QK_A:W_CONV_A])
        q = q * lax.rsqrt(jnp.sum(q * q, axis=1, keepdims=True) + EPS) * (DK_A ** -0.5)
        k = k * lax.rsqrt(jnp.sum(k * k, axis=1, keepdims=True) + EPS)
        decay = jnp.exp(jnp.where(incl, gc_c - gc_r, -jnp.inf))
        kb = k * beta_c
        k16 = k.astype(BF16)
        lmat = jnp.where(strict, _dot_nt(kb.astype(BF16), k16) * decay, 0.0)
        a_qk = jnp.where(incl, _dot_nt(q.astype(BF16), k16) * decay, 0.0)
        mneg = -lmat
        tinv = jnp.where(eye, 1.0, 0.0) + mneg
        for _ in range(int(math.log2(c)) - 1):
            mneg = jnp.dot(mneg.astype(BF16), mneg.astype(BF16), preferred_element_type=F32)
            tinv = tinv + jnp.dot(tinv.astype(BF16), mneg.astype(BF16), preferred_element_type=F32)
        eg = jnp.exp(gc_c)
        rhs = jnp.concatenate([v * beta_c, kb * eg], axis=1).astype(BF16)
        uw = jnp.dot(tinv.astype(BF16), rhs, preferred_element_type=F32)
        pre.append(dict(u=uw[:, 0:DV_A], w=uw[:, DV_A:2 * DV_A].astype(BF16), qg=(q * eg).astype(BF16),
                        a_qk=a_qk.astype(BF16), kdec=(k * jnp.exp(gl_c - gc_c)).astype(BF16),
                        dec=[jnp.exp(gl_c[h * c:h * c + 1, :]) for h in range(H_A)]))

    s_heads = [st_ref[:, h * DV_A:(h + 1) * DV_A] for h in range(H_A)]
    for n in range(nc):
        rows = slice(n * c, (n + 1) * c)
        pc = pre[n]
        v_parts, qs_parts = [], []
        for h in range(H_A):
            hr = slice(h * c, (h + 1) * c)
            wq = jnp.concatenate([pc["w"][hr], pc["qg"][hr]], axis=0)
            wq_s = jnp.dot(wq, s_heads[h].astype(BF16), preferred_element_type=F32)
            v_parts.append(pc["u"][hr] - wq_s[0:c])
            qs_parts.append(wq_s[c:2 * c])
        v_new = jnp.concatenate(v_parts, axis=0).astype(BF16)
        o = jnp.concatenate(qs_parts, axis=0) + jnp.dot(pc["a_qk"], v_new, preferred_element_type=F32)
        for h in range(H_A):
            hr = slice(h * c, (h + 1) * c)
            s_heads[h] = s_heads[h] * pc["dec"][h] + _dot_tn(pc["kdec"][hr], v_new[hr])
        on = o * lax.rsqrt(jnp.mean(o * o, axis=1, keepdims=True) + EPS) * gn_ref[...]
        on = jnp.concatenate([on[h * c:(h + 1) * c, :] for h in range(H_A)], axis=1)
        oa_ref[0, rows, :] = on * _silu(z_ref[0, rows, :])
    for h in range(H_A):
        st_ref[:, h * DV_A:(h + 1) * DV_A] = s_heads[h]

    @pl.when(pl.program_id(1) == pl.num_programs(1) - 1)
    def _():
        s_ref[0] = st_ref[...]


def _gdn_prompt(conv_in, z, a_rows, b_rows, conv_w, alog_row, dt_row, gnorm, nc=8):
    b, l, wc = conv_in.shape
    n = l // CHUNK_A
    nc = min(nc, n)
    t = nc * CHUNK_A
    r = H_A * CHUNK_A
    blk = lambda bi, i: (bi, i, 0)
    kern = functools.partial(_gdn_kernel, nc=nc)
    oa, s = pl.pallas_call(
        kern, grid=(b, n // nc),
        in_specs=[pl.BlockSpec((1, t, wc), blk), pl.BlockSpec((1, t, W_V_A), blk),
                  pl.BlockSpec((1, nc, r), blk), pl.BlockSpec((1, nc, r), blk),
                  _const_spec(conv_w.shape), _const_spec(alog_row.shape), _const_spec(dt_row.shape),
                  _const_spec(gnorm.shape)],
        out_specs=[pl.BlockSpec((1, t, W_V_A), blk),
                   pl.BlockSpec((1, DK_A, H_A * DV_A), lambda bi, i: (bi, 0, 0))],
        out_shape=[jax.ShapeDtypeStruct((b, l, W_V_A), F32),
                   jax.ShapeDtypeStruct((b, DK_A, H_A * DV_A), F32)],
        scratch_shapes=[pltpu.VMEM((t + 8, wc), F32), pltpu.VMEM((DK_A, H_A * DV_A), F32)],
        compiler_params=_cparams("parallel", "arbitrary"), name="gdn_prompt")(
            conv_in, z, a_rows, b_rows, conv_w, alog_row, dt_row, gnorm)
    return oa, s


def _cfm_tail(y, bd_ref, lg_ref, lb_ref, w2_ref, res):
    y = y + bd_ref[...]
    yc = y - jnp.mean(y, axis=-1, keepdims=True)
    yn = yc * lax.rsqrt(jnp.mean(yc * yc, axis=-1, keepdims=True) + EPS) * lg_ref[...] + lb_ref[...]
    return res + jnp.dot(_silu(yn).astype(BF16), w2_ref[...], preferred_element_type=F32)


def _cfm_kernel(u_ref, x_ref, wd_ref, bd_ref, lg_ref, lb_ref, w2_ref, o_ref, xs_ref, sh_ref):
    t = u_ref.shape[1]
    halo = 32

    @pl.when(pl.program_id(1) == 0)
    def _():
        xs_ref[0:halo, :] = jnp.zeros((halo, xs_ref.shape[1]), F32)

    @pl.when(pl.program_id(1) != 0)
    def _():
        xs_ref[0:halo, :] = xs_ref[t:t + halo, :]

    xs_ref[halo:halo + t, :] = u_ref[0]
    y = None
    for r in range(8):
        offs = [o for o in range(halo - (CONV_C - 1), halo + 1) if o % 8 == r]
        if not offs:
            continue
        if r:
            n_rows = max(offs) - r + t
            sh_ref[0:n_rows, :] = xs_ref[r:r + n_rows, :]
        win_ref = sh_ref if r else xs_ref
        for o in offs:
            jj = o - (halo - (CONV_C - 1))
            term = win_ref[o - r:o - r + t, :] * wd_ref[jj:jj + 1, :]
            y = term if y is None else y + term
    o_ref[0] = _cfm_tail(y, bd_ref, lg_ref, lb_ref, w2_ref, x_ref[0])


def _cfm_prompt(u, x, w_dw, b_dw, ln_g, ln_b, w2, t=256):
    b, l, d = u.shape
    t = min(t, l)
    blk = lambda bi, i: (bi, i, 0)
    return pl.pallas_call(
        _cfm_kernel, grid=(b, l // t),
        in_specs=[pl.BlockSpec((1, t, d), blk), pl.BlockSpec((1, t, d), blk), _const_spec(w_dw.shape),
                  _const_spec(b_dw.shape), _const_spec(ln_g.shape), _const_spec(ln_b.shape),
                  _const_spec(w2.shape)],
        out_specs=pl.BlockSpec((1, t, d), blk),
        out_shape=jax.ShapeDtypeStruct((b, l, d), F32),
        scratch_shapes=[pltpu.VMEM((t + 32, d), F32), pltpu.VMEM((t + 32, d), F32)],
        compiler_params=_cparams("parallel", "arbitrary"), name="cfm_prompt")(
            u, x, w_dw, b_dw, ln_g, ln_b, w2)


def _cfm_sample_kernel(buf_ref, u_ref, x_ref, wd_ref, bd_ref, lg_ref, lb_ref, w2_ref, o_ref):
    y = u_ref[...] * wd_ref[CONV_C - 1:CONV_C, :]
    for jj in range(CONV_C - 1):
        y = y + buf_ref[jj] * wd_ref[jj:jj + 1, :]
    o_ref[...] = _cfm_tail(y, bd_ref, lg_ref, lb_ref, w2_ref, x_ref[...])


def _cfm_sample(buf_t, u, x, w_dw, b_dw, ln_g, ln_b, w2):
    args = (buf_t, u, x, w_dw, b_dw, ln_g, ln_b, w2)
    return pl.pallas_call(
        _cfm_sample_kernel, grid=(1,), in_specs=[_const_spec(a.shape) for a in args],
        out_specs=_const_spec(x.shape), out_shape=jax.ShapeDtypeStruct(x.shape, F32),
        compiler_params=_cparams("arbitrary"), name="cfm_sample")(*args)


def _rows8(row):
    return jnp.broadcast_to(row, (8, row.shape[1]))


def _gdn_sample_kernel(f_ref, cw_ref, sm_ref, al_ref, dt_ref, z_ref, gn_ref, s_ref, oa_ref, so_ref):
    y = _silu(jnp.sum(f_ref[0] * cw_ref[...], axis=0, keepdims=True))
    sm = sm_ref[0]
    g = -jnp.exp(al_ref[...]) * _softplus(sm + dt_ref[...])
    beta = jax.nn.sigmoid(sm)
    row_id = _iota((8, DK_A), 0)
    outs = []
    for h in range(H_A):
        q = y[:, h * DK_A:(h + 1) * DK_A]
        k = y[:, W_QK_A + h * DK_A:W_QK_A + (h + 1) * DK_A]
        v = y[:, 2 * W_QK_A + h * DV_A:2 * W_QK_A + (h + 1) * DV_A]
        q = q * lax.rsqrt(jnp.sum(q * q, axis=1, keepdims=True) + EPS) * (DK_A ** -0.5)
        k = k * lax.rsqrt(jnp.sum(k * k, axis=1, keepdims=True) + EPS)
        eg = jnp.exp(g[:, SM_A + h:SM_A + h + 1])
        bh = beta[:, SM_B + h:SM_B + h + 1]
        s = s_ref[0, h]
        lhs = jnp.where(row_id == 0, _rows8(k), jnp.where(row_id == 1, _rows8(q), 0.0)).astype(BF16)
        rs = jnp.dot(lhs, s.astype(BF16), preferred_element_type=F32)
        v_new = bh * (v - eg * rs[0:1, :])
        o = eg * rs[1:2, :] + jnp.sum(q * k, axis=1, keepdims=True) * v_new
        so_ref[0, h] = s * eg + _col_from_row(k, DK_A) * v_new
        outs.append(o * lax.rsqrt(jnp.mean(o * o, axis=1, keepdims=True) + EPS) * gn_ref[...])
    oa_ref[0] = jnp.concatenate(outs, axis=1) * _silu(z_ref[0])


def _gdn_sample(full, conv_w, small, alog_row, dt_row, z, gnorm, state):
    db = full.shape[0]
    r3 = lambda i: (i, 0, 0)
    r4 = lambda i: (i, 0, 0, 0)
    return pl.pallas_call(
        _gdn_sample_kernel, grid=(db,),
        in_specs=[pl.BlockSpec((1,) + full.shape[1:], r3), _const_spec(conv_w.shape),
                  pl.BlockSpec((1, 1, LANES), r3), _const_spec(alog_row.shape), _const_spec(dt_row.shape),
                  pl.BlockSpec((1, 1, W_V_A), r3), _const_spec(gnorm.shape),
                  pl.BlockSpec((1,) + state.shape[1:], r4)],
        out_specs=[pl.BlockSpec((1, 1, W_V_A), r3), pl.BlockSpec((1,) + state.shape[1:], r4)],
        out_shape=[jax.ShapeDtypeStruct((db, 1, W_V_A), F32), jax.ShapeDtypeStruct(state.shape, F32)],
        compiler_params=_cparams("parallel"), name="gdn_sample")(
            full, conv_w, small, alog_row, dt_row, z, gnorm, state)


def _xattn_sample_kernel(q_ref, mk_ref, mv_ref, o_ref):
    n_rows = mk_ref.shape[0]
    n_mem = n_rows // 8
    dh = 2 * LANES
    q8 = q_ref[0] * (dh ** -0.5)
    red = jnp.sum(mk_ref[...].reshape(n_mem, 8, LANES) * q8[None], axis=-1, keepdims=True)
    own_lane = _iota((LANES, 8, LANES), 0) == _iota((LANES, 8, LANES), 2)
    n_t = n_mem // LANES
    tiles = [jnp.sum(jnp.where(own_lane, red[j * LANES:(j + 1) * LANES], 0.0), axis=0) for j in range(n_t)]
    s8 = jnp.concatenate(tiles, axis=1)
    s = s8[0:H_X] + s8[H_X:2 * H_X]
    p = jnp.exp(s - jnp.max(s, axis=-1, keepdims=True))
    inv_l = 1.0 / jnp.sum(p, axis=-1, keepdims=True)
    p8 = jnp.concatenate([p, p], axis=0)
    ones = jnp.ones((LANES, LANES), BF16)
    acc = None
    for j in range(n_t):
        z = jnp.where(own_lane, p8[None, :, j * LANES:(j + 1) * LANES], 0.0).reshape(LANES * 8, LANES)
        p3 = jnp.dot(z.astype(BF16), ones, preferred_element_type=F32).reshape(LANES, 8, LANES)
        v3 = mv_ref[j * LANES * 8:(j + 1) * LANES * 8, :].reshape(LANES, 8, LANES)
        part = jnp.sum(p3 * v3, axis=0)
        acc = part if acc is None else acc + part
    o_ref[0] = acc * jnp.concatenate([inv_l, inv_l], axis=0)


def _mem_rows(cache):
    dp, db, nm = cache.shape[:3]
    x = cache.reshape(dp, db, nm, H_X, 2, LANES)
    return jnp.transpose(x, (0, 1, 2, 4, 3, 5)).reshape(dp, db, nm * 8, LANES)


def _xattn_sample(q, mk_rows, mv_rows, layer):
    db, d = q.shape
    n_rows = mk_rows.shape[2]
    q8 = jnp.transpose(q.reshape(db, H_X, 2, LANES), (0, 2, 1, 3)).reshape(db, 8, LANES)
    r3 = lambda i: (i, 0, 0)
    mem = lambda i: (layer, i, 0, 0)
    o8 = pl.pallas_call(
        _xattn_sample_kernel, grid=(db,),
        in_specs=[pl.BlockSpec((1, 8, LANES), r3), pl.BlockSpec((None, None, n_rows, LANES), mem),
                  pl.BlockSpec((None, None, n_rows, LANES), mem)],
        out_specs=pl.BlockSpec((1, 8, LANES), r3), out_shape=jax.ShapeDtypeStruct((db, 8, LANES), F32),
        compiler_params=_cparams("parallel"), name="xattn_sample")(q8, mk_rows, mv_rows)
    return jnp.transpose(o8.reshape(db, 2, H_X, LANES), (0, 2, 1, 3)).reshape(db, d)


def _head_rows(h):
    return slice(h * DH_B, (h + 1) * DH_B)


def _fox_sample_kernel(pt_ref, q_ref, kn_ref, vn_ref, f_ref, bf_ref, *refs, g_pages):
    k_refs = refs[0:g_pages]
    v_refs = refs[g_pages:2 * g_pages]
    lf_refs = refs[2 * g_pages:3 * g_pages]
    o_ref, lfn_ref, qb_ref, m_ref, l_ref, acc_ref, carry_ref = refs[3 * g_pages:]
    del pt_ref
    w = H_B * DH_B
    gi = pl.program_id(1)
    bcast = lambda col: jnp.broadcast_to(col, (H_B, LANES))
    head_id = _iota((H_B, LANES), 0)

    def rows_to_tile(rows):
        n = rows[0].shape[1]
        out = jnp.broadcast_to(rows[0], (H_B, n))
        for h in range(1, H_B):
            out = jnp.where(head_id[:, 0:n] == h, rows[h], out)
        return out

    @pl.when(gi == 0)
    def _():
        q = q_ref[0] * (DH_B ** -0.5)
        qb_ref[...] = jnp.broadcast_to(_col_from_row(q, w), (w, LANES))
        lfn = _log_sigmoid(f_ref[0] + bf_ref[...])
        lfn_ref[0] = lfn
        qk = q * kn_ref[0]
        s_new = rows_to_tile([jnp.sum(qk[:, _head_rows(h)], axis=1, keepdims=True) for h in range(H_B)])
        m_ref[...] = bcast(s_new)
        l_ref[...] = jnp.ones_like(l_ref)
        acc_ref[...] = jnp.where(_iota((w, LANES), 1) == 0, _col_from_row(vn_ref[0], w), 0.0)
        carry_ref[...] = bcast(lfn)

    later = (_iota((PAGE, PAGE), 0) > _iota((PAGE, PAGE), 1)).astype(BF16)
    m = m_ref[:, 0:1]
    l = l_ref[:, 0:1]
    carry = carry_ref[:, 0:1]
    logits = [None] * g_pages
    top = None
    for jj in reversed(range(g_pages)):
        s_t = rows_to_tile([jnp.sum(k_refs[jj][h] * qb_ref[_head_rows(h), :], axis=0, keepdims=True)
                            for h in range(H_B)])
        lf_t = lf_refs[jj][...]
        logits[jj] = s_t + _dot3_r(lf_t, later) + carry
        carry = carry + jnp.sum(lf_t, axis=1, keepdims=True)
        top = logits[jj] if top is None else jnp.maximum(top, logits[jj])
    m_new = jnp.maximum(m, jnp.max(top, axis=1, keepdims=True))
    alpha = jnp.exp(m - m_new)
    probs = [jnp.exp(lg - m_new) for lg in logits]
    p_sum = probs[0]
    for p_t in probs[1:]:
        p_sum = p_sum + p_t
    for h in range(H_B):
        acc_h = acc_ref[_head_rows(h), :] * alpha[h:h + 1, :]
        for jj in range(g_pages):
            acc_h = acc_h + v_refs[jj][h] * probs[jj][h:h + 1, :]
        acc_ref[_head_rows(h), :] = acc_h
    l = alpha * l + jnp.sum(p_sum, axis=1, keepdims=True)
    m_ref[...] = bcast(m_new)
    l_ref[...] = bcast(l)
    carry_ref[...] = bcast(carry)

    @pl.when(gi == pl.num_programs(1) - 1)
    def _():
        inv = 1.0 / l
        den = jnp.concatenate([jnp.broadcast_to(inv[h:h + 1, :], (DH_B, 1)) for h in range(H_B)], axis=0)
        col = jnp.sum(acc_ref[...], axis=1, keepdims=True) * den
        eye = _iota((w, w), 0) == _iota((w, w), 1)
        o_ref[0] = jnp.sum(jnp.where(eye, jnp.broadcast_to(col, (w, w)), 0.0), axis=0, keepdims=True)


def _fox_sample(q, k_new, v_new, f_col, bf_col, cache_kt, cache_vt, cache_lft, page_table, g_pages=16):
    db = q.shape[0]
    w = H_B * DH_B
    n_pages = page_table.shape[1]
    g_pages = min(g_pages, n_pages)
    ng = n_pages // g_pages
    r3 = lambda i, g, pt: (i, 0, 0)

    def page_map(nd):
        return [(lambda i, g, pt, jj=jj: (pt[i * n_pages + (ng - 1 - g) * g_pages + jj],) + (0,) * nd)
                for jj in range(g_pages)]

    in_specs = [pl.BlockSpec((1, 1, w), r3)] * 3 + [pl.BlockSpec((1, H_B, 1), r3),
                                                    pl.BlockSpec(bf_col.shape, lambda i, g, pt: (0, 0))]
    in_specs += [pl.BlockSpec((None, H_B, DH_B, PAGE), mp) for mp in page_map(3)] * 2
    in_specs += [pl.BlockSpec((None, H_B, PAGE), mp) for mp in page_map(2)]
    grid_spec = pltpu.PrefetchScalarGridSpec(
        num_scalar_prefetch=1, grid=(db, ng), in_specs=in_specs,
        out_specs=[pl.BlockSpec((1, 1, w), r3), pl.BlockSpec((1, H_B, 1), r3)],
        scratch_shapes=[pltpu.VMEM((w, LANES), F32), pltpu.VMEM((H_B, LANES), F32), pltpu.VMEM((H_B, LANES), F32),
                        pltpu.VMEM((w, LANES), F32), pltpu.VMEM((H_B, LANES), F32)])
    kern = functools.partial(_fox_sample_kernel, g_pages=g_pages)
    return pl.pallas_call(
        kern, grid_spec=grid_spec,
        out_shape=[jax.ShapeDtypeStruct((db, 1, w), F32), jax.ShapeDtypeStruct((db, H_B, 1), F32)],
        compiler_params=_cparams("parallel", "arbitrary"), name="fox_sample")(
            page_table.reshape(-1), q, k_new, v_new, f_col, bf_col,
            *([cache_kt] * g_pages), *([cache_vt] * g_pages), *([cache_lft] * g_pages))


def _row(v):
    return v.reshape(1, -1).astype(F32)


def _pad_lanes(v, start):
    return jnp.zeros((1, LANES), F32).at[0, start:start + v.shape[0]].set(v)


def _chunk_rows(cols, b, l):
    n = l // CHUNK_A
    return cols.reshape(b, n, CHUNK_A, H_A).transpose(0, 1, 3, 2).reshape(b, n, H_A * CHUNK_A)


def kernel(x_prompt, x_sample, mem_prompt, cache_fox_k, cache_fox_v, cache_fox_logf, page_table, state_gdn, state_gdn_conv, state_cfm_conv, cache_mem_k, cache_mem_v, norm_mix, w_in_e, conv_a, a_log, dt_bias, gnorm_a, b_f, w_out_e, w_pw1, b_pw1, w_dw, b_dw, ln_g, ln_b, w_pw2, norm_mem, norm_x, w_xq, w_xkv, w_xo, norm_f, w_up, w_down, norm_out):
    b, l, d = x_prompt.shape
    db = x_sample.shape[0]
    n_mem = mem_prompt.shape[1]
    depth = norm_mix.shape[0]
    dh_x = d // H_X
    bf = lambda w: w.astype(BF16)

    w_in = w_in_e[0]
    off_aa = W_CONV_A + W_V_A
    off_qb = off_aa + 2 * H_A
    off_fb = off_qb + 3 * W_B
    w_small = jnp.concatenate([w_in[:, off_aa:off_qb], w_in[:, off_fb:]], axis=1)
    w_small = jnp.pad(w_small, ((0, 0), (0, LANES - w_small.shape[1])))
    w_in_r = bf(jnp.concatenate([w_in[:, :off_aa], w_in[:, off_qb:off_fb], w_small], axis=1))
    in_widths = (W_CONV_A, W_V_A, W_B, W_B, W_B, LANES)
    off_kb = off_qb + W_B
    off_vb = off_kb + W_B
    w_in_p = bf(jnp.concatenate([w_in[:, :off_aa], w_in[:, off_qb:off_kb], w_in[:, off_vb:off_fb], w_small], axis=1))
    in_widths_p = (W_CONV_A, W_V_A, W_B, W_B, LANES)
    w_in_t = [bf(w_in[:, off_kb:off_vb].T), bf(w_in[:, off_vb:off_fb].T), bf(w_small.T)]
    w_out_a = bf(w_out_e[0][:W_V_A])
    w_out_b = bf(w_out_e[0][W_V_A:])
    w_xq_b = [bf(w_xq[i]) for i in range(depth)]
    w_xo_b = [bf(w_xo[i]) for i in range(depth)]
    w_up_b = [bf(w_up[i]) for i in range(depth)]
    w_down_b = [bf(w_down[i]) for i in range(depth)]
    w_pw1_b = bf(w_pw1[0])
    w_pw2_b = bf(w_pw2[0])
    alog_rows = _row(jnp.repeat(a_log[0], CHUNK_A))
    dt_rows = _row(jnp.repeat(dt_bias[0], CHUNK_A))
    gn = _row(gnorm_a[0])

    memf = mem_prompt.reshape(b * n_mem, d)
    mem_k, mem_v = [], []
    for i in range(depth):
        mk, mv = _linear([memf], [bf(w_xkv[i])], gain=_row(norm_mem[i]), out_widths=(d, d), name="mem_kv")
        mem_k.append(mk.reshape(b, n_mem, d))
        mem_v.append(mv.reshape(b, n_mem, d))
    mem_k_prompt = jnp.stack(mem_k).reshape(depth, b, n_mem, H_X, dh_x)
    mem_v_prompt = jnp.stack(mem_v).reshape(depth, b, n_mem, H_X, dh_x)

    xp = x_prompt.reshape(b * l, d)
    conv_in, z, q, v, small, k_t, v_t, small_t = _linear(
        [xp], [w_in_p], gain=_row(norm_mix[0]), out_widths=in_widths_p, wts=w_in_t, t_batch=(b, l), name="in_proj")
    fox_k_prompt = jnp.transpose(k_t.reshape(b, H_B, DH_B, l), (0, 3, 1, 2))[None]
    fox_v_prompt = jnp.transpose(v_t.reshape(b, H_B, DH_B, l), (0, 3, 1, 2))[None]
    conv_in3 = conv_in.reshape(b, l, W_CONV_A)
    gdn_conv_prompt = conv_in3[:, l - (CONV_A - 1):, :][None]
    oa, s_fin = _gdn_prompt(conv_in3, z.reshape(b, l, W_V_A),
                            _chunk_rows(small[:, SM_A:SM_A + H_A], b, l),
                            _chunk_rows(small[:, SM_B:SM_B + H_A], b, l),
                            conv_a[0], alog_rows, dt_rows, gn)
    gdn_state_prompt = s_fin.reshape(b, DK_A, H_A, DV_A).transpose(0, 2, 1, 3)[None]
    qf, kft, vf, lft = _fox_prep(q.reshape(b, l, W_B), k_t, v.reshape(b, l, W_B), small.reshape(b, l, LANES),
                                 small_t, _pad_lanes(b_f[0], SM_F), b_f[0].reshape(H_B, 1))
    fox_logf_prompt = jnp.transpose(lft, (0, 2, 1))[None]
    ob = _fox_flash(qf, kft, vf)
    x = _linear([oa.reshape(b * l, W_V_A), ob.reshape(b * l, W_B)], [w_out_a, w_out_b], res=xp, name="out_proj")
    x = _xattn_prompt(x.reshape(b, l, d), _row(norm_x[0]), w_xq_b[0], bf(mem_k[0]), bf(mem_v[0]), w_xo_b[0])
    x = _mlp(x.reshape(b * l, d), _row(norm_f[0]), w_up_b[0], w_down_b[0])
    glu = _linear([x], [w_pw1_b], gain=_row(norm_mix[1]), bias=_row(b_pw1[0]), act="glu", name="pw1_glu")
    glu3 = glu.reshape(b, l, d)
    cfm_conv_prompt = glu3[:, l - (CONV_C - 1):, :][None]
    x = _cfm_prompt(glu3, x.reshape(b, l, d), w_dw[0], _row(b_dw[0]), _row(ln_g[0]), _row(ln_b[0]), w_pw2_b)
    x = _xattn_prompt(x, _row(norm_x[1]), w_xq_b[1], bf(mem_k[1]), bf(mem_v[1]), w_xo_b[1])
    y_prompt = _mlp(x.reshape(b * l, d), _row(norm_f[1]), w_up_b[1], w_down_b[1],
                    final_gain=_row(norm_out)).reshape(b, l, d)

    xs = x_sample.reshape(db, d)
    conv_s, z_s, q_s, k_s, v_s, small_s = _linear([xs], [w_in_r], gain=_row(norm_mix[0]),
                                                  out_widths=in_widths, name="in_proj_s")
    fox_k_sample = k_s.reshape(1, db, 1, H_B, DH_B)
    fox_v_sample = v_s.reshape(1, db, 1, H_B, DH_B)
    full = jnp.concatenate([state_gdn_conv[0], conv_s[:, None, :]], axis=1)
    gdn_conv_sample = full[:, 1:, :][None]
    oa_s, s_new = _gdn_sample(full, conv_a[0], small_s.reshape(db, 1, LANES), _pad_lanes(a_log[0], SM_A),
                              _pad_lanes(dt_bias[0], SM_A), z_s.reshape(db, 1, W_V_A), gn, state_gdn[0])
    gdn_state_sample = s_new[None]
    ob_s, lf_s = _fox_sample(q_s.reshape(db, 1, W_B), k_s.reshape(db, 1, W_B), v_s.reshape(db, 1, W_B),
                             small_s[:, SM_F:SM_F + H_B].reshape(db, H_B, 1), b_f[0].reshape(H_B, 1),
                             jnp.transpose(cache_fox_k[0], (0, 2, 3, 1)), jnp.transpose(cache_fox_v[0], (0, 2, 3, 1)),
                             jnp.transpose(cache_fox_logf[0], (0, 2, 1)), page_table)
    fox_logf_sample = lf_s.reshape(1, db, 1, H_B)
    x = _linear([oa_s.reshape(db, W_V_A), ob_s.reshape(db, W_B)], [w_out_a, w_out_b], res=xs, name="out_proj_s")

    mk_rows = _mem_rows(cache_mem_k)
    mv_rows = _mem_rows(cache_mem_v)

    def xattn_s(x, i):
        qx = _linear([x], [w_xq_b[i]], gain=_row(norm_x[i]), name="xq_s")
        o = _xattn_sample(qx, mk_rows, mv_rows, i)
        return _linear([o], [w_xo_b[i]], res=x, name="xo_s")

    x = xattn_s(x, 0)
    x = _mlp(x, _row(norm_f[0]), w_up_b[0], w_down_b[0])
    glu_s = _linear([x], [w_pw1_b], gain=_row(norm_mix[1]), bias=_row(b_pw1[0]), act="glu", name="pw1_glu_s")
    cfm_conv_sample = jnp.concatenate([state_cfm_conv[0][:, 1:, :], glu_s[:, None, :]], axis=1)[None]
    x = _cfm_sample(state_cfm_conv[0].transpose(1, 0, 2), glu_s, x, w_dw[0], _row(b_dw[0]), _row(ln_g[0]),
                    _row(ln_b[0]), w_pw2_b)
    x = xattn_s(x, 1)
    y_sample = _mlp(x, _row(norm_f[1]), w_up_b[1], w_down_b[1], final_gain=_row(norm_out)).reshape(db, 1, d)

    return (y_prompt, y_sample, fox_k_prompt, fox_v_prompt, fox_logf_prompt, fox_k_sample, fox_v_sample,
            fox_logf_sample, gdn_state_prompt, gdn_conv_prompt, gdn_state_sample, gdn_conv_sample,
            cfm_conv_prompt, cfm_conv_sample, mem_k_prompt, mem_v_prompt)
```

```python
import functools
import math

import jax
import jax.numpy as jnp
import numpy as np
from jax import lax
from jax.experimental import pallas as pl
from jax.experimental.pallas import tpu as pltpu

F32 = jnp.float32
BF16 = jnp.bfloat16
EPS = 1e-6
NEG = -1e30
LOG2E = 1.4426950408889634

H_A = 4
DK_A = 128
DV_A = 128
CONV_A = 4
CHUNK_A = 64
H_B = 8
DH_B = 64
H_X = 4
CONV_C = 31
PAGE = 128

W_QK_A = H_A * DK_A
W_V_A = H_A * DV_A
W_CONV_A = 2 * W_QK_A + W_V_A
W_B = H_B * DH_B
LANES = 128
SM_A = 0
SM_B = H_A
SM_F = 2 * H_A

VMEM_LIMIT = 56 * 1024 * 1024


def _cparams(*sem):
    return pltpu.CompilerParams(dimension_semantics=sem, vmem_limit_bytes=VMEM_LIMIT)


def _const_spec(shape):
    nd = len(shape)
    return pl.BlockSpec(shape, lambda *_: (0,) * nd)


def _rms(x, g):
    return x * lax.rsqrt(jnp.mean(x * x, axis=-1, keepdims=True) + EPS) * g


def _silu(x):
    return x * jax.nn.sigmoid(x)


def _softplus(x):
    return jnp.maximum(x, 0.0) + jnp.log1p(jnp.exp(-jnp.abs(x)))


def _split3(x):
    h = x.astype(BF16)
    r = x - h.astype(F32)
    m = r.astype(BF16)
    l = (r - m.astype(F32)).astype(BF16)
    return h, m, l


def _dot3_l(mat01, x):
    h, m, l = _split3(x)
    d = lambda t: jnp.dot(mat01, t, preferred_element_type=F32)
    return d(h) + d(m) + d(l)


def _dot3_r(x, mat01):
    h, m, l = _split3(x)
    d = lambda t: jnp.dot(t, mat01, preferred_element_type=F32)
    return d(h) + d(m) + d(l)


def _dot_nt(a, b):
    return lax.dot_general(a, b, (((1,), (1,)), ((), ())), preferred_element_type=F32)


def _dot_tn(a, b):
    return lax.dot_general(a, b, (((0,), (0,)), ((), ())), preferred_element_type=F32)


def _iota(shape, dim):
    return lax.broadcasted_iota(jnp.int32, shape, dim)


def _col_from_row(row, n):
    eye = _iota((n, n), 0) == _iota((n, n), 1)
    return jnp.sum(jnp.where(eye, jnp.broadcast_to(row, (n, n)), 0.0), axis=1, keepdims=True)


def _linear_kernel(*refs, n_in, n_t, has_gain, has_bias, has_res, act, out_widths, chunk):
    it = iter(refs)
    x_refs = [next(it) for _ in range(n_in)]
    w_refs = [next(it) for _ in range(n_in)]
    wt_refs = [next(it) for _ in range(n_t)]
    gain_ref = next(it) if has_gain else None
    bias_ref = next(it) if has_bias else None
    res_ref = next(it) if has_res else None
    out_refs = [next(it) for _ in out_widths]
    outt_refs = [next(it) for _ in range(n_t)]
    xs = []
    for i, xr in enumerate(x_refs):
        x = xr[...]
        if has_gain and i == 0:
            x = _rms(x, gain_ref[...])
        xs.append(x.astype(BF16))
    n_total = sum(out_widths)

    def mm(col0, cw):
        acc = None
        for xb, wr in zip(xs, w_refs):
            d = jnp.dot(xb, wr[:, col0:col0 + cw], preferred_element_type=F32)
            acc = d if acc is None else acc + d
        if has_bias:
            acc = acc + bias_ref[:, col0:col0 + cw]
        return acc

    off = 0
    for o_ref, width in zip(out_refs, out_widths):
        for c0 in range(0, width, chunk):
            cw = min(chunk, width - c0)
            y = mm(off + c0, cw)
            if act == "glu":
                y = y * jax.nn.sigmoid(mm(n_total + off + c0, cw))
            if has_res:
                y = y + res_ref[:, off + c0:off + c0 + cw]
            o_ref[:, c0:c0 + cw] = y
        off += width
    for wt_ref, ot_ref in zip(wt_refs, outt_refs):
        ot_ref[0] = _dot_nt(wt_ref[...], xs[0])


def _linear(xs, ws, *, gain=None, bias=None, res=None, act=None, out_widths=None, wts=(), t_batch=None,
            tm=512, chunk=512, name="linear"):
    m = xs[0].shape[0]
    tm = min(tm, m)
    assert m % tm == 0
    n_mm = ws[0].shape[1]
    n_out = n_mm // 2 if act == "glu" else n_mm
    if out_widths is None:
        out_widths = (n_out,)
    assert sum(out_widths) == n_out
    row = lambda i: (i, 0)
    in_specs = [pl.BlockSpec((tm, x.shape[1]), row) for x in xs]
    in_specs += [_const_spec(w.shape) for w in ws]
    in_specs += [_const_spec(w.shape) for w in wts]
    args = list(xs) + list(ws) + list(wts)
    if gain is not None:
        in_specs.append(_const_spec(gain.shape)); args.append(gain)
    if bias is not None:
        in_specs.append(_const_spec(bias.shape)); args.append(bias)
    if res is not None:
        in_specs.append(pl.BlockSpec((tm, res.shape[1]), row)); args.append(res)
    out_shape = [jax.ShapeDtypeStruct((m, w), F32) for w in out_widths]
    out_specs = [pl.BlockSpec((tm, w), row) for w in out_widths]
    if wts:
        tb, tl = t_batch
        assert tb * tl == m and tl % tm == 0
        per = tl // tm
        out_shape += [jax.ShapeDtypeStruct((tb, w.shape[0], tl), F32) for w in wts]
        out_specs += [pl.BlockSpec((1, w.shape[0], tm), lambda i: (i // per, 0, i % per)) for w in wts]
    kern = functools.partial(_linear_kernel, n_in=len(xs), n_t=len(wts), has_gain=gain is not None,
                             has_bias=bias is not None, has_res=res is not None, act=act,
                             out_widths=tuple(out_widths), chunk=chunk)
    outs = pl.pallas_call(kern, grid=(m // tm,), in_specs=in_specs, out_specs=out_specs,
                          out_shape=out_shape, compiler_params=_cparams("parallel"), name=name)(*args)
    return outs if len(outs) > 1 else outs[0]


def _mlp_kernel(*refs, chunk, final_norm):
    if final_norm:
        x_ref, g_ref, wu_ref, wd_ref, go_ref, o_ref = refs
    else:
        x_ref, g_ref, wu_ref, wd_ref, o_ref = refs
    x = x_ref[...]
    xn = _rms(x, g_ref[...]).astype(BF16)
    acc = x
    d_ff = wu_ref.shape[1]
    for c0 in range(0, d_ff, chunk):
        h = jnp.dot(xn, wu_ref[:, c0:c0 + chunk], preferred_element_type=F32)
        h = jnp.square(jnp.maximum(h, 0.0)).astype(BF16)
        acc = acc + jnp.dot(h, wd_ref[c0:c0 + chunk, :], preferred_element_type=F32)
    if final_norm:
        acc = _rms(acc, go_ref[...])
    o_ref[...] = acc


def _mlp(x, gain, w_up, w_down, final_gain=None, tm=512, chunk=512):
    m, d = x.shape
    tm = min(tm, m)
    row = lambda i: (i, 0)
    in_specs = [pl.BlockSpec((tm, d), row), _const_spec(gain.shape), _const_spec(w_up.shape),
                _const_spec(w_down.shape)]
    args = [x, gain, w_up, w_down]
    if final_gain is not None:
        in_specs.append(_const_spec(final_gain.shape)); args.append(final_gain)
    kern = functools.partial(_mlp_kernel, chunk=chunk, final_norm=final_gain is not None)
    return pl.pallas_call(kern, grid=(m // tm,), in_specs=in_specs,
                          out_specs=pl.BlockSpec((tm, d), row),
                          out_shape=jax.ShapeDtypeStruct((m, d), F32),
                          compiler_params=_cparams("parallel"), name="mlp")(*args)


def _xattn_kernel(x_ref, g_ref, wq_ref, mk_ref, mv_ref, wo_ref, o_ref):
    x = x_ref[0]
    d = x.shape[1]
    dh = d // H_X
    xn = _rms(x, g_ref[...]).astype(BF16)
    q = jnp.dot(xn, wq_ref[...], preferred_element_type=F32) * (dh ** -0.5)
    q = q.astype(BF16)
    head = lambda h: slice(h * dh, (h + 1) * dh)
    scores = lambda h: _dot_nt(q[:, head(h)], mk_ref[0, :, head(h)])
    outs = []
    s_next = scores(0)
    for h in range(H_X):
        sl = head(h)
        s = s_next
        if h + 1 < H_X:
            s_next = scores(h + 1)
        p = jnp.exp(s - jnp.max(s, axis=-1, keepdims=True))
        l = jnp.sum(p, axis=-1, keepdims=True)
        o = jnp.dot(p.astype(BF16), mv_ref[0, :, sl], preferred_element_type=F32)
        outs.append((o * (1.0 / l)).astype(BF16))
    o = jnp.concatenate(outs, axis=1)
    o_ref[0] = x + jnp.dot(o, wo_ref[...], preferred_element_type=F32)


def _xattn_prompt(x, gain, wq, mk, mv, wo, tm=512):
    b, l, d = x.shape
    tm = min(tm, l)
    nm = mk.shape[1]
    blk = lambda bi, i: (bi, i, 0)
    mem = lambda bi, i: (bi, 0, 0)
    return pl.pallas_call(
        _xattn_kernel, grid=(b, l // tm),
        in_specs=[pl.BlockSpec((1, tm, d), blk), _const_spec(gain.shape), _const_spec(wq.shape),
                  pl.BlockSpec((1, nm, d), mem), pl.BlockSpec((1, nm, d), mem), _const_spec(wo.shape)],
        out_specs=pl.BlockSpec((1, tm, d), blk),
        out_shape=jax.ShapeDtypeStruct((b, l, d), F32),
        compiler_params=_cparams("parallel", "parallel"), name="xattn_prompt")(x, gain, wq, mk, mv, wo)


def _log_sigmoid(x):
    return jnp.minimum(x, 0.0) - jnp.log1p(jnp.exp(-jnp.abs(x)))


def _head_lanes(x, h):
    p = h // 2
    blk = x[:, p * LANES:(p + 1) * LANES]
    if h % 2:
        blk = pltpu.roll(blk, DH_B, 1)
    return blk


def _fox_prep_kernel(q_ref, kt_ref, v_ref, sm_ref, smt_ref, bf_ref, bfc_ref, qf_ref, kft_ref, vf_ref, lft_ref,
                     carry_ref, carryt_ref):
    t = q_ref.shape[1]

    @pl.when(pl.program_id(1) == 0)
    def _():
        carry_ref[...] = jnp.zeros_like(carry_ref)
        carryt_ref[...] = jnp.zeros_like(carryt_ref)

    lane = _iota((t, LANES), 1)
    gate_lane = (lane >= SM_F) & (lane < SM_F + H_B)
    logf = jnp.where(gate_lane, _log_sigmoid(sm_ref[0] + bf_ref[...]), 0.0)
    before = _iota((t, t), 0) >= _iota((t, t), 1)
    c = _dot3_l(before.astype(BF16), logf) + carry_ref[0:1, :]
    carry_ref[0:1, :] = c[t - 1:t, :]
    c1, c2, c3 = _split3(c * LOG2E)
    c1, c2, c3 = c1.astype(F32), c2.astype(F32), c3.astype(F32)
    lft = _log_sigmoid(smt_ref[0] + bfc_ref[...])
    lft_ref[0] = lft
    ct = _dot3_r(lft, (_iota((t, t), 0) <= _iota((t, t), 1)).astype(BF16)) + carryt_ref[:, 0:1]
    carryt_ref[...] = jnp.broadcast_to(ct[:, t - 1:t], carryt_ref.shape)
    r1, r2, r3 = _split3(ct * LOG2E)
    r1, r2, r3 = r1.astype(F32), r2.astype(F32), r3.astype(F32)
    q = q_ref[0] * (DH_B ** -0.5 * LOG2E)
    v = v_ref[0]
    low = lane < DH_B
    row = _iota((DH_B, t), 0)
    for h in range(H_B):
        col = slice(SM_F + h, SM_F + h + 1)
        b1, b2, b3 = c1[:, col], c2[:, col], c3[:, col]
        qx = jnp.where(lane == DH_B, b1, jnp.where(lane == DH_B + 1, b2, jnp.where(
            lane == DH_B + 2, b3, jnp.where(lane < DH_B + 6, 1.0, 0.0))))
        kx = jnp.where(row < 3, 1.0, jnp.where(row == 3, -r1[h:h + 1, :], jnp.where(
            row == 4, -r2[h:h + 1, :], jnp.where(row == 5, -r3[h:h + 1, :], 0.0))))
        vx = jnp.where(lane == DH_B, 1.0, 0.0)
        qf_ref[0, h] = jnp.where(low, _head_lanes(q, h), qx).astype(BF16)
        kft_ref[0, h] = jnp.concatenate([kt_ref[0, h * DH_B:(h + 1) * DH_B, :], kx], axis=0).astype(BF16)
        vf_ref[0, h] = jnp.where(low, _head_lanes(v, h), vx).astype(BF16)


def _fox_prep(q, kt, v, small, small_t, bf_row, bf_col, t=256):
    b, l, w = q.shape
    t = min(t, l)
    blk = lambda bi, i: (bi, i, 0)
    tblk = lambda bi, i: (bi, 0, i)
    hblk = lambda bi, i: (bi, 0, i, 0)
    hshape = jax.ShapeDtypeStruct((b, H_B, l, LANES), BF16)
    return pl.pallas_call(
        _fox_prep_kernel, grid=(b, l // t),
        in_specs=[pl.BlockSpec((1, t, w), blk), pl.BlockSpec((1, w, t), tblk), pl.BlockSpec((1, t, w), blk),
                  pl.BlockSpec((1, t, LANES), blk),
                  pl.BlockSpec((1, H_B, t), lambda bi, i: (bi, SM_F // H_B, i)),
                  _const_spec(bf_row.shape), _const_spec(bf_col.shape)],
        out_specs=[pl.BlockSpec((1, H_B, t, LANES), hblk),
                   pl.BlockSpec((1, H_B, LANES, t), lambda bi, i: (bi, 0, 0, i)),
                   pl.BlockSpec((1, H_B, t, LANES), hblk), pl.BlockSpec((1, H_B, t), tblk)],
        out_shape=[hshape, jax.ShapeDtypeStruct((b, H_B, LANES, l), BF16), hshape,
                   jax.ShapeDtypeStruct((b, H_B, l), F32)],
        scratch_shapes=[pltpu.VMEM((8, LANES), F32), pltpu.VMEM((H_B, LANES), F32)],
        compiler_params=_cparams("parallel", "arbitrary"), name="fox_prep")(
            q, kt, v, small, small_t, bf_row, bf_col)


def _fox_flash_kernel(it_ref, jt_ref, qf_ref, kft_ref, vf_ref, o_ref, m_ref, acc_ref):
    p_id = pl.program_id(1)
    i = it_ref[p_id]
    j = jt_ref[p_id]
    tq = qf_ref.shape[2]
    tk = kft_ref.shape[3]

    @pl.when(j == 0)
    def _():
        m_ref[...] = jnp.full_like(m_ref, NEG)
        acc_ref[...] = jnp.zeros_like(acc_ref)

    def heads(masked):
        def scores(h):
            s = jnp.dot(qf_ref[0, h], kft_ref[0, h], preferred_element_type=F32)
            if masked:
                s = jnp.where(_iota((tq, tk), 1) <= _iota((tq, tk), 0), s, NEG)
            return s

        s_next = scores(0)
        for h in range(H_B):
            s = s_next
            if h + 1 < H_B:
                s_next = scores(h + 1)
            m_prev = m_ref[h]
            m_new = jnp.maximum(m_prev, jnp.max(s, axis=-1, keepdims=True))
            alpha = jnp.exp2(m_prev - m_new)
            p = jnp.exp2(s - jnp.concatenate([m_new] * (tk // LANES), axis=1)).astype(BF16)
            acc_ref[h] = alpha * acc_ref[h] + jnp.dot(p, vf_ref[0, h], preferred_element_type=F32)
            m_ref[h] = m_new

    @pl.when(j < i)
    def _():
        heads(False)

    @pl.when(j == i)
    def _():
        heads(True)
        lane = _iota((tq, LANES), 1)
        for pr in range(H_B // 2):
            a0 = acc_ref[2 * pr]
            a1 = acc_ref[2 * pr + 1]
            o0 = a0 * (1.0 / a0[:, DH_B:DH_B + 1])
            o1 = a1 * (1.0 / a1[:, DH_B:DH_B + 1])
            o_ref[0, :, pr * LANES:(pr + 1) * LANES] = jnp.where(lane < DH_B, o0, pltpu.roll(o1, DH_B, 1))


def _fox_flash(qf, kft, vf, t=512):
    b, hh, l, _ = qf.shape
    t = min(t, l)
    n = l // t
    pairs = [(i, j) for i in range(n) for j in range(i + 1)]
    it = jnp.asarray(np.array([p[0] for p in pairs], np.int32))
    jt = jnp.asarray(np.array([p[1] for p in pairs], np.int32))
    qmap = lambda bi, p, it, jt: (bi, 0, it[p], 0)
    kmap = lambda bi, p, it, jt: (bi, 0, jt[p], 0)
    ktmap = lambda bi, p, it, jt: (bi, 0, 0, jt[p])
    omap = lambda bi, p, it, jt: (bi, it[p], 0)
    grid_spec = pltpu.PrefetchScalarGridSpec(
        num_scalar_prefetch=2, grid=(b, len(pairs)),
        in_specs=[pl.BlockSpec((1, hh, t, LANES), qmap), pl.BlockSpec((1, hh, LANES, t), ktmap),
                  pl.BlockSpec((1, hh, t, LANES), kmap)],
        out_specs=pl.BlockSpec((1, t, hh * DH_B), omap),
        scratch_shapes=[pltpu.VMEM((hh, t, LANES), F32), pltpu.VMEM((hh, t, LANES), F32)])
    return pl.pallas_call(
        _fox_flash_kernel, grid_spec=grid_spec,
        out_shape=jax.ShapeDtypeStruct((b, l, hh * DH_B), F32),
        compiler_params=_cparams("parallel", "arbitrary"), name="fox_flash")(it, jt, qf, kft, vf)


def _gdn_kernel(x_ref, z_ref, a_ref, b_ref, cw_ref, al_ref, dt_ref, gn_ref, oa_ref, s_ref,
                xs_ref, st_ref, *, nc):
    c = CHUNK_A
    r = H_A * c
    t = nc * c
    halo = 8

    @pl.when(pl.program_id(1) == 0)
    def _():
        xs_ref[0:halo, :] = jnp.zeros((halo, xs_ref.shape[1]), F32)
        st_ref[...] = jnp.zeros_like(st_ref)

    @pl.when(pl.program_id(1) != 0)
    def _():
        xs_ref[0:halo, :] = xs_ref[t:t + halo, :]

    xs_ref[halo:halo + t, :] = x_ref[0]
    y = None
    for jj in range(CONV_A):
        o = halo - (CONV_A - 1) + jj
        term = xs_ref[o:o + t, :] * cw_ref[jj:jj + 1, :]
        y = term if y is None else y + term
    y = _silu(y)

    g_rows = -jnp.exp(al_ref[...]) * _softplus(a_ref[0] + dt_ref[...])
    beta_rows = jax.nn.sigmoid(b_ref[0])
    ri = _iota((r, r), 0)
    ci = _iota((r, r), 1)
    same = (ri // c) == (ci // c)
    incl = same & (ri >= ci)
    strict = same & (ri > ci)
    gc_rows = _dot3_r(g_rows, (same & (ri <= ci)).astype(BF16))
    eye = ri == ci
    last = ((ci % c) == (c - 1)) & same

    def stack(a):
        return jnp.concatenate([a[:, h * DK_A:(h + 1) * DK_A] for h in range(H_A)], axis=0)

    pre = []
    for n in range(nc):
        rows = slice(n * c, (n + 1) * c)
        gc_r = gc_rows[n:n + 1, :]
        gcb = jnp.broadcast_to(gc_r, (r, r))
        gc_c = jnp.sum(jnp.where(eye, gcb, 0.0), axis=1, keepdims=True)
        gl_c = jnp.sum(jnp.where(last, gcb, 0.0), axis=1, keepdims=True)
        beta_c = jnp.sum(jnp.where(eye, jnp.broadcast_to(beta_rows[n:n + 1, :], (r, r)), 0.0),
                         axis=1, keepdims=True)
        q = stack(y[rows, 0:W_QK_A])
        k = stack(y[rows, W_QK_A:2 * W_QK_A])
        v = stack(y[rows, 2 * W_QK_A:W_CONV_A])
        q = q * lax.rsqrt(jnp.sum(q * q, axis=1, keepdims=True) + EPS) * (DK_A ** -0.5)
        k = k * lax.rsqrt(jnp.sum(k * k, axis=1, keepdims=True) + EPS)
        decay = jnp.exp(jnp.where(incl, gc_c - gc_r, -jnp.inf))
        kb = k * beta_c
        k16 = k.astype(BF16)
        eg = jnp.exp(gc_c)
        pre.append(dict(decay=decay, kk=_dot_nt(kb.astype(BF16), k16), qk=_dot_nt(q.astype(BF16), k16),
                        rhs=jnp.concatenate([v * beta_c, kb * eg], axis=1).astype(BF16),
                        qg=(q * eg).astype(BF16), kdec=(k * jnp.exp(gl_c - gc_c)).astype(BF16),
                        dec=[jnp.exp(gl_c[h * c:h * c + 1, :]) for h in range(H_A)]))
    for pc in pre:
        pc["a_qk"] = jnp.where(incl, pc["qk"] * pc["decay"], 0.0).astype(BF16)
        pc["mneg"] = jnp.where(strict, -(pc["kk"] * pc["decay"]), 0.0)
        pc["tinv"] = jnp.where(eye, 1.0, 0.0) + pc["mneg"]
    for _ in range(int(math.log2(c)) - 1):
        for pc in pre:
            m16 = pc["mneg"].astype(BF16)
            pc["mneg"] = jnp.dot(m16, m16, preferred_element_type=F32)
        for pc in pre:
            pc["tinv"] = pc["tinv"] + jnp.dot(pc["tinv"].astype(BF16), pc["mneg"].astype(BF16),
                                              preferred_element_type=F32)
    for pc in pre:
        uw = jnp.dot(pc["tinv"].astype(BF16), pc["rhs"], preferred_element_type=F32)
        pc["u"] = uw[:, 0:DV_A]
        pc["w"] = uw[:, DV_A:2 * DV_A].astype(BF16)

    s_heads = [st_ref[:, h * DV_A:(h + 1) * DV_A] for h in range(H_A)]
    for n in range(nc):
        rows = slice(n * c, (n + 1) * c)
        pc = pre[n]
        v_parts, qs_parts = [], []
        for h in range(H_A):
            hr = slice(h * c, (h + 1) * c)
            wq = jnp.concatenate([pc["w"][hr], pc["qg"][hr]], axis=0)
            wq_s = jnp.dot(wq, s_heads[h].astype(BF16), preferred_element_type=F32)
            v_parts.append(pc["u"][hr] - wq_s[0:c])
            qs_parts.append(wq_s[c:2 * c])
        v_new = jnp.concatenate(v_parts, axis=0).astype(BF16)
        o = jnp.concatenate(qs_parts, axis=0) + jnp.dot(pc["a_qk"], v_new, preferred_element_type=F32)
        for h in range(H_A):
            hr = slice(h * c, (h + 1) * c)
            s_heads[h] = s_heads[h] * pc["dec"][h] + _dot_tn(pc["kdec"][hr], v_new[hr])
        on = o * lax.rsqrt(jnp.mean(o * o, axis=1, keepdims=True) + EPS) * gn_ref[...]
        on = jnp.concatenate([on[h * c:(h + 1) * c, :] for h in range(H_A)], axis=1)
        oa_ref[0, rows, :] = on * _silu(z_ref[0, rows, :])
    for h in range(H_A):
        st_ref[:, h * DV_A:(h + 1) * DV_A] = s_heads[h]

    @pl.when(pl.program_id(1) == pl.num_programs(1) - 1)
    def _():
        s_ref[0] = st_ref[...]


def _gdn_prompt(conv_in, z, a_rows, b_rows, conv_w, alog_row, dt_row, gnorm, nc=8):
    b, l, wc = conv_in.shape
    n = l // CHUNK_A
    nc = min(nc, n)
    t = nc * CHUNK_A
    r = H_A * CHUNK_A
    blk = lambda bi, i: (bi, i, 0)
    kern = functools.partial(_gdn_kernel, nc=nc)
    oa, s = pl.pallas_call(
        kern, grid=(b, n // nc),
        in_specs=[pl.BlockSpec((1, t, wc), blk), pl.BlockSpec((1, t, W_V_A), blk),
                  pl.BlockSpec((1, nc, r), blk), pl.BlockSpec((1, nc, r), blk),
                  _const_spec(conv_w.shape), _const_spec(alog_row.shape), _const_spec(dt_row.shape),
                  _const_spec(gnorm.shape)],
        out_specs=[pl.BlockSpec((1, t, W_V_A), blk),
                   pl.BlockSpec((1, DK_A, H_A * DV_A), lambda bi, i: (bi, 0, 0))],
        out_shape=[jax.ShapeDtypeStruct((b, l, W_V_A), F32),
                   jax.ShapeDtypeStruct((b, DK_A, H_A * DV_A), F32)],
        scratch_shapes=[pltpu.VMEM((t + 8, wc), F32), pltpu.VMEM((DK_A, H_A * DV_A), F32)],
        compiler_params=_cparams("parallel", "arbitrary"), name="gdn_prompt")(
            conv_in, z, a_rows, b_rows, conv_w, alog_row, dt_row, gnorm)
    return oa, s


def _cfm_tail(y, bd_ref, lg_ref, lb_ref, w2_ref, res):
    y = y + bd_ref[...]
    yc = y - jnp.mean(y, axis=-1, keepdims=True)
    yn = yc * lax.rsqrt(jnp.mean(yc * yc, axis=-1, keepdims=True) + EPS) * lg_ref[...] + lb_ref[...]
    return res + jnp.dot(_silu(yn).astype(BF16), w2_ref[...], preferred_element_type=F32)


def _cfm_kernel(u_ref, x_ref, wd_ref, bd_ref, lg_ref, lb_ref, w2_ref, o_ref, xs_ref, sh_ref):
    t = u_ref.shape[1]
    halo = 32

    @pl.when(pl.program_id(1) == 0)
    def _():
        xs_ref[0:halo, :] = jnp.zeros((halo, xs_ref.shape[1]), F32)

    @pl.when(pl.program_id(1) != 0)
    def _():
        xs_ref[0:halo, :] = xs_ref[t:t + halo, :]

    xs_ref[halo:halo + t, :] = u_ref[0]
    y = None
    for r in range(8):
        offs = [o for o in range(halo - (CONV_C - 1), halo + 1) if o % 8 == r]
        if not offs:
            continue
        if r:
            n_rows = max(offs) - r + t
            sh_ref[0:n_rows, :] = xs_ref[r:r + n_rows, :]
        win_ref = sh_ref if r else xs_ref
        for o in offs:
            jj = o - (halo - (CONV_C - 1))
            term = win_ref[o - r:o - r + t, :] * wd_ref[jj:jj + 1, :]
            y = term if y is None else y + term
    o_ref[0] = _cfm_tail(y, bd_ref, lg_ref, lb_ref, w2_ref, x_ref[0])


def _cfm_prompt(u, x, w_dw, b_dw, ln_g, ln_b, w2, t=256):
    b, l, d = u.shape
    t = min(t, l)
    blk = lambda bi, i: (bi, i, 0)
    return pl.pallas_call(
        _cfm_kernel, grid=(b, l // t),
        in_specs=[pl.BlockSpec((1, t, d), blk), pl.BlockSpec((1, t, d), blk), _const_spec(w_dw.shape),
                  _const_spec(b_dw.shape), _const_spec(ln_g.shape), _const_spec(ln_b.shape),
                  _const_spec(w2.shape)],
        out_specs=pl.BlockSpec((1, t, d), blk),
        out_shape=jax.ShapeDtypeStruct((b, l, d), F32),
        scratch_shapes=[pltpu.VMEM((t + 32, d), F32), pltpu.VMEM((t + 32, d), F32)],
        compiler_params=_cparams("parallel", "arbitrary"), name="cfm_prompt")(
            u, x, w_dw, b_dw, ln_g, ln_b, w2)


def _cfm_sample_kernel(buf_ref, u_ref, x_ref, wd_ref, bd_ref, lg_ref, lb_ref, w2_ref, o_ref):
    y = u_ref[...] * wd_ref[CONV_C - 1:CONV_C, :]
    for jj in range(CONV_C - 1):
        y = y + buf_ref[jj] * wd_ref[jj:jj + 1, :]
    o_ref[...] = _cfm_tail(y, bd_ref, lg_ref, lb_ref, w2_ref, x_ref[...])


def _cfm_sample(buf_t, u, x, w_dw, b_dw, ln_g, ln_b, w2):
    args = (buf_t, u, x, w_dw, b_dw, ln_g, ln_b, w2)
    return pl.pallas_call(
        _cfm_sample_kernel, grid=(1,), in_specs=[_const_spec(a.shape) for a in args],
        out_specs=_const_spec(x.shape), out_shape=jax.ShapeDtypeStruct(x.shape, F32),
        compiler_params=_cparams("arbitrary"), name="cfm_sample")(*args)


def _rows8(row):
    return jnp.broadcast_to(row, (8, row.shape[1]))


def _gdn_sample_kernel(f_ref, cw_ref, sm_ref, al_ref, dt_ref, z_ref, gn_ref, s_ref, oa_ref, so_ref):
    y = _silu(jnp.sum(f_ref[0] * cw_ref[...], axis=0, keepdims=True))
    sm = sm_ref[0]
    g = -jnp.exp(al_ref[...]) * _softplus(sm + dt_ref[...])
    beta = jax.nn.sigmoid(sm)
    row_id = _iota((8, DK_A), 0)
    outs = []
    for h in range(H_A):
        q = y[:, h * DK_A:(h + 1) * DK_A]
        k = y[:, W_QK_A + h * DK_A:W_QK_A + (h + 1) * DK_A]
        v = y[:, 2 * W_QK_A + h * DV_A:2 * W_QK_A + (h + 1) * DV_A]
        q = q * lax.rsqrt(jnp.sum(q * q, axis=1, keepdims=True) + EPS) * (DK_A ** -0.5)
        k = k * lax.rsqrt(jnp.sum(k * k, axis=1, keepdims=True) + EPS)
        eg = jnp.exp(g[:, SM_A + h:SM_A + h + 1])
        bh = beta[:, SM_B + h:SM_B + h + 1]
        s = s_ref[0, h]
        lhs = jnp.where(row_id == 0, _rows8(k), jnp.where(row_id == 1, _rows8(q), 0.0)).astype(BF16)
        rs = jnp.dot(lhs, s.astype(BF16), preferred_element_type=F32)
        v_new = bh * (v - eg * rs[0:1, :])
        o = eg * rs[1:2, :] + jnp.sum(q * k, axis=1, keepdims=True) * v_new
        so_ref[0, h] = s * eg + _col_from_row(k, DK_A) * v_new
        outs.append(o * lax.rsqrt(jnp.mean(o * o, axis=1, keepdims=True) + EPS) * gn_ref[...])
    oa_ref[0] = jnp.concatenate(outs, axis=1) * _silu(z_ref[0])


def _gdn_sample(full, conv_w, small, alog_row, dt_row, z, gnorm, state):
    db = full.shape[0]
    r3 = lambda i: (i, 0, 0)
    r4 = lambda i: (i, 0, 0, 0)
    return pl.pallas_call(
        _gdn_sample_kernel, grid=(db,),
        in_specs=[pl.BlockSpec((1,) + full.shape[1:], r3), _const_spec(conv_w.shape),
                  pl.BlockSpec((1, 1, LANES), r3), _const_spec(alog_row.shape), _const_spec(dt_row.shape),
                  pl.BlockSpec((1, 1, W_V_A), r3), _const_spec(gnorm.shape),
                  pl.BlockSpec((1,) + state.shape[1:], r4)],
        out_specs=[pl.BlockSpec((1, 1, W_V_A), r3), pl.BlockSpec((1,) + state.shape[1:], r4)],
        out_shape=[jax.ShapeDtypeStruct((db, 1, W_V_A), F32), jax.ShapeDtypeStruct(state.shape, F32)],
        compiler_params=_cparams("parallel"), name="gdn_sample")(
            full, conv_w, small, alog_row, dt_row, z, gnorm, state)


def _xattn_sample_kernel(q_ref, mk_ref, mv_ref, o_ref):
    n_rows = mk_ref.shape[0]
    n_mem = n_rows // 8
    dh = 2 * LANES
    q8 = q_ref[0] * (dh ** -0.5)
    red = jnp.sum(mk_ref[...].reshape(n_mem, 8, LANES) * q8[None], axis=-1, keepdims=True)
    own_lane = _iota((LANES, 8, LANES), 0) == _iota((LANES, 8, LANES), 2)
    n_t = n_mem // LANES
    tiles = [jnp.sum(jnp.where(own_lane, red[j * LANES:(j + 1) * LANES], 0.0), axis=0) for j in range(n_t)]
    s8 = jnp.concatenate(tiles, axis=1)
    s = s8[0:H_X] + s8[H_X:2 * H_X]
    p = jnp.exp(s - jnp.max(s, axis=-1, keepdims=True))
    inv_l = 1.0 / jnp.sum(p, axis=-1, keepdims=True)
    p8 = jnp.concatenate([p, p], axis=0)
    ones = jnp.ones((LANES, LANES), BF16)
    acc = None
    for j in range(n_t):
        z = jnp.where(own_lane, p8[None, :, j * LANES:(j + 1) * LANES], 0.0).reshape(LANES * 8, LANES)
        p3 = jnp.dot(z.astype(BF16), ones, preferred_element_type=F32).reshape(LANES, 8, LANES)
        v3 = mv_ref[j * LANES * 8:(j + 1) * LANES * 8, :].reshape(LANES, 8, LANES)
        part = jnp.sum(p3 * v3, axis=0)
        acc = part if acc is None else acc + part
    o_ref[0] = acc * jnp.concatenate([inv_l, inv_l], axis=0)


def _mem_rows(cache):
    dp, db, nm = cache.shape[:3]
    x = cache.reshape(dp, db, nm, H_X, 2, LANES)
    return jnp.transpose(x, (0, 1, 2, 4, 3, 5)).reshape(dp, db, nm * 8, LANES)


def _xattn_sample(q, mk_rows, mv_rows, layer):
    db, d = q.shape
    n_rows = mk_rows.shape[2]
    q8 = jnp.transpose(q.reshape(db, H_X, 2, LANES), (0, 2, 1, 3)).reshape(db, 8, LANES)
    r3 = lambda i: (i, 0, 0)
    mem = lambda i: (layer, i, 0, 0)
    o8 = pl.pallas_call(
        _xattn_sample_kernel, grid=(db,),
        in_specs=[pl.BlockSpec((1, 8, LANES), r3), pl.BlockSpec((None, None, n_rows, LANES), mem),
                  pl.BlockSpec((None, None, n_rows, LANES), mem)],
        out_specs=pl.BlockSpec((1, 8, LANES), r3), out_shape=jax.ShapeDtypeStruct((db, 8, LANES), F32),
        compiler_params=_cparams("parallel"), name="xattn_sample")(q8, mk_rows, mv_rows)
    return jnp.transpose(o8.reshape(db, 2, H_X, LANES), (0, 2, 1, 3)).reshape(db, d)


def _head_rows(h):
    return slice(h * DH_B, (h + 1) * DH_B)


def _fox_sample_kernel(pt_ref, q_ref, kn_ref, vn_ref, f_ref, bf_ref, *refs, g_pages):
    k_refs = refs[0:g_pages]
    v_refs = refs[g_pages:2 * g_pages]
    lf_refs = refs[2 * g_pages:3 * g_pages]
    o_ref, lfn_ref, qb_ref, m_ref, l_ref, acc_ref, carry_ref = refs[3 * g_pages:]
    del pt_ref
    w = H_B * DH_B
    gi = pl.program_id(1)
    bcast = lambda col: jnp.broadcast_to(col, (H_B, LANES))
    head_id = _iota((H_B, LANES), 0)

    def rows_to_tile(rows):
        n = rows[0].shape[1]
        out = jnp.broadcast_to(rows[0], (H_B, n))
        for h in range(1, H_B):
            out = jnp.where(head_id[:, 0:n] == h, rows[h], out)
        return out

    @pl.when(gi == 0)
    def _():
        q = q_ref[0] * (DH_B ** -0.5)
        qb_ref[...] = jnp.broadcast_to(_col_from_row(q, w), (w, LANES))
        lfn = _log_sigmoid(f_ref[0] + bf_ref[...])
        lfn_ref[0] = lfn
        qk = q * kn_ref[0]
        s_new = rows_to_tile([jnp.sum(qk[:, _head_rows(h)], axis=1, keepdims=True) for h in range(H_B)])
        m_ref[...] = bcast(s_new)
        l_ref[...] = jnp.ones_like(l_ref)
        acc_ref[...] = jnp.where(_iota((w, LANES), 1) == 0, _col_from_row(vn_ref[0], w), 0.0)
        carry_ref[...] = bcast(lfn)

    later = (_iota((PAGE, PAGE), 0) > _iota((PAGE, PAGE), 1)).astype(BF16)
    m = m_ref[:, 0:1]
    l = l_ref[:, 0:1]
    carry = carry_ref[:, 0:1]
    logits = [None] * g_pages
    top = None
    for jj in reversed(range(g_pages)):
        s_t = rows_to_tile([jnp.sum(k_refs[jj][h] * qb_ref[_head_rows(h), :], axis=0, keepdims=True)
                            for h in range(H_B)])
        lf_t = lf_refs[jj][...]
        logits[jj] = s_t + _dot3_r(lf_t, later) + carry
        carry = carry + jnp.sum(lf_t, axis=1, keepdims=True)
        top = logits[jj] if top is None else jnp.maximum(top, logits[jj])
    m_new = jnp.maximum(m, jnp.max(top, axis=1, keepdims=True))
    alpha = jnp.exp(m - m_new)
    probs = [jnp.exp(lg - m_new) for lg in logits]
    p_sum = probs[0]
    for p_t in probs[1:]:
        p_sum = p_sum + p_t
    for h in range(H_B):
        acc_h = acc_ref[_head_rows(h), :] * alpha[h:h + 1, :]
        for jj in range(g_pages):
            acc_h = acc_h + v_refs[jj][h] * probs[jj][h:h + 1, :]
        acc_ref[_head_rows(h), :] = acc_h
    l = alpha * l + jnp.sum(p_sum, axis=1, keepdims=True)
    m_ref[...] = bcast(m_new)
    l_ref[...] = bcast(l)
    carry_ref[...] = bcast(carry)

    @pl.when(gi == pl.num_programs(1) - 1)
    def _():
        inv = 1.0 / l
        den = jnp.concatenate([jnp.broadcast_to(inv[h:h + 1, :], (DH_B, 1)) for h in range(H_B)], axis=0)
        col = jnp.sum(acc_ref[...], axis=1, keepdims=True) * den
        eye = _iota((w, w), 0) == _iota((w, w), 1)
        o_ref[0] = jnp.sum(jnp.where(eye, jnp.broadcast_to(col, (w, w)), 0.0), axis=0, keepdims=True)


def _fox_sample(q, k_new, v_new, f_col, bf_col, cache_kt, cache_vt, cache_lft, page_table, g_pages=16):
    db = q.shape[0]
    w = H_B * DH_B
    n_pages = page_table.shape[1]
    g_pages = min(g_pages, n_pages)
    ng = n_pages // g_pages
    r3 = lambda i, g, pt: (i, 0, 0)

    def page_map(nd):
        return [(lambda i, g, pt, jj=jj: (pt[i * n_pages + (ng - 1 - g) * g_pages + jj],) + (0,) * nd)
                for jj in range(g_pages)]

    in_specs = [pl.BlockSpec((1, 1, w), r3)] * 3 + [pl.BlockSpec((1, H_B, 1), r3),
                                                    pl.BlockSpec(bf_col.shape, lambda i, g, pt: (0, 0))]
    in_specs += [pl.BlockSpec((None, H_B, DH_B, PAGE), mp) for mp in page_map(3)] * 2
    in_specs += [pl.BlockSpec((None, H_B, PAGE), mp) for mp in page_map(2)]
    grid_spec = pltpu.PrefetchScalarGridSpec(
        num_scalar_prefetch=1, grid=(db, ng), in_specs=in_specs,
        out_specs=[pl.BlockSpec((1, 1, w), r3), pl.BlockSpec((1, H_B, 1), r3)],
        scratch_shapes=[pltpu.VMEM((w, LANES), F32), pltpu.VMEM((H_B, LANES), F32), pltpu.VMEM((H_B, LANES), F32),
                        pltpu.VMEM((w, LANES), F32), pltpu.VMEM((H_B, LANES), F32)])
    kern = functools.partial(_fox_sample_kernel, g_pages=g_pages)
    return pl.pallas_call(
        kern, grid_spec=grid_spec,
        out_shape=[jax.ShapeDtypeStruct((db, 1, w), F32), jax.ShapeDtypeStruct((db, H_B, 1), F32)],
        compiler_params=_cparams("parallel", "arbitrary"), name="fox_sample")(
            page_table.reshape(-1), q, k_new, v_new, f_col, bf_col,
            *([cache_kt] * g_pages), *([cache_vt] * g_pages), *([cache_lft] * g_pages))


def _row(v):
    return v.reshape(1, -1).astype(F32)


def _pad_lanes(v, start):
    return jnp.zeros((1, LANES), F32).at[0, start:start + v.shape[0]].set(v)


def _chunk_rows(cols, b, l):
    n = l // CHUNK_A
    return cols.reshape(b, n, CHUNK_A, H_A).transpose(0, 1, 3, 2).reshape(b, n, H_A * CHUNK_A)


def kernel(x_prompt, x_sample, mem_prompt, cache_fox_k, cache_fox_v, cache_fox_logf, page_table, state_gdn, state_gdn_conv, state_cfm_conv, cache_mem_k, cache_mem_v, norm_mix, w_in_e, conv_a, a_log, dt_bias, gnorm_a, b_f, w_out_e, w_pw1, b_pw1, w_dw, b_dw, ln_g, ln_b, w_pw2, norm_mem, norm_x, w_xq, w_xkv, w_xo, norm_f, w_up, w_down, norm_out):
    b, l, d = x_prompt.shape
    db = x_sample.shape[0]
    n_mem = mem_prompt.shape[1]
    depth = norm_mix.shape[0]
    dh_x = d // H_X
    bf = lambda w: w.astype(BF16)

    w_in = w_in_e[0]
    off_aa = W_CONV_A + W_V_A
    off_qb = off_aa + 2 * H_A
    off_fb = off_qb + 3 * W_B
    w_small = jnp.concatenate([w_in[:, off_aa:off_qb], w_in[:, off_fb:]], axis=1)
    w_small = jnp.pad(w_small, ((0, 0), (0, LANES - w_small.shape[1])))
    w_in_r = bf(jnp.concatenate([w_in[:, :off_aa], w_in[:, off_qb:off_fb], w_small], axis=1))
    in_widths = (W_CONV_A, W_V_A, W_B, W_B, W_B, LANES)
    off_kb = off_qb + W_B
    off_vb = off_kb + W_B
    w_in_p = bf(jnp.concatenate([w_in[:, :off_aa], w_in[:, off_qb:off_kb], w_in[:, off_vb:off_fb], w_small], axis=1))
    in_widths_p = (W_CONV_A, W_V_A, W_B, W_B, LANES)
    w_in_t = [bf(w_in[:, off_kb:off_vb].T), bf(w_in[:, off_vb:off_fb].T), bf(w_small.T)]
    w_out_a = bf(w_out_e[0][:W_V_A])
    w_out_b = bf(w_out_e[0][W_V_A:])
    w_xq_b = [bf(w_xq[i]) for i in range(depth)]
    w_xo_b = [bf(w_xo[i]) for i in range(depth)]
    w_up_b = [bf(w_up[i]) for i in range(depth)]
    w_down_b = [bf(w_down[i]) for i in range(depth)]
    w_pw1_b = bf(w_pw1[0])
    w_pw2_b = bf(w_pw2[0])
    alog_rows = _row(jnp.repeat(a_log[0], CHUNK_A))
    dt_rows = _row(jnp.repeat(dt_bias[0], CHUNK_A))
    gn = _row(gnorm_a[0])

    memf = mem_prompt.reshape(b * n_mem, d)
    mem_k, mem_v = [], []
    for i in range(depth):
        mk, mv = _linear([memf], [bf(w_xkv[i])], gain=_row(norm_mem[i]), out_widths=(d, d), name="mem_kv")
        mem_k.append(mk.reshape(b, n_mem, d))
        mem_v.append(mv.reshape(b, n_mem, d))
    mem_k_prompt = jnp.stack(mem_k).reshape(depth, b, n_mem, H_X, dh_x)
    mem_v_prompt = jnp.stack(mem_v).reshape(depth, b, n_mem, H_X, dh_x)

    xp = x_prompt.reshape(b * l, d)
    conv_in, z, q, v, small, k_t, v_t, small_t = _linear(
        [xp], [w_in_p], gain=_row(norm_mix[0]), out_widths=in_widths_p, wts=w_in_t, t_batch=(b, l), name="in_proj")
    fox_k_prompt = jnp.transpose(k_t.reshape(b, H_B, DH_B, l), (0, 3, 1, 2))[None]
    fox_v_prompt = jnp.transpose(v_t.reshape(b, H_B, DH_B, l), (0, 3, 1, 2))[None]
    conv_in3 = conv_in.reshape(b, l, W_CONV_A)
    gdn_conv_prompt = conv_in3[:, l - (CONV_A - 1):, :][None]
    oa, s_fin = _gdn_prompt(conv_in3, z.reshape(b, l, W_V_A),
                            _chunk_rows(small[:, SM_A:SM_A + H_A], b, l),
                            _chunk_rows(small[:, SM_B:SM_B + H_A], b, l),
                            conv_a[0], alog_rows, dt_rows, gn)
    gdn_state_prompt = s_fin.reshape(b, DK_A, H_A, DV_A).transpose(0, 2, 1, 3)[None]
    qf, kft, vf, lft = _fox_prep(q.reshape(b, l, W_B), k_t, v.reshape(b, l, W_B), small.reshape(b, l, LANES),
                                 small_t, _pad_lanes(b_f[0], SM_F), b_f[0].reshape(H_B, 1))
    fox_logf_prompt = jnp.transpose(lft, (0, 2, 1))[None]
    ob = _fox_flash(qf, kft, vf)
    x = _linear([oa.reshape(b * l, W_V_A), ob.reshape(b * l, W_B)], [w_out_a, w_out_b], res=xp, name="out_proj")
    x = _xattn_prompt(x.reshape(b, l, d), _row(norm_x[0]), w_xq_b[0], bf(mem_k[0]), bf(mem_v[0]), w_xo_b[0])
    x = _mlp(x.reshape(b * l, d), _row(norm_f[0]), w_up_b[0], w_down_b[0])
    glu = _linear([x], [w_pw1_b], gain=_row(norm_mix[1]), bias=_row(b_pw1[0]), act="glu", name="pw1_glu")
    glu3 = glu.reshape(b, l, d)
    cfm_conv_prompt = glu3[:, l - (CONV_C - 1):, :][None]
    x = _cfm_prompt(glu3, x.reshape(b, l, d), w_dw[0], _row(b_dw[0]), _row(ln_g[0]), _row(ln_b[0]), w_pw2_b)
    x = _xattn_prompt(x, _row(norm_x[1]), w_xq_b[1], bf(mem_k[1]), bf(mem_v[1]), w_xo_b[1])
    y_prompt = _mlp(x.reshape(b * l, d), _row(norm_f[1]), w_up_b[1], w_down_b[1],
                    final_gain=_row(norm_out)).reshape(b, l, d)

    xs = x_sample.reshape(db, d)
    conv_s, z_s, q_s, k_s, v_s, small_s = _linear([xs], [w_in_r], gain=_row(norm_mix[0]),
                                                  out_widths=in_widths, name="in_proj_s")
    fox_k_sample = k_s.reshape(1, db, 1, H_B, DH_B)
    fox_v_sample = v_s.reshape(1, db, 1, H_B, DH_B)
    full = jnp.concatenate([state_gdn_conv[0], conv_s[:, None, :]], axis=1)
    gdn_conv_sample = full[:, 1:, :][None]
    oa_s, s_new = _gdn_sample(full, conv_a[0], small_s.reshape(db, 1, LANES), _pad_lanes(a_log[0], SM_A),
                              _pad_lanes(dt_bias[0], SM_A), z_s.reshape(db, 1, W_V_A), gn, state_gdn[0])
    gdn_state_sample = s_new[None]
    ob_s, lf_s = _fox_sample(q_s.reshape(db, 1, W_B), k_s.reshape(db, 1, W_B), v_s.reshape(db, 1, W_B),
                             small_s[:, SM_F:SM_F + H_B].reshape(db, H_B, 1), b_f[0].reshape(H_B, 1),
                             jnp.transpose(cache_fox_k[0], (0, 2, 3, 1)), jnp.transpose(cache_fox_v[0], (0, 2, 3, 1)),
                             jnp.transpose(cache_fox_logf[0], (0, 2, 1)), page_table)
    fox_logf_sample = lf_s.reshape(1, db, 1, H_B)
    x = _linear([oa_s.reshape(db, W_V_A), ob_s.reshape(db, W_B)], [w_out_a, w_out_b], res=xs, name="out_proj_s")

    mk_rows = _mem_rows(cache_mem_k)
    mv_rows = _mem_rows(cache_mem_v)

    def xattn_s(x, i):
        qx = _linear([x], [w_xq_b[i]], gain=_row(norm_x[i]), name="xq_s")
        o = _xattn_sample(qx, mk_rows, mv_rows, i)
        return _linear([o], [w_xo_b[i]], res=x, name="xo_s")

    x = xattn_s(x, 0)
    x = _mlp(x, _row(norm_f[0]), w_up_b[0], w_down_b[0])
    glu_s = _linear([x], [w_pw1_b], gain=_row(norm_mix[1]), bias=_row(b_pw1[0]), act="glu", name="pw1_glu_s")
    cfm_conv_sample = jnp.concatenate([state_cfm_conv[0][:, 1:, :], glu_s[:, None, :]], axis=1)[None]
    x = _cfm_sample(state_cfm_conv[0].transpose(1, 0, 2), glu_s, x, w_dw[0], _row(b_dw[0]), _row(ln_g[0]),
                    _row(ln_b[0]), w_pw2_b)
    x = xattn_s(x, 1)
    y_sample = _mlp(x, _row(norm_f[1]), w_up_b[1], w_down_b[1], final_gain=_row(norm_out)).reshape(db, 1, d)

    return (y_prompt, y_sample, fox_k_prompt, fox_v_prompt, fox_logf_prompt, fox_k_sample, fox_v_sample,
            fox_logf_sample, gdn_state_prompt, gdn_conv_prompt, gdn_state_sample, gdn_conv_sample,
            cfm_conv_prompt, cfm_conv_sample, mem_k_prompt, mem_v_prompt)
```

```python
import functools
import math

import jax
import jax.numpy as jnp
import numpy as np
from jax import lax
from jax.experimental import pallas as pl
from jax.experimental.pallas import tpu as pltpu

F32 = jnp.float32
BF16 = jnp.bfloat16
EPS = 1e-6
NEG = -1e30
LOG2E = 1.4426950408889634

H_A = 4
DK_A = 128
DV_A = 128
CONV_A = 4
CHUNK_A = 64
H_B = 8
DH_B = 64
H_X = 4
CONV_C = 31
PAGE = 128

W_QK_A = H_A * DK_A
W_V_A = H_A * DV_A
W_CONV_A = 2 * W_QK_A + W_V_A
W_B = H_B * DH_B
LANES = 128
SM_A = 0
SM_B = H_A
SM_F = 2 * H_A

VMEM_LIMIT = 56 * 1024 * 1024


def _cparams(*sem):
    return pltpu.CompilerParams(dimension_semantics=sem, vmem_limit_bytes=VMEM_LIMIT)


def _const_spec(shape):
    nd = len(shape)
    return pl.BlockSpec(shape, lambda *_: (0,) * nd)


def _rms(x, g):
    return x * lax.rsqrt(jnp.mean(x * x, axis=-1, keepdims=True) + EPS) * g


def _silu(x):
    return x * jax.nn.sigmoid(x)


def _softplus(x):
    return jnp.maximum(x, 0.0) + jnp.log1p(jnp.exp(-jnp.abs(x)))


def _split3(x):
    h = x.astype(BF16)
    r = x - h.astype(F32)
    m = r.astype(BF16)
    l = (r - m.astype(F32)).astype(BF16)
    return h, m, l


def _dot3_l(mat01, x):
    h, m, l = _split3(x)
    d = lambda t: jnp.dot(mat01, t, preferred_element_type=F32)
    return d(h) + d(m) + d(l)


def _dot3_r(x, mat01):
    h, m, l = _split3(x)
    d = lambda t: jnp.dot(t, mat01, preferred_element_type=F32)
    return d(h) + d(m) + d(l)


def _dot_nt(a, b):
    return lax.dot_general(a, b, (((1,), (1,)), ((), ())), preferred_element_type=F32)


def _dot_tn(a, b):
    return lax.dot_general(a, b, (((0,), (0,)), ((), ())), preferred_element_type=F32)


def _iota(shape, dim):
    return lax.broadcasted_iota(jnp.int32, shape, dim)


def _col_from_row(row, n):
    eye = _iota((n, n), 0) == _iota((n, n), 1)
    return jnp.sum(jnp.where(eye, jnp.broadcast_to(row, (n, n)), 0.0), axis=1, keepdims=True)


def _linear_kernel(*refs, n_in, n_t, has_gain, has_bias, has_res, act, out_widths, chunk):
    it = iter(refs)
    x_refs = [next(it) for _ in range(n_in)]
    w_refs = [next(it) for _ in range(n_in)]
    wt_refs = [next(it) for _ in range(n_t)]
    gain_ref = next(it) if has_gain else None
    bias_ref = next(it) if has_bias else None
    res_ref = next(it) if has_res else None
    out_refs = [next(it) for _ in out_widths]
    outt_refs = [next(it) for _ in range(n_t)]
    xs = []
    for i, xr in enumerate(x_refs):
        x = xr[...]
        if has_gain and i == 0:
            x = _rms(x, gain_ref[...])
        xs.append(x.astype(BF16))
    n_total = sum(out_widths)

    def mm(col0, cw):
        acc = None
        for xb, wr in zip(xs, w_refs):
            d = jnp.dot(xb, wr[:, col0:col0 + cw], preferred_element_type=F32)
            acc = d if acc is None else acc + d
        if has_bias:
            acc = acc + bias_ref[:, col0:col0 + cw]
        return acc

    off = 0
    for o_ref, width in zip(out_refs, out_widths):
        for c0 in range(0, width, chunk):
            cw = min(chunk, width - c0)
            y = mm(off + c0, cw)
            if act == "glu":
                y = y * jax.nn.sigmoid(mm(n_total + off + c0, cw))
            if has_res:
                y = y + res_ref[:, off + c0:off + c0 + cw]
            o_ref[:, c0:c0 + cw] = y
        off += width
    for wt_ref, ot_ref in zip(wt_refs, outt_refs):
        ot_ref[0] = _dot_nt(wt_ref[...], xs[0])


def _linear(xs, ws, *, gain=None, bias=None, res=None, act=None, out_widths=None, wts=(), t_batch=None,
            tm=512, chunk=512, name="linear"):
    m = xs[0].shape[0]
    tm = min(tm, m)
    assert m % tm == 0
    n_mm = ws[0].shape[1]
    n_out = n_mm // 2 if act == "glu" else n_mm
    if out_widths is None:
        out_widths = (n_out,)
    assert sum(out_widths) == n_out
    row = lambda i: (i, 0)
    in_specs = [pl.BlockSpec((tm, x.shape[1]), row) for x in xs]
    in_specs += [_const_spec(w.shape) for w in ws]
    in_specs += [_const_spec(w.shape) for w in wts]
    args = list(xs) + list(ws) + list(wts)
    if gain is not None:
        in_specs.append(_const_spec(gain.shape)); args.append(gain)
    if bias is not None:
        in_specs.append(_const_spec(bias.shape)); args.append(bias)
    if res is not None:
        in_specs.append(pl.BlockSpec((tm, res.shape[1]), row)); args.append(res)
    out_shape = [jax.ShapeDtypeStruct((m, w), F32) for w in out_widths]
    out_specs = [pl.BlockSpec((tm, w), row) for w in out_widths]
    if wts:
        tb, tl = t_batch
        assert tb * tl == m and tl % tm == 0
        per = tl // tm
        out_shape += [jax.ShapeDtypeStruct((tb, w.shape[0], tl), F32) for w in wts]
        out_specs += [pl.BlockSpec((1, w.shape[0], tm), lambda i: (i // per, 0, i % per)) for w in wts]
    kern = functools.partial(_linear_kernel, n_in=len(xs), n_t=len(wts), has_gain=gain is not None,
                             has_bias=bias is not None, has_res=res is not None, act=act,
                             out_widths=tuple(out_widths), chunk=chunk)
    outs = pl.pallas_call(kern, grid=(m // tm,), in_specs=in_specs, out_specs=out_specs,
                          out_shape=out_shape, compiler_params=_cparams("parallel"), name=name)(*args)
    return outs if len(outs) > 1 else outs[0]


def _mlp_kernel(*refs, chunk, final_norm):
    if final_norm:
        x_ref, g_ref, wu_ref, wd_ref, go_ref, o_ref = refs
    else:
        x_ref, g_ref, wu_ref, wd_ref, o_ref = refs
    x = x_ref[...]
    xn = _rms(x, g_ref[...]).astype(BF16)
    acc = x
    d_ff = wu_ref.shape[1]
    for c0 in range(0, d_ff, chunk):
        h = jnp.dot(xn, wu_ref[:, c0:c0 + chunk], preferred_element_type=F32)
        h = jnp.square(jnp.maximum(h, 0.0)).astype(BF16)
        acc = acc + jnp.dot(h, wd_ref[c0:c0 + chunk, :], preferred_element_type=F32)
    if final_norm:
        acc = _rms(acc, go_ref[...])
    o_ref[...] = acc


def _mlp(x, gain, w_up, w_down, final_gain=None, tm=512, chunk=512):
    m, d = x.shape
    tm = min(tm, m)
    row = lambda i: (i, 0)
    in_specs = [pl.BlockSpec((tm, d), row), _const_spec(gain.shape), _const_spec(w_up.shape),
                _const_spec(w_down.shape)]
    args = [x, gain, w_up, w_down]
    if final_gain is not None:
        in_specs.append(_const_spec(final_gain.shape)); args.append(final_gain)
    kern = functools.partial(_mlp_kernel, chunk=chunk, final_norm=final_gain is not None)
    return pl.pallas_call(kern, grid=(m // tm,), in_specs=in_specs,
                          out_specs=pl.BlockSpec((tm, d), row),
                          out_shape=jax.ShapeDtypeStruct((m, d), F32),
                          compiler_params=_cparams("parallel"), name="mlp")(*args)


def _xattn_kernel(x_ref, g_ref, wq_ref, mk_ref, mv_ref, wo_ref, o_ref):
    x = x_ref[0]
    d = x.shape[1]
    dh = d // H_X
    xn = _rms(x, g_ref[...]).astype(BF16)
    q = jnp.dot(xn, wq_ref[...], preferred_element_type=F32) * (dh ** -0.5)
    q = q.astype(BF16)
    head = lambda h: slice(h * dh, (h + 1) * dh)
    scores = lambda h: _dot_nt(q[:, head(h)], mk_ref[0, :, head(h)])
    outs = []
    s_next = scores(0)
    for h in range(H_X):
        sl = head(h)
        s = s_next
        if h + 1 < H_X:
            s_next = scores(h + 1)
        p = jnp.exp(s - jnp.max(s, axis=-1, keepdims=True))
        l = jnp.sum(p, axis=-1, keepdims=True)
        o = jnp.dot(p.astype(BF16), mv_ref[0, :, sl], preferred_element_type=F32)
        outs.append((o * (1.0 / l)).astype(BF16))
    o = jnp.concatenate(outs, axis=1)
    o_ref[0] = x + jnp.dot(o, wo_ref[...], preferred_element_type=F32)


def _xattn_prompt(x, gain, wq, mk, mv, wo, tm=512):
    b, l, d = x.shape
    tm = min(tm, l)
    nm = mk.shape[1]
    blk = lambda bi, i: (bi, i, 0)
    mem = lambda bi, i: (bi, 0, 0)
    return pl.pallas_call(
        _xattn_kernel, grid=(b, l // tm),
        in_specs=[pl.BlockSpec((1, tm, d), blk), _const_spec(gain.shape), _const_spec(wq.shape),
                  pl.BlockSpec((1, nm, d), mem), pl.BlockSpec((1, nm, d), mem), _const_spec(wo.shape)],
        out_specs=pl.BlockSpec((1, tm, d), blk),
        out_shape=jax.ShapeDtypeStruct((b, l, d), F32),
        compiler_params=_cparams("parallel", "parallel"), name="xattn_prompt")(x, gain, wq, mk, mv, wo)


def _log_sigmoid(x):
    return jnp.minimum(x, 0.0) - jnp.log1p(jnp.exp(-jnp.abs(x)))


def _head_lanes(x, h):
    p = h // 2
    blk = x[:, p * LANES:(p + 1) * LANES]
    if h % 2:
        blk = pltpu.roll(blk, DH_B, 1)
    return blk


def _fox_prep_kernel(q_ref, kt_ref, v_ref, sm_ref, smt_ref, bf_ref, bfc_ref, qf_ref, kft_ref, vf_ref, lft_ref,
                     carry_ref, carryt_ref):
    t = q_ref.shape[1]

    @pl.when(pl.program_id(1) == 0)
    def _():
        carry_ref[...] = jnp.zeros_like(carry_ref)
        carryt_ref[...] = jnp.zeros_like(carryt_ref)

    lane = _iota((t, LANES), 1)
    gate_lane = (lane >= SM_F) & (lane < SM_F + H_B)
    logf = jnp.where(gate_lane, _log_sigmoid(sm_ref[0] + bf_ref[...]), 0.0)
    before = _iota((t, t), 0) >= _iota((t, t), 1)
    c = _dot3_l(before.astype(BF16), logf) + carry_ref[0:1, :]
    carry_ref[0:1, :] = c[t - 1:t, :]
    c1, c2, c3 = _split3(c * LOG2E)
    c1, c2, c3 = c1.astype(F32), c2.astype(F32), c3.astype(F32)
    lft = _log_sigmoid(smt_ref[0] + bfc_ref[...])
    lft_ref[0] = lft
    ct = _dot3_r(lft, (_iota((t, t), 0) <= _iota((t, t), 1)).astype(BF16)) + carryt_ref[:, 0:1]
    carryt_ref[...] = jnp.broadcast_to(ct[:, t - 1:t], carryt_ref.shape)
    r1, r2, r3 = _split3(ct * LOG2E)
    r1, r2, r3 = r1.astype(F32), r2.astype(F32), r3.astype(F32)
    q = q_ref[0] * (DH_B ** -0.5 * LOG2E)
    v = v_ref[0]
    low = lane < DH_B
    row = _iota((DH_B, t), 0)
    for h in range(H_B):
        col = slice(SM_F + h, SM_F + h + 1)
        b1, b2, b3 = c1[:, col], c2[:, col], c3[:, col]
        qx = jnp.where(lane == DH_B, b1, jnp.where(lane == DH_B + 1, b2, jnp.where(
            lane == DH_B + 2, b3, jnp.where(lane < DH_B + 6, 1.0, 0.0))))
        kx = jnp.where(row < 3, 1.0, jnp.where(row == 3, -r1[h:h + 1, :], jnp.where(
            row == 4, -r2[h:h + 1, :], jnp.where(row == 5, -r3[h:h + 1, :], 0.0))))
        vx = jnp.where(lane == DH_B, 1.0, 0.0)
        qf_ref[0, h] = jnp.where(low, _head_lanes(q, h), qx).astype(BF16)
        kft_ref[0, h] = jnp.concatenate([kt_ref[0, h * DH_B:(h + 1) * DH_B, :], kx], axis=0).astype(BF16)
        vf_ref[0, h] = jnp.where(low, _head_lanes(v, h), vx).astype(BF16)


def _fox_prep(q, kt, v, small, small_t, bf_row, bf_col, t=256):
    b, l, w = q.shape
    t = min(t, l)
    blk = lambda bi, i: (bi, i, 0)
    tblk = lambda bi, i: (bi, 0, i)
    hblk = lambda bi, i: (bi, 0, i, 0)
    hshape = jax.ShapeDtypeStruct((b, H_B, l, LANES), BF16)
    return pl.pallas_call(
        _fox_prep_kernel, grid=(b, l // t),
        in_specs=[pl.BlockSpec((1, t, w), blk), pl.BlockSpec((1, w, t), tblk), pl.BlockSpec((1, t, w), blk),
                  pl.BlockSpec((1, t, LANES), blk),
                  pl.BlockSpec((1, H_B, t), lambda bi, i: (bi, SM_F // H_B, i)),
                  _const_spec(bf_row.shape), _const_spec(bf_col.shape)],
        out_specs=[pl.BlockSpec((1, H_B, t, LANES), hblk),
                   pl.BlockSpec((1, H_B, LANES, t), lambda bi, i: (bi, 0, 0, i)),
                   pl.BlockSpec((1, H_B, t, LANES), hblk), pl.BlockSpec((1, H_B, t), tblk)],
        out_shape=[hshape, jax.ShapeDtypeStruct((b, H_B, LANES, l), BF16), hshape,
                   jax.ShapeDtypeStruct((b, H_B, l), F32)],
        scratch_shapes=[pltpu.VMEM((8, LANES), F32), pltpu.VMEM((H_B, LANES), F32)],
        compiler_params=_cparams("parallel", "arbitrary"), name="fox_prep")(
            q, kt, v, small, small_t, bf_row, bf_col)


LOOKAHEAD = 2


def _fox_flash_kernel(it_ref, jt_ref, fl_ref, qf_ref, kft_ref, vf_ref, o_ref, m_ref, acc_ref):
    p_id = pl.program_id(1)
    i = it_ref[p_id]
    j = jt_ref[p_id]
    flags = fl_ref[p_id]
    tq = qf_ref.shape[2]
    tk = kft_ref.shape[3]

    @pl.when(j == 0)
    def _():
        m_ref[...] = jnp.full_like(m_ref, NEG)
        acc_ref[...] = jnp.zeros_like(acc_ref)

    def heads(masked):
        if masked:
            visible = _iota((tq, tk), 1) + j * tk <= _iota((tq, tk), 0) + i * tq

        def scores(h):
            s = jnp.dot(qf_ref[0, h], kft_ref[0, h], preferred_element_type=F32)
            return jnp.where(visible, s, NEG) if masked else s

        pending = [scores(h) for h in range(LOOKAHEAD)]
        for h in range(H_B):
            s = pending.pop(0)
            if h + LOOKAHEAD < H_B:
                pending.append(scores(h + LOOKAHEAD))
            m_prev = m_ref[h]
            m_new = jnp.maximum(m_prev, jnp.max(s, axis=-1, keepdims=True))
            alpha = jnp.exp2(m_prev - m_new)
            p = jnp.exp2(s - jnp.concatenate([m_new] * (tk // LANES), axis=1)).astype(BF16)
            acc_ref[h] = alpha * acc_ref[h] + jnp.dot(p, vf_ref[0, h], preferred_element_type=F32)
            m_ref[h] = m_new

    @pl.when((flags & 1) == 0)
    def _():
        heads(False)

    @pl.when((flags & 1) == 1)
    def _():
        heads(True)

    @pl.when((flags & 2) == 2)
    def _():
        lane = _iota((tq, LANES), 1)
        for pr in range(H_B // 2):
            a0 = acc_ref[2 * pr]
            a1 = acc_ref[2 * pr + 1]
            o0 = a0 * (1.0 / a0[:, DH_B:DH_B + 1])
            o1 = a1 * (1.0 / a1[:, DH_B:DH_B + 1])
            o_ref[0, :, pr * LANES:(pr + 1) * LANES] = jnp.where(lane < DH_B, o0, pltpu.roll(o1, DH_B, 1))


def _fox_flash(qf, kft, vf, tq=1024, tk=512):
    b, hh, l, _ = qf.shape
    tq = min(tq, l)
    tk = min(tk, l)
    its, jts, fls = [], [], []
    for i in range(l // tq):
        j_last = ((i + 1) * tq - 1) // tk
        for j in range(j_last + 1):
            crosses = (j + 1) * tk - 1 > i * tq
            its.append(i)
            jts.append(j)
            fls.append(int(crosses) + 2 * int(j == j_last))
    it, jt, fl = (jnp.asarray(np.array(a, np.int32)) for a in (its, jts, fls))
    qmap = lambda bi, p, it, jt, fl: (bi, 0, it[p], 0)
    kmap = lambda bi, p, it, jt, fl: (bi, 0, jt[p], 0)
    ktmap = lambda bi, p, it, jt, fl: (bi, 0, 0, jt[p])
    omap = lambda bi, p, it, jt, fl: (bi, it[p], 0)
    grid_spec = pltpu.PrefetchScalarGridSpec(
        num_scalar_prefetch=3, grid=(b, len(its)),
        in_specs=[pl.BlockSpec((1, hh, tq, LANES), qmap), pl.BlockSpec((1, hh, LANES, tk), ktmap),
                  pl.BlockSpec((1, hh, tk, LANES), kmap)],
        out_specs=pl.BlockSpec((1, tq, hh * DH_B), omap),
        scratch_shapes=[pltpu.VMEM((hh, tq, LANES), F32), pltpu.VMEM((hh, tq, LANES), F32)])
    return pl.pallas_call(
        _fox_flash_kernel, grid_spec=grid_spec,
        out_shape=jax.ShapeDtypeStruct((b, l, hh * DH_B), F32),
        compiler_params=_cparams("parallel", "arbitrary"), name="fox_flash")(it, jt, fl, qf, kft, vf)


def _gdn_kernel(x_ref, z_ref, a_ref, b_ref, cw_ref, al_ref, dt_ref, gn_ref, oa_ref, s_ref,
                xs_ref, st_ref, *, nc):
    c = CHUNK_A
    r = H_A * c
    t = nc * c
    halo = 8

    @pl.when(pl.program_id(1) == 0)
    def _():
        xs_ref[0:halo, :] = jnp.zeros((halo, xs_ref.shape[1]), F32)
        st_ref[...] = jnp.zeros_like(st_ref)

    @pl.when(pl.program_id(1) != 0)
    def _():
        xs_ref[0:halo, :] = xs_ref[t:t + halo, :]

    xs_ref[halo:halo + t, :] = x_ref[0]
    y = None
    for jj in range(CONV_A):
        o = halo - (CONV_A - 1) + jj
        term = xs_ref[o:o + t, :] * cw_ref[jj:jj + 1, :]
        y = term if y is None else y + term
    y = _silu(y)

    g_rows = -jnp.exp(al_ref[...]) * _softplus(a_ref[0] + dt_ref[...])
    beta_rows = jax.nn.sigmoid(b_ref[0])
    ri = _iota((r, r), 0)
    ci = _iota((r, r), 1)
    same = (ri // c) == (ci // c)
    incl = same & (ri >= ci)
    strict = same & (ri > ci)
    gc_rows = _dot3_r(g_rows, (same & (ri <= ci)).astype(BF16))
    eye = ri == ci
    last = ((ci % c) == (c - 1)) & same

    def stack(a):
        return jnp.concatenate([a[:, h * DK_A:(h + 1) * DK_A] for h in range(H_A)], axis=0)

    pre = []
    for n in range(nc):
        rows = slice(n * c, (n + 1) * c)
        gc_r = gc_rows[n:n + 1, :]
        gcb = jnp.broadcast_to(gc_r, (r, r))
        gc_c = jnp.sum(jnp.where(eye, gcb, 0.0), axis=1, keepdims=True)
        gl_c = jnp.sum(jnp.where(last, gcb, 0.0), axis=1, keepdims=True)
        beta_c = jnp.sum(jnp.where(eye, jnp.broadcast_to(beta_rows[n:n + 1, :], (r, r)), 0.0),
                         axis=1, keepdims=True)
        q = stack(y[rows, 0:W_QK_A])
        k = stack(y[rows, W_QK_A:2 * W_QK_A])
        v = stack(y[rows, 2 * W_QK_A:W_CONV_A])
        q = q * lax.rsqrt(jnp.sum(q * q, axis=1, keepdims=True) + EPS) * (DK_A ** -0.5)
        k = k * lax.rsqrt(jnp.sum(k * k, axis=1, keepdims=True) + EPS)
        decay = jnp.exp(jnp.where(incl, gc_c - gc_r, -jnp.inf))
        kb = k * beta_c
        k16 = k.astype(BF16)
        eg = jnp.exp(gc_c)
        pre.append(dict(decay=decay, kk=_dot_nt(kb.astype(BF16), k16), qk=_dot_nt(q.astype(BF16), k16),
                        rhs=jnp.concatenate([v * beta_c, kb * eg], axis=1).astype(BF16),
                        qg=(q * eg).astype(BF16), kdec=(k * jnp.exp(gl_c - gc_c)).astype(BF16),
                        dec=[jnp.exp(gl_c[h * c:h * c + 1, :]) for h in range(H_A)]))
    for pc in pre:
        pc["a_qk"] = jnp.where(incl, pc["qk"] * pc["decay"], 0.0).astype(BF16)
        pc["mneg"] = jnp.where(strict, -(pc["kk"] * pc["decay"]), 0.0)
        pc["tinv"] = jnp.where(eye, 1.0, 0.0) + pc["mneg"]
    for _ in range(int(math.log2(c)) - 1):
        for pc in pre:
            m16 = pc["mneg"].astype(BF16)
            pc["mneg"] = jnp.dot(m16, m16, preferred_element_type=F32)
        for pc in pre:
            pc["tinv"] = pc["tinv"] + jnp.dot(pc["tinv"].astype(BF16), pc["mneg"].astype(BF16),
                                              preferred_element_type=F32)
    for pc in pre:
        uw = jnp.dot(pc["tinv"].astype(BF16), pc["rhs"], preferred_element_type=F32)
        pc["u"] = uw[:, 0:DV_A]
        pc["w"] = uw[:, DV_A:2 * DV_A].astype(BF16)

    s_heads = [st_ref[:, h * DV_A:(h + 1) * DV_A] for h in range(H_A)]
    for n in range(nc):
        rows = slice(n * c, (n + 1) * c)
        pc = pre[n]
        v_parts, qs_parts = [], []
        for h in range(H_A):
            hr = slice(h * c, (h + 1) * c)
            wq = jnp.concatenate([pc["w"][hr], pc["qg"][hr]], axis=0)
            wq_s = jnp.dot(wq, s_heads[h].astype(BF16), preferred_element_type=F32)
            v_parts.append(pc["u"][hr] - wq_s[0:c])
            qs_parts.append(wq_s[c:2 * c])
        v_new = jnp.concatenate(v_parts, axis=0).astype(BF16)
        o = jnp.concatenate(qs_parts, axis=0) + jnp.dot(pc["a_qk"], v_new, preferred_element_type=F32)
        for h in range(H_A):
            hr = slice(h * c, (h + 1) * c)
            s_heads[h] = s_heads[h] * pc["dec"][h] + _dot_tn(pc["kdec"][hr], v_new[hr])
        on = o * lax.rsqrt(jnp.mean(o * o, axis=1, keepdims=True) + EPS) * gn_ref[...]
        on = jnp.concatenate([on[h * c:(h + 1) * c, :] for h in range(H_A)], axis=1)
        oa_ref[0, rows, :] = on * _silu(z_ref[0, rows, :])
    for h in range(H_A):
        st_ref[:, h * DV_A:(h + 1) * DV_A] = s_heads[h]

    @pl.when(pl.program_id(1) == pl.num_programs(1) - 1)
    def _():
        s_ref[0] = st_ref[...]


def _gdn_prompt(conv_in, z, a_rows, b_rows, conv_w, alog_row, dt_row, gnorm, nc=8):
    b, l, wc = conv_in.shape
    n = l // CHUNK_A
    nc = min(nc, n)
    t = nc * CHUNK_A
    r = H_A * CHUNK_A
    blk = lambda bi, i: (bi, i, 0)
    kern = functools.partial(_gdn_kernel, nc=nc)
    oa, s = pl.pallas_call(
        kern, grid=(b, n // nc),
        in_specs=[pl.BlockSpec((1, t, wc), blk), pl.BlockSpec((1, t, W_V_A), blk),
                  pl.BlockSpec((1, nc, r), blk), pl.BlockSpec((1, nc, r), blk),
                  _const_spec(conv_w.shape), _const_spec(alog_row.shape), _const_spec(dt_row.shape),
                  _const_spec(gnorm.shape)],
        out_specs=[pl.BlockSpec((1, t, W_V_A), blk),
                   pl.BlockSpec((1, DK_A, H_A * DV_A), lambda bi, i: (bi, 0, 0))],
        out_shape=[jax.ShapeDtypeStruct((b, l, W_V_A), F32),
                   jax.ShapeDtypeStruct((b, DK_A, H_A * DV_A), F32)],
        scratch_shapes=[pltpu.VMEM((t + 8, wc), F32), pltpu.VMEM((DK_A, H_A * DV_A), F32)],
        compiler_params=_cparams("parallel", "arbitrary"), name="gdn_prompt")(
            conv_in, z, a_rows, b_rows, conv_w, alog_row, dt_row, gnorm)
    return oa, s


def _cfm_tail(y, bd_ref, lg_ref, lb_ref, w2_ref, res):
    y = y + bd_ref[...]
    yc = y - jnp.mean(y, axis=-1, keepdims=True)
    yn = yc * lax.rsqrt(jnp.mean(yc * yc, axis=-1, keepdims=True) + EPS) * lg_ref[...] + lb_ref[...]
    return res + jnp.dot(_silu(yn).astype(BF16), w2_ref[...], preferred_element_type=F32)


def _cfm_kernel(u_ref, x_ref, wd_ref, bd_ref, lg_ref, lb_ref, w2_ref, o_ref, xs_ref, sh_ref):
    t = u_ref.shape[1]
    halo = 32

    @pl.when(pl.program_id(1) == 0)
    def _():
        xs_ref[0:halo, :] = jnp.zeros((halo, xs_ref.shape[1]), F32)

    @pl.when(pl.program_id(1) != 0)
    def _():
        xs_ref[0:halo, :] = xs_ref[t:t + halo, :]

    xs_ref[halo:halo + t, :] = u_ref[0]
    y = None
    for r in range(8):
        offs = [o for o in range(halo - (CONV_C - 1), halo + 1) if o % 8 == r]
        if not offs:
            continue
        if r:
            n_rows = max(offs) - r + t
            sh_ref[0:n_rows, :] = xs_ref[r:r + n_rows, :]
        win_ref = sh_ref if r else xs_ref
        for o in offs:
            jj = o - (halo - (CONV_C - 1))
            term = win_ref[o - r:o - r + t, :] * wd_ref[jj:jj + 1, :]
            y = term if y is None else y + term
    o_ref[0] = _cfm_tail(y, bd_ref, lg_ref, lb_ref, w2_ref, x_ref[0])


def _cfm_prompt(u, x, w_dw, b_dw, ln_g, ln_b, w2, t=256):
    b, l, d = u.shape
    t = min(t, l)
    blk = lambda bi, i: (bi, i, 0)
    return pl.pallas_call(
        _cfm_kernel, grid=(b, l // t),
        in_specs=[pl.BlockSpec((1, t, d), blk), pl.BlockSpec((1, t, d), blk), _const_spec(w_dw.shape),
                  _const_spec(b_dw.shape), _const_spec(ln_g.shape), _const_spec(ln_b.shape),
                  _const_spec(w2.shape)],
        out_specs=pl.BlockSpec((1, t, d), blk),
        out_shape=jax.ShapeDtypeStruct((b, l, d), F32),
        scratch_shapes=[pltpu.VMEM((t + 32, d), F32), pltpu.VMEM((t + 32, d), F32)],
        compiler_params=_cparams("parallel", "arbitrary"), name="cfm_prompt")(
            u, x, w_dw, b_dw, ln_g, ln_b, w2)


def _cfm_sample_kernel(buf_ref, u_ref, x_ref, wd_ref, bd_ref, lg_ref, lb_ref, w2_ref, o_ref):
    y = u_ref[...] * wd_ref[CONV_C - 1:CONV_C, :]
    for jj in range(CONV_C - 1):
        y = y + buf_ref[jj] * wd_ref[jj:jj + 1, :]
    o_ref[...] = _cfm_tail(y, bd_ref, lg_ref, lb_ref, w2_ref, x_ref[...])


def _cfm_sample(buf_t, u, x, w_dw, b_dw, ln_g, ln_b, w2):
    args = (buf_t, u, x, w_dw, b_dw, ln_g, ln_b, w2)
    return pl.pallas_call(
        _cfm_sample_kernel, grid=(1,), in_specs=[_const_spec(a.shape) for a in args],
        out_specs=_const_spec(x.shape), out_shape=jax.ShapeDtypeStruct(x.shape, F32),
        compiler_params=_cparams("arbitrary"), name="cfm_sample")(*args)


def _rows8(row):
    return jnp.broadcast_to(row, (8, row.shape[1]))


def _gdn_sample_kernel(f_ref, cw_ref, sm_ref, al_ref, dt_ref, z_ref, gn_ref, s_ref, oa_ref, so_ref):
    y = _silu(jnp.sum(f_ref[0] * cw_ref[...], axis=0, keepdims=True))
    sm = sm_ref[0]
    g = -jnp.exp(al_ref[...]) * _softplus(sm + dt_ref[...])
    beta = jax.nn.sigmoid(sm)
    row_id = _iota((8, DK_A), 0)
    outs = []
    for h in range(H_A):
        q = y[:, h * DK_A:(h + 1) * DK_A]
        k = y[:, W_QK_A + h * DK_A:W_QK_A + (h + 1) * DK_A]
        v = y[:, 2 * W_QK_A + h * DV_A:2 * W_QK_A + (h + 1) * DV_A]
        q = q * lax.rsqrt(jnp.sum(q * q, axis=1, keepdims=True) + EPS) * (DK_A ** -0.5)
        k = k * lax.rsqrt(jnp.sum(k * k, axis=1, keepdims=True) + EPS)
        eg = jnp.exp(g[:, SM_A + h:SM_A + h + 1])
        bh = beta[:, SM_B + h:SM_B + h + 1]
        s = s_ref[0, h]
        lhs = jnp.where(row_id == 0, _rows8(k), jnp.where(row_id == 1, _rows8(q), 0.0)).astype(BF16)
        rs = jnp.dot(lhs, s.astype(BF16), preferred_element_type=F32)
        v_new = bh * (v - eg * rs[0:1, :])
        o = eg * rs[1:2, :] + jnp.sum(q * k, axis=1, keepdims=True) * v_new
        so_ref[0, h] = s * eg + _col_from_row(k, DK_A) * v_new
        outs.append(o * lax.rsqrt(jnp.mean(o * o, axis=1, keepdims=True) + EPS) * gn_ref[...])
    oa_ref[0] = jnp.concatenate(outs, axis=1) * _silu(z_ref[0])


def _gdn_sample(full, conv_w, small, alog_row, dt_row, z, gnorm, state):
    db = full.shape[0]
    r3 = lambda i: (i, 0, 0)
    r4 = lambda i: (i, 0, 0, 0)
    return pl.pallas_call(
        _gdn_sample_kernel, grid=(db,),
        in_specs=[pl.BlockSpec((1,) + full.shape[1:], r3), _const_spec(conv_w.shape),
                  pl.BlockSpec((1, 1, LANES), r3), _const_spec(alog_row.shape), _const_spec(dt_row.shape),
                  pl.BlockSpec((1, 1, W_V_A), r3), _const_spec(gnorm.shape),
                  pl.BlockSpec((1,) + state.shape[1:], r4)],
        out_specs=[pl.BlockSpec((1, 1, W_V_A), r3), pl.BlockSpec((1,) + state.shape[1:], r4)],
        out_shape=[jax.ShapeDtypeStruct((db, 1, W_V_A), F32), jax.ShapeDtypeStruct(state.shape, F32)],
        compiler_params=_cparams("parallel"), name="gdn_sample")(
            full, conv_w, small, alog_row, dt_row, z, gnorm, state)


def _xattn_sample_kernel(q_ref, mk_ref, mv_ref, o_ref):
    n_rows = mk_ref.shape[0]
    n_mem = n_rows // 8
    dh = 2 * LANES
    q8 = q_ref[0] * (dh ** -0.5)
    red = jnp.sum(mk_ref[...].reshape(n_mem, 8, LANES) * q8[None], axis=-1, keepdims=True)
    own_lane = _iota((LANES, 8, LANES), 0) == _iota((LANES, 8, LANES), 2)
    n_t = n_mem // LANES
    tiles = [jnp.sum(jnp.where(own_lane, red[j * LANES:(j + 1) * LANES], 0.0), axis=0) for j in range(n_t)]
    s8 = jnp.concatenate(tiles, axis=1)
    s = s8[0:H_X] + s8[H_X:2 * H_X]
    p = jnp.exp(s - jnp.max(s, axis=-1, keepdims=True))
    inv_l = 1.0 / jnp.sum(p, axis=-1, keepdims=True)
    p8 = jnp.concatenate([p, p], axis=0)
    ones = jnp.ones((LANES, LANES), BF16)
    acc = None
    for j in range(n_t):
        z = jnp.where(own_lane, p8[None, :, j * LANES:(j + 1) * LANES], 0.0).reshape(LANES * 8, LANES)
        p3 = jnp.dot(z.astype(BF16), ones, preferred_element_type=F32).reshape(LANES, 8, LANES)
        v3 = mv_ref[j * LANES * 8:(j + 1) * LANES * 8, :].reshape(LANES, 8, LANES)
        part = jnp.sum(p3 * v3, axis=0)
        acc = part if acc is None else acc + part
    o_ref[0] = acc * jnp.concatenate([inv_l, inv_l], axis=0)


def _mem_rows(cache):
    dp, db, nm = cache.shape[:3]
    x = cache.reshape(dp, db, nm, H_X, 2, LANES)
    return jnp.transpose(x, (0, 1, 2, 4, 3, 5)).reshape(dp, db, nm * 8, LANES)


def _xattn_sample(q, mk_rows, mv_rows, layer):
    db, d = q.shape
    n_rows = mk_rows.shape[2]
    q8 = jnp.transpose(q.reshape(db, H_X, 2, LANES), (0, 2, 1, 3)).reshape(db, 8, LANES)
    r3 = lambda i: (i, 0, 0)
    mem = lambda i: (layer, i, 0, 0)
    o8 = pl.pallas_call(
        _xattn_sample_kernel, grid=(db,),
        in_specs=[pl.BlockSpec((1, 8, LANES), r3), pl.BlockSpec((None, None, n_rows, LANES), mem),
                  pl.BlockSpec((None, None, n_rows, LANES), mem)],
        out_specs=pl.BlockSpec((1, 8, LANES), r3), out_shape=jax.ShapeDtypeStruct((db, 8, LANES), F32),
        compiler_params=_cparams("parallel"), name="xattn_sample")(q8, mk_rows, mv_rows)
    return jnp.transpose(o8.reshape(db, 2, H_X, LANES), (0, 2, 1, 3)).reshape(db, d)


def _head_rows(h):
    return slice(h * DH_B, (h + 1) * DH_B)


def _fox_sample_kernel(pt_ref, q_ref, kn_ref, vn_ref, f_ref, bf_ref, *refs, g_pages):
    k_refs = refs[0:g_pages]
    v_refs = refs[g_pages:2 * g_pages]
    lf_refs = refs[2 * g_pages:3 * g_pages]
    o_ref, lfn_ref, qb_ref, m_ref, l_ref, acc_ref, carry_ref = refs[3 * g_pages:]
    del pt_ref
    w = H_B * DH_B
    gi = pl.program_id(1)
    bcast = lambda col: jnp.broadcast_to(col, (H_B, LANES))
    head_id = _iota((H_B, LANES), 0)

    def rows_to_tile(rows):
        n = rows[0].shape[1]
        out = jnp.broadcast_to(rows[0], (H_B, n))
        for h in range(1, H_B):
            out = jnp.where(head_id[:, 0:n] == h, rows[h], out)
        return out

    @pl.when(gi == 0)
    def _():
        q = q_ref[0] * (DH_B ** -0.5)
        qb_ref[...] = jnp.broadcast_to(_col_from_row(q, w), (w, LANES))
        lfn = _log_sigmoid(f_ref[0] + bf_ref[...])
        lfn_ref[0] = lfn
        qk = q * kn_ref[0]
        s_new = rows_to_tile([jnp.sum(qk[:, _head_rows(h)], axis=1, keepdims=True) for h in range(H_B)])
        m_ref[...] = bcast(s_new)
        l_ref[...] = jnp.ones_like(l_ref)
        acc_ref[...] = jnp.where(_iota((w, LANES), 1) == 0, _col_from_row(vn_ref[0], w), 0.0)
        carry_ref[...] = bcast(lfn)

    later = (_iota((PAGE, PAGE), 0) > _iota((PAGE, PAGE), 1)).astype(BF16)
    m = m_ref[:, 0:1]
    l = l_ref[:, 0:1]
    carry = carry_ref[:, 0:1]
    logits = [None] * g_pages
    top = None
    suffix = _dot3_r(jnp.concatenate([lf_refs[jj][...] for jj in range(g_pages)], axis=0), later)
    for jj in reversed(range(g_pages)):
        s_t = rows_to_tile([jnp.sum(k_refs[jj][h] * qb_ref[_head_rows(h), :], axis=0, keepdims=True)
                            for h in range(H_B)])
        lf_t = lf_refs[jj][...]
        logits[jj] = s_t + suffix[jj * H_B:(jj + 1) * H_B, :] + carry
        carry = carry + jnp.sum(lf_t, axis=1, keepdims=True)
        top = logits[jj] if top is None else jnp.maximum(top, logits[jj])
    m_new = jnp.maximum(m, jnp.max(top, axis=1, keepdims=True))
    alpha = jnp.exp(m - m_new)
    probs = [jnp.exp(lg - m_new) for lg in logits]
    p_sum = probs[0]
    for p_t in probs[1:]:
        p_sum = p_sum + p_t
    for h in range(H_B):
        acc_h = acc_ref[_head_rows(h), :] * alpha[h:h + 1, :]
        for jj in range(g_pages):
            acc_h = acc_h + v_refs[jj][h] * probs[jj][h:h + 1, :]
        acc_ref[_head_rows(h), :] = acc_h
    l = alpha * l + jnp.sum(p_sum, axis=1, keepdims=True)
    m_ref[...] = bcast(m_new)
    l_ref[...] = bcast(l)
    carry_ref[...] = bcast(carry)

    @pl.when(gi == pl.num_programs(1) - 1)
    def _():
        inv = 1.0 / l
        den = jnp.concatenate([jnp.broadcast_to(inv[h:h + 1, :], (DH_B, 1)) for h in range(H_B)], axis=0)
        col = jnp.sum(acc_ref[...], axis=1, keepdims=True) * den
        eye = _iota((w, w), 0) == _iota((w, w), 1)
        o_ref[0] = jnp.sum(jnp.where(eye, jnp.broadcast_to(col, (w, w)), 0.0), axis=0, keepdims=True)


def _fox_sample(q, k_new, v_new, f_col, bf_col, cache_kt, cache_vt, cache_lft, page_table, g_pages=16):
    db = q.shape[0]
    w = H_B * DH_B
    n_pages = page_table.shape[1]
    g_pages = min(g_pages, n_pages)
    ng = n_pages // g_pages
    r3 = lambda i, g, pt: (i, 0, 0)

    def page_map(nd):
        return [(lambda i, g, pt, jj=jj: (pt[i * n_pages + (ng - 1 - g) * g_pages + jj],) + (0,) * nd)
                for jj in range(g_pages)]

    in_specs = [pl.BlockSpec((1, 1, w), r3)] * 3 + [pl.BlockSpec((1, H_B, 1), r3),
                                                    pl.BlockSpec(bf_col.shape, lambda i, g, pt: (0, 0))]
    in_specs += [pl.BlockSpec((None, H_B, DH_B, PAGE), mp) for mp in page_map(3)] * 2
    in_specs += [pl.BlockSpec((None, H_B, PAGE), mp) for mp in page_map(2)]
    grid_spec = pltpu.PrefetchScalarGridSpec(
        num_scalar_prefetch=1, grid=(db, ng), in_specs=in_specs,
        out_specs=[pl.BlockSpec((1, 1, w), r3), pl.BlockSpec((1, H_B, 1), r3)],
        scratch_shapes=[pltpu.VMEM((w, LANES), F32), pltpu.VMEM((H_B, LANES), F32), pltpu.VMEM((H_B, LANES), F32),
                        pltpu.VMEM((w, LANES), F32), pltpu.VMEM((H_B, LANES), F32)])
    kern = functools.partial(_fox_sample_kernel, g_pages=g_pages)
    return pl.pallas_call(
        kern, grid_spec=grid_spec,
        out_shape=[jax.ShapeDtypeStruct((db, 1, w), F32), jax.ShapeDtypeStruct((db, H_B, 1), F32)],
        compiler_params=_cparams("parallel", "arbitrary"), name="fox_sample")(
            page_table.reshape(-1), q, k_new, v_new, f_col, bf_col,
            *([cache_kt] * g_pages), *([cache_vt] * g_pages), *([cache_lft] * g_pages))


def _row(v):
    return v.reshape(1, -1).astype(F32)


def _pad_lanes(v, start):
    return jnp.zeros((1, LANES), F32).at[0, start:start + v.shape[0]].set(v)


def _chunk_rows(cols, b, l):
    n = l // CHUNK_A
    return cols.reshape(b, n, CHUNK_A, H_A).transpose(0, 1, 3, 2).reshape(b, n, H_A * CHUNK_A)


def kernel(x_prompt, x_sample, mem_prompt, cache_fox_k, cache_fox_v, cache_fox_logf, page_table, state_gdn, state_gdn_conv, state_cfm_conv, cache_mem_k, cache_mem_v, norm_mix, w_in_e, conv_a, a_log, dt_bias, gnorm_a, b_f, w_out_e, w_pw1, b_pw1, w_dw, b_dw, ln_g, ln_b, w_pw2, norm_mem, norm_x, w_xq, w_xkv, w_xo, norm_f, w_up, w_down, norm_out):
    b, l, d = x_prompt.shape
    db = x_sample.shape[0]
    n_mem = mem_prompt.shape[1]
    depth = norm_mix.shape[0]
    dh_x = d // H_X
    bf = lambda w: w.astype(BF16)

    w_in = w_in_e[0]
    off_aa = W_CONV_A + W_V_A
    off_qb = off_aa + 2 * H_A
    off_fb = off_qb + 3 * W_B
    w_small = jnp.concatenate([w_in[:, off_aa:off_qb], w_in[:, off_fb:]], axis=1)
    w_small = jnp.pad(w_small, ((0, 0), (0, LANES - w_small.shape[1])))
    w_in_r = bf(jnp.concatenate([w_in[:, :off_aa], w_in[:, off_qb:off_fb], w_small], axis=1))
    in_widths = (W_CONV_A, W_V_A, W_B, W_B, W_B, LANES)
    off_kb = off_qb + W_B
    off_vb = off_kb + W_B
    w_in_p = bf(jnp.concatenate([w_in[:, :off_aa], w_in[:, off_qb:off_kb], w_in[:, off_vb:off_fb], w_small], axis=1))
    in_widths_p = (W_CONV_A, W_V_A, W_B, W_B, LANES)
    w_in_t = [bf(w_in[:, off_kb:off_vb].T), bf(w_in[:, off_vb:off_fb].T), bf(w_small.T)]
    w_out_a = bf(w_out_e[0][:W_V_A])
    w_out_b = bf(w_out_e[0][W_V_A:])
    w_xq_b = [bf(w_xq[i]) for i in range(depth)]
    w_xo_b = [bf(w_xo[i]) for i in range(depth)]
    w_up_b = [bf(w_up[i]) for i in range(depth)]
    w_down_b = [bf(w_down[i]) for i in range(depth)]
    w_pw1_b = bf(w_pw1[0])
    w_pw2_b = bf(w_pw2[0])
    alog_rows = _row(jnp.repeat(a_log[0], CHUNK_A))
    dt_rows = _row(jnp.repeat(dt_bias[0], CHUNK_A))
    gn = _row(gnorm_a[0])

    memf = mem_prompt.reshape(b * n_mem, d)
    mem_k, mem_v = [], []
    for i in range(depth):
        mk, mv = _linear([memf], [bf(w_xkv[i])], gain=_row(norm_mem[i]), out_widths=(d, d), name="mem_kv")
        mem_k.append(mk.reshape(b, n_mem, d))
        mem_v.append(mv.reshape(b, n_mem, d))
    mem_k_prompt = jnp.stack(mem_k).reshape(depth, b, n_mem, H_X, dh_x)
    mem_v_prompt = jnp.stack(mem_v).reshape(depth, b, n_mem, H_X, dh_x)

    xp = x_prompt.reshape(b * l, d)
    conv_in, z, q, v, small, k_t, v_t, small_t = _linear(
        [xp], [w_in_p], gain=_row(norm_mix[0]), out_widths=in_widths_p, wts=w_in_t, t_batch=(b, l), name="in_proj")
    fox_k_prompt = jnp.transpose(k_t.reshape(b, H_B, DH_B, l), (0, 3, 1, 2))[None]
    fox_v_prompt = jnp.transpose(v_t.reshape(b, H_B, DH_B, l), (0, 3, 1, 2))[None]
    conv_in3 = conv_in.reshape(b, l, W_CONV_A)
    gdn_conv_prompt = conv_in3[:, l - (CONV_A - 1):, :][None]
    oa, s_fin = _gdn_prompt(conv_in3, z.reshape(b, l, W_V_A),
                            _chunk_rows(small[:, SM_A:SM_A + H_A], b, l),
                            _chunk_rows(small[:, SM_B:SM_B + H_A], b, l),
                            conv_a[0], alog_rows, dt_rows, gn)
    gdn_state_prompt = s_fin.reshape(b, DK_A, H_A, DV_A).transpose(0, 2, 1, 3)[None]
    qf, kft, vf, lft = _fox_prep(q.reshape(b, l, W_B), k_t, v.reshape(b, l, W_B), small.reshape(b, l, LANES),
                                 small_t, _pad_lanes(b_f[0], SM_F), b_f[0].reshape(H_B, 1))
    fox_logf_prompt = jnp.transpose(lft, (0, 2, 1))[None]
    ob = _fox_flash(qf, kft, vf)
    x = _linear([oa.reshape(b * l, W_V_A), ob.reshape(b * l, W_B)], [w_out_a, w_out_b], res=xp, name="out_proj")
    x = _xattn_prompt(x.reshape(b, l, d), _row(norm_x[0]), w_xq_b[0], bf(mem_k[0]), bf(mem_v[0]), w_xo_b[0])
    x = _mlp(x.reshape(b * l, d), _row(norm_f[0]), w_up_b[0], w_down_b[0])
    glu = _linear([x], [w_pw1_b], gain=_row(norm_mix[1]), bias=_row(b_pw1[0]), act="glu", name="pw1_glu")
    glu3 = glu.reshape(b, l, d)
    cfm_conv_prompt = glu3[:, l - (CONV_C - 1):, :][None]
    x = _cfm_prompt(glu3, x.reshape(b, l, d), w_dw[0], _row(b_dw[0]), _row(ln_g[0]), _row(ln_b[0]), w_pw2_b)
    x = _xattn_prompt(x, _row(norm_x[1]), w_xq_b[1], bf(mem_k[1]), bf(mem_v[1]), w_xo_b[1])
    y_prompt = _mlp(x.reshape(b * l, d), _row(norm_f[1]), w_up_b[1], w_down_b[1],
                    final_gain=_row(norm_out)).reshape(b, l, d)

    xs = x_sample.reshape(db, d)
    conv_s, z_s, q_s, k_s, v_s, small_s = _linear([xs], [w_in_r], gain=_row(norm_mix[0]),
                                                  out_widths=in_widths, name="in_proj_s")
    fox_k_sample = k_s.reshape(1, db, 1, H_B, DH_B)
    fox_v_sample = v_s.reshape(1, db, 1, H_B, DH_B)
    full = jnp.concatenate([state_gdn_conv[0], conv_s[:, None, :]], axis=1)
    gdn_conv_sample = full[:, 1:, :][None]
    oa_s, s_new = _gdn_sample(full, conv_a[0], small_s.reshape(db, 1, LANES), _pad_lanes(a_log[0], SM_A),
                              _pad_lanes(dt_bias[0], SM_A), z_s.reshape(db, 1, W_V_A), gn, state_gdn[0])
    gdn_state_sample = s_new[None]
    ob_s, lf_s = _fox_sample(q_s.reshape(db, 1, W_B), k_s.reshape(db, 1, W_B), v_s.reshape(db, 1, W_B),
                             small_s[:, SM_F:SM_F + H_B].reshape(db, H_B, 1), b_f[0].reshape(H_B, 1),
                             jnp.transpose(cache_fox_k[0], (0, 2, 3, 1)), jnp.transpose(cache_fox_v[0], (0, 2, 3, 1)),
                             jnp.transpose(cache_fox_logf[0], (0, 2, 1)), page_table)
    fox_logf_sample = lf_s.reshape(1, db, 1, H_B)
    x = _linear([oa_s.reshape(db, W_V_A), ob_s.reshape(db, W_B)], [w_out_a, w_out_b], res=xs, name="out_proj_s")

    mk_rows = _mem_rows(cache_mem_k)
    mv_rows = _mem_rows(cache_mem_v)

    def xattn_s(x, i):
        qx = _linear([x], [w_xq_b[i]], gain=_row(norm_x[i]), name="xq_s")
        o = _xattn_sample(qx, mk_rows, mv_rows, i)
        return _linear([o], [w_xo_b[i]], res=x, name="xo_s")

    x = xattn_s(x, 0)
    x = _mlp(x, _row(norm_f[0]), w_up_b[0], w_down_b[0])
    glu_s = _linear([x], [w_pw1_b], gain=_row(norm_mix[1]), bias=_row(b_pw1[0]), act="glu", name="pw1_glu_s")
    cfm_conv_sample = jnp.concatenate([state_cfm_conv[0][:, 1:, :], glu_s[:, None, :]], axis=1)[None]
    x = _cfm_sample(state_cfm_conv[0].transpose(1, 0, 2), glu_s, x, w_dw[0], _row(b_dw[0]), _row(ln_g[0]),
                    _row(ln_b[0]), w_pw2_b)
    x = xattn_s(x, 1)
    y_sample = _mlp(x, _row(norm_f[1]), w_up_b[1], w_down_b[1], final_gain=_row(norm_out)).reshape(db, 1, d)

    return (y_prompt, y_sample, fox_k_prompt, fox_v_prompt, fox_logf_prompt, fox_k_sample, fox_v_sample,
            fox_logf_sample, gdn_state_prompt, gdn_conv_prompt, gdn_state_sample, gdn_conv_sample,
            cfm_conv_prompt, cfm_conv_sample, mem_k_prompt, mem_v_prompt)
```

```python
import functools
import math

import jax
import jax.numpy as jnp
import numpy as np
from jax import lax
from jax.experimental import pallas as pl
from jax.experimental.pallas import tpu as pltpu

F32 = jnp.float32
BF16 = jnp.bfloat16
EPS = 1e-6
NEG = -1e30
LOG2E = 1.4426950408889634

H_A = 4
DK_A = 128
DV_A = 128
CONV_A = 4
CHUNK_A = 64
H_B = 8
DH_B = 64
H_X = 4
CONV_C = 31
PAGE = 128

W_QK_A = H_A * DK_A
W_V_A = H_A * DV_A
W_CONV_A = 2 * W_QK_A + W_V_A
W_B = H_B * DH_B
LANES = 128
SM_A = 0
SM_B = H_A
SM_F = 2 * H_A

VMEM_LIMIT = 56 * 1024 * 1024


def _cparams(*sem):
    return pltpu.CompilerParams(dimension_semantics=sem, vmem_limit_bytes=VMEM_LIMIT)


def _const_spec(shape):
    nd = len(shape)
    return pl.BlockSpec(shape, lambda *_: (0,) * nd)


def _rms(x, g):
    return x * lax.rsqrt(jnp.mean(x * x, axis=-1, keepdims=True) + EPS) * g


def _silu(x):
    return x * jax.nn.sigmoid(x)


def _softplus(x):
    return jnp.maximum(x, 0.0) + jnp.log1p(jnp.exp(-jnp.abs(x)))


def _split3(x):
    h = x.astype(BF16)
    r = x - h.astype(F32)
    m = r.astype(BF16)
    l = (r - m.astype(F32)).astype(BF16)
    return h, m, l


def _dot3_l(mat01, x):
    h, m, l = _split3(x)
    d = lambda t: jnp.dot(mat01, t, preferred_element_type=F32)
    return d(h) + d(m) + d(l)


def _dot3_r(x, mat01):
    h, m, l = _split3(x)
    d = lambda t: jnp.dot(t, mat01, preferred_element_type=F32)
    return d(h) + d(m) + d(l)


def _dot_nt(a, b):
    return lax.dot_general(a, b, (((1,), (1,)), ((), ())), preferred_element_type=F32)


def _dot_tn(a, b):
    return lax.dot_general(a, b, (((0,), (0,)), ((), ())), preferred_element_type=F32)


def _iota(shape, dim):
    return lax.broadcasted_iota(jnp.int32, shape, dim)


def _col_from_row(row, n):
    eye = _iota((n, n), 0) == _iota((n, n), 1)
    return jnp.sum(jnp.where(eye, jnp.broadcast_to(row, (n, n)), 0.0), axis=1, keepdims=True)


def _linear_kernel(*refs, n_in, n_t, has_gain, has_bias, has_res, act, out_widths, chunk):
    it = iter(refs)
    x_refs = [next(it) for _ in range(n_in)]
    w_refs = [next(it) for _ in range(n_in)]
    wt_refs = [next(it) for _ in range(n_t)]
    gain_ref = next(it) if has_gain else None
    bias_ref = next(it) if has_bias else None
    res_ref = next(it) if has_res else None
    out_refs = [next(it) for _ in out_widths]
    outt_refs = [next(it) for _ in range(n_t)]
    xs = []
    for i, xr in enumerate(x_refs):
        x = xr[...]
        if has_gain and i == 0:
            x = _rms(x, gain_ref[...])
        xs.append(x.astype(BF16))
    n_total = sum(out_widths)

    def mm(col0, cw):
        acc = None
        for xb, wr in zip(xs, w_refs):
            d = jnp.dot(xb, wr[:, col0:col0 + cw], preferred_element_type=F32)
            acc = d if acc is None else acc + d
        if has_bias:
            acc = acc + bias_ref[:, col0:col0 + cw]
        return acc

    off = 0
    for o_ref, width in zip(out_refs, out_widths):
        for c0 in range(0, width, chunk):
            cw = min(chunk, width - c0)
            y = mm(off + c0, cw)
            if act == "glu":
                y = y * jax.nn.sigmoid(mm(n_total + off + c0, cw))
            if has_res:
                y = y + res_ref[:, off + c0:off + c0 + cw]
            o_ref[:, c0:c0 + cw] = y
        off += width
    for wt_ref, ot_ref in zip(wt_refs, outt_refs):
        ot_ref[0] = _dot_nt(wt_ref[...], xs[0])


def _linear(xs, ws, *, gain=None, bias=None, res=None, act=None, out_widths=None, wts=(), t_batch=None,
            tm=512, chunk=512, name="linear"):
    m = xs[0].shape[0]
    tm = min(tm, m)
    assert m % tm == 0
    n_mm = ws[0].shape[1]
    n_out = n_mm // 2 if act == "glu" else n_mm
    if out_widths is None:
        out_widths = (n_out,)
    assert sum(out_widths) == n_out
    row = lambda i: (i, 0)
    in_specs = [pl.BlockSpec((tm, x.shape[1]), row) for x in xs]
    in_specs += [_const_spec(w.shape) for w in ws]
    in_specs += [_const_spec(w.shape) for w in wts]
    args = list(xs) + list(ws) + list(wts)
    if gain is not None:
        in_specs.append(_const_spec(gain.shape)); args.append(gain)
    if bias is not None:
        in_specs.append(_const_spec(bias.shape)); args.append(bias)
    if res is not None:
        in_specs.append(pl.BlockSpec((tm, res.shape[1]), row)); args.append(res)
    out_shape = [jax.ShapeDtypeStruct((m, w), F32) for w in out_widths]
    out_specs = [pl.BlockSpec((tm, w), row) for w in out_widths]
    if wts:
        tb, tl = t_batch
        assert tb * tl == m and tl % tm == 0
        per = tl // tm
        out_shape += [jax.ShapeDtypeStruct((tb, w.shape[0], tl), F32) for w in wts]
        out_specs += [pl.BlockSpec((1, w.shape[0], tm), lambda i: (i // per, 0, i % per)) for w in wts]
    kern = functools.partial(_linear_kernel, n_in=len(xs), n_t=len(wts), has_gain=gain is not None,
                             has_bias=bias is not None, has_res=res is not None, act=act,
                             out_widths=tuple(out_widths), chunk=chunk)
    outs = pl.pallas_call(kern, grid=(m // tm,), in_specs=in_specs, out_specs=out_specs,
                          out_shape=out_shape, compiler_params=_cparams("parallel"), name=name)(*args)
    return outs if len(outs) > 1 else outs[0]


def _mlp_kernel(*refs, chunk, final_norm):
    if final_norm:
        x_ref, g_ref, wu_ref, wd_ref, go_ref, o_ref = refs
    else:
        x_ref, g_ref, wu_ref, wd_ref, o_ref = refs
    x = x_ref[...]
    xn = _rms(x, g_ref[...]).astype(BF16)
    acc = x
    d_ff = wu_ref.shape[1]
    for c0 in range(0, d_ff, chunk):
        h = jnp.dot(xn, wu_ref[:, c0:c0 + chunk], preferred_element_type=F32)
        h = jnp.square(jnp.maximum(h, 0.0)).astype(BF16)
        acc = acc + jnp.dot(h, wd_ref[c0:c0 + chunk, :], preferred_element_type=F32)
    if final_norm:
        acc = _rms(acc, go_ref[...])
    o_ref[...] = acc


def _mlp(x, gain, w_up, w_down, final_gain=None, tm=512, chunk=512):
    m, d = x.shape
    tm = min(tm, m)
    row = lambda i: (i, 0)
    in_specs = [pl.BlockSpec((tm, d), row), _const_spec(gain.shape), _const_spec(w_up.shape),
                _const_spec(w_down.shape)]
    args = [x, gain, w_up, w_down]
    if final_gain is not None:
        in_specs.append(_const_spec(final_gain.shape)); args.append(final_gain)
    kern = functools.partial(_mlp_kernel, chunk=chunk, final_norm=final_gain is not None)
    return pl.pallas_call(kern, grid=(m // tm,), in_specs=in_specs,
                          out_specs=pl.BlockSpec((tm, d), row),
                          out_shape=jax.ShapeDtypeStruct((m, d), F32),
                          compiler_params=_cparams("parallel"), name="mlp")(*args)


def _xattn_kernel(x_ref, g_ref, wq_ref, mk_ref, mv_ref, wo_ref, o_ref):
    x = x_ref[0]
    d = x.shape[1]
    dh = d // H_X
    xn = _rms(x, g_ref[...]).astype(BF16)
    q = jnp.dot(xn, wq_ref[...], preferred_element_type=F32) * (dh ** -0.5)
    q = q.astype(BF16)
    head = lambda h: slice(h * dh, (h + 1) * dh)
    scores = lambda h: _dot_nt(q[:, head(h)], mk_ref[0, :, head(h)])
    outs = []
    s_next = scores(0)
    for h in range(H_X):
        sl = head(h)
        s = s_next
        if h + 1 < H_X:
            s_next = scores(h + 1)
        p = jnp.exp(s - jnp.max(s, axis=-1, keepdims=True))
        l = jnp.sum(p, axis=-1, keepdims=True)
        o = jnp.dot(p.astype(BF16), mv_ref[0, :, sl], preferred_element_type=F32)
        outs.append((o * (1.0 / l)).astype(BF16))
    o = jnp.concatenate(outs, axis=1)
    o_ref[0] = x + jnp.dot(o, wo_ref[...], preferred_element_type=F32)


def _xattn_prompt(x, gain, wq, mk, mv, wo, tm=512):
    b, l, d = x.shape
    tm = min(tm, l)
    nm = mk.shape[1]
    blk = lambda bi, i: (bi, i, 0)
    mem = lambda bi, i: (bi, 0, 0)
    return pl.pallas_call(
        _xattn_kernel, grid=(b, l // tm),
        in_specs=[pl.BlockSpec((1, tm, d), blk), _const_spec(gain.shape), _const_spec(wq.shape),
                  pl.BlockSpec((1, nm, d), mem), pl.BlockSpec((1, nm, d), mem), _const_spec(wo.shape)],
        out_specs=pl.BlockSpec((1, tm, d), blk),
        out_shape=jax.ShapeDtypeStruct((b, l, d), F32),
        compiler_params=_cparams("parallel", "parallel"), name="xattn_prompt")(x, gain, wq, mk, mv, wo)


def _log_sigmoid(x):
    return jnp.minimum(x, 0.0) - jnp.log1p(jnp.exp(-jnp.abs(x)))


def _head_lanes(x, h):
    p = h // 2
    blk = x[:, p * LANES:(p + 1) * LANES]
    if h % 2:
        blk = pltpu.roll(blk, DH_B, 1)
    return blk


def _fox_prep_kernel(q_ref, kt_ref, v_ref, sm_ref, smt_ref, bf_ref, bfc_ref, qf_ref, kft_ref, vf_ref, lft_ref,
                     carry_ref, carryt_ref):
    t = q_ref.shape[1]

    @pl.when(pl.program_id(1) == 0)
    def _():
        carry_ref[...] = jnp.zeros_like(carry_ref)
        carryt_ref[...] = jnp.zeros_like(carryt_ref)

    lane = _iota((t, LANES), 1)
    gate_lane = (lane >= SM_F) & (lane < SM_F + H_B)
    logf = jnp.where(gate_lane, _log_sigmoid(sm_ref[0] + bf_ref[...]), 0.0)
    before = _iota((t, t), 0) >= _iota((t, t), 1)
    c = _dot3_l(before.astype(BF16), logf) + carry_ref[0:1, :]
    carry_ref[0:1, :] = c[t - 1:t, :]
    c1, c2, c3 = _split3(c * LOG2E)
    c1, c2, c3 = c1.astype(F32), c2.astype(F32), c3.astype(F32)
    lft = _log_sigmoid(smt_ref[0] + bfc_ref[...])
    lft_ref[0] = lft
    ct = _dot3_r(lft, (_iota((t, t), 0) <= _iota((t, t), 1)).astype(BF16)) + carryt_ref[:, 0:1]
    carryt_ref[...] = jnp.broadcast_to(ct[:, t - 1:t], carryt_ref.shape)
    r1, r2, r3 = _split3(ct * LOG2E)
    r1, r2, r3 = r1.astype(F32), r2.astype(F32), r3.astype(F32)
    q = q_ref[0] * (DH_B ** -0.5 * LOG2E)
    v = v_ref[0]
    low = lane < DH_B
    row = _iota((DH_B, t), 0)
    for h in range(H_B):
        col = slice(SM_F + h, SM_F + h + 1)
        b1, b2, b3 = c1[:, col], c2[:, col], c3[:, col]
        qx = jnp.where(lane == DH_B, b1, jnp.where(lane == DH_B + 1, b2, jnp.where(
            lane == DH_B + 2, b3, jnp.where(lane < DH_B + 6, 1.0, 0.0))))
        kx = jnp.where(row < 3, 1.0, jnp.where(row == 3, -r1[h:h + 1, :], jnp.where(
            row == 4, -r2[h:h + 1, :], jnp.where(row == 5, -r3[h:h + 1, :], 0.0))))
        vx = jnp.where(lane == DH_B, 1.0, 0.0)
        qf_ref[0, h] = jnp.where(low, _head_lanes(q, h), qx).astype(BF16)
        kft_ref[0, h] = jnp.concatenate([kt_ref[0, h * DH_B:(h + 1) * DH_B, :], kx], axis=0).astype(BF16)
        vf_ref[0, h] = jnp.where(low, _head_lanes(v, h), vx).astype(BF16)


def _fox_prep(q, kt, v, small, small_t, bf_row, bf_col, t=256):
    b, l, w = q.shape
    t = min(t, l)
    blk = lambda bi, i: (bi, i, 0)
    tblk = lambda bi, i: (bi, 0, i)
    hblk = lambda bi, i: (bi, 0, i, 0)
    hshape = jax.ShapeDtypeStruct((b, H_B, l, LANES), BF16)
    return pl.pallas_call(
        _fox_prep_kernel, grid=(b, l // t),
        in_specs=[pl.BlockSpec((1, t, w), blk), pl.BlockSpec((1, w, t), tblk), pl.BlockSpec((1, t, w), blk),
                  pl.BlockSpec((1, t, LANES), blk),
                  pl.BlockSpec((1, H_B, t), lambda bi, i: (bi, SM_F // H_B, i)),
                  _const_spec(bf_row.shape), _const_spec(bf_col.shape)],
        out_specs=[pl.BlockSpec((1, H_B, t, LANES), hblk),
                   pl.BlockSpec((1, H_B, LANES, t), lambda bi, i: (bi, 0, 0, i)),
                   pl.BlockSpec((1, H_B, t, LANES), hblk), pl.BlockSpec((1, H_B, t), tblk)],
        out_shape=[hshape, jax.ShapeDtypeStruct((b, H_B, LANES, l), BF16), hshape,
                   jax.ShapeDtypeStruct((b, H_B, l), F32)],
        scratch_shapes=[pltpu.VMEM((8, LANES), F32), pltpu.VMEM((H_B, LANES), F32)],
        compiler_params=_cparams("parallel", "arbitrary"), name="fox_prep")(
            q, kt, v, small, small_t, bf_row, bf_col)


LOOKAHEAD = 2


def _fox_flash_kernel(it_ref, jt_ref, fl_ref, qf_ref, kft_ref, vf_ref, o_ref, m_ref, acc_ref):
    p_id = pl.program_id(1)
    i = it_ref[p_id]
    j = jt_ref[p_id]
    flags = fl_ref[p_id]
    tq = qf_ref.shape[2]
    tk = kft_ref.shape[3]

    @pl.when(j == 0)
    def _():
        m_ref[...] = jnp.full_like(m_ref, NEG)
        acc_ref[...] = jnp.zeros_like(acc_ref)

    def heads(masked, row0=0):
        nr = tq - row0
        rs = slice(row0, tq)
        if masked:
            visible = _iota((nr, tk), 1) + j * tk <= _iota((nr, tk), 0) + (i * tq + row0)

        def scores(h):
            s = jnp.dot(qf_ref[0, h, rs, :], kft_ref[0, h], preferred_element_type=F32)
            return jnp.where(visible, s, NEG) if masked else s

        pending = [scores(h) for h in range(LOOKAHEAD)]
        for h in range(H_B):
            s = pending.pop(0)
            if h + LOOKAHEAD < H_B:
                pending.append(scores(h + LOOKAHEAD))
            m_prev = m_ref[h, rs, :]
            m_new = jnp.maximum(m_prev, jnp.max(s, axis=-1, keepdims=True))
            alpha = jnp.exp2(m_prev - m_new)
            p = jnp.exp2(s - jnp.concatenate([m_new] * (tk // LANES), axis=1)).astype(BF16)
            acc_ref[h, rs, :] = alpha * acc_ref[h, rs, :] + jnp.dot(p, vf_ref[0, h], preferred_element_type=F32)
            m_ref[h, rs, :] = m_new

    @pl.when((flags & 5) == 0)
    def _():
        heads(False)

    @pl.when((flags & 5) == 1)
    def _():
        heads(True)

    @pl.when((flags & 4) == 4)
    def _():
        heads(True, tq // 2)

    @pl.when((flags & 2) == 2)
    def _():
        lane = _iota((tq, LANES), 1)
        for pr in range(H_B // 2):
            a0 = acc_ref[2 * pr]
            a1 = acc_ref[2 * pr + 1]
            o0 = a0 * (1.0 / a0[:, DH_B:DH_B + 1])
            o1 = a1 * (1.0 / a1[:, DH_B:DH_B + 1])
            o_ref[0, :, pr * LANES:(pr + 1) * LANES] = jnp.where(lane < DH_B, o0, pltpu.roll(o1, DH_B, 1))


def _fox_flash(qf, kft, vf, tq=1024, tk=512):
    b, hh, l, _ = qf.shape
    tk = min(tk, l)
    tq = tq if l % tq == 0 else tk
    assert l % tq == 0 and l % tk == 0
    its, jts, fls = [], [], []
    for i in range(l // tq):
        j_last = ((i + 1) * tq - 1) // tk
        for j in range(j_last + 1):
            crosses = (j + 1) * tk - 1 > i * tq
            lower_only = j * tk > i * tq + tq // 2 - 1
            its.append(i)
            jts.append(j)
            fls.append(int(crosses) + 2 * int(j == j_last) + 4 * int(lower_only))
    it, jt, fl = (jnp.asarray(np.array(a, np.int32)) for a in (its, jts, fls))
    qmap = lambda bi, p, it, jt, fl: (bi, 0, it[p], 0)
    kmap = lambda bi, p, it, jt, fl: (bi, 0, jt[p], 0)
    ktmap = lambda bi, p, it, jt, fl: (bi, 0, 0, jt[p])
    omap = lambda bi, p, it, jt, fl: (bi, it[p], 0)
    grid_spec = pltpu.PrefetchScalarGridSpec(
        num_scalar_prefetch=3, grid=(b, len(its)),
        in_specs=[pl.BlockSpec((1, hh, tq, LANES), qmap), pl.BlockSpec((1, hh, LANES, tk), ktmap),
                  pl.BlockSpec((1, hh, tk, LANES), kmap)],
        out_specs=pl.BlockSpec((1, tq, hh * DH_B), omap),
        scratch_shapes=[pltpu.VMEM((hh, tq, LANES), F32), pltpu.VMEM((hh, tq, LANES), F32)])
    return pl.pallas_call(
        _fox_flash_kernel, grid_spec=grid_spec,
        out_shape=jax.ShapeDtypeStruct((b, l, hh * DH_B), F32),
        compiler_params=_cparams("parallel", "arbitrary"), name="fox_flash")(it, jt, fl, qf, kft, vf)


def _gdn_kernel(x_ref, z_ref, a_ref, b_ref, cw_ref, al_ref, dt_ref, gn_ref, oa_ref, s_ref,
                xs_ref, st_ref, *, nc):
    c = CHUNK_A
    r = H_A * c
    t = nc * c
    halo = 8
    nb = x_ref.shape[0]

    @pl.when(pl.program_id(0) == 0)
    def _():
        xs_ref[:, 0:halo, :] = jnp.zeros((nb, halo, xs_ref.shape[2]), F32)
        st_ref[...] = jnp.zeros_like(st_ref)

    @pl.when(pl.program_id(0) != 0)
    def _():
        xs_ref[:, 0:halo, :] = xs_ref[:, t:t + halo, :]

    ri = _iota((r, r), 0)
    ci = _iota((r, r), 1)
    same = (ri // c) == (ci // c)
    incl = same & (ri >= ci)
    strict = same & (ri > ci)
    cum = (same & (ri <= ci)).astype(BF16)
    eye = ri == ci
    last = ((ci % c) == (c - 1)) & same

    def stack(a):
        return jnp.concatenate([a[:, h * DK_A:(h + 1) * DK_A] for h in range(H_A)], axis=0)

    pre = []
    for bi in range(nb):
        xs_ref[bi, halo:halo + t, :] = x_ref[bi]
        y = None
        for jj in range(CONV_A):
            o = halo - (CONV_A - 1) + jj
            term = xs_ref[bi, o:o + t, :] * cw_ref[jj:jj + 1, :]
            y = term if y is None else y + term
        y = _silu(y)
        g_rows = -jnp.exp(al_ref[...]) * _softplus(a_ref[bi, 0] + dt_ref[...])
        beta_rows = jax.nn.sigmoid(b_ref[bi, 0])
        gc_rows = _dot3_r(g_rows, cum)
        for n in range(nc):
            rows = slice(n * c, (n + 1) * c)
            gc_r = gc_rows[n:n + 1, :]
            gcb = jnp.broadcast_to(gc_r, (r, r))
            gc_c = jnp.sum(jnp.where(eye, gcb, 0.0), axis=1, keepdims=True)
            gl_c = jnp.sum(jnp.where(last, gcb, 0.0), axis=1, keepdims=True)
            beta_c = jnp.sum(jnp.where(eye, jnp.broadcast_to(beta_rows[n:n + 1, :], (r, r)), 0.0),
                             axis=1, keepdims=True)
            q = stack(y[rows, 0:W_QK_A])
            k = stack(y[rows, W_QK_A:2 * W_QK_A])
            v = stack(y[rows, 2 * W_QK_A:W_CONV_A])
            q = q * lax.rsqrt(jnp.sum(q * q, axis=1, keepdims=True) + EPS) * (DK_A ** -0.5)
            k = k * lax.rsqrt(jnp.sum(k * k, axis=1, keepdims=True) + EPS)
            decay = jnp.exp(jnp.where(incl, gc_c - gc_r, -jnp.inf))
            kb = k * beta_c
            k16 = k.astype(BF16)
            eg = jnp.exp(gc_c)
            pre.append(dict(decay=decay, kk=_dot_nt(kb.astype(BF16), k16), qk=_dot_nt(q.astype(BF16), k16),
                            rhs=jnp.concatenate([v * beta_c, kb * eg], axis=1).astype(BF16),
                            qg=(q * eg).astype(BF16), kdec=(k * jnp.exp(gl_c - gc_c)).astype(BF16),
                            dec=[jnp.exp(gl_c[h * c:h * c + 1, :]) for h in range(H_A)]))
    for pc in pre:
        pc["a_qk"] = jnp.where(incl, pc["qk"] * pc["decay"], 0.0).astype(BF16)
        pc["mneg"] = jnp.where(strict, -(pc["kk"] * pc["decay"]), 0.0)
        pc["tinv"] = jnp.where(eye, 1.0, 0.0) + pc["mneg"]
    for _ in range(int(math.log2(c)) - 1):
        for pc in pre:
            m16 = pc["mneg"].astype(BF16)
            pc["mneg"] = jnp.dot(m16, m16, preferred_element_type=F32)
        for pc in pre:
            pc["tinv"] = pc["tinv"] + jnp.dot(pc["tinv"].astype(BF16), pc["mneg"].astype(BF16),
                                              preferred_element_type=F32)
    for pc in pre:
        uw = jnp.dot(pc["tinv"].astype(BF16), pc["rhs"], preferred_element_type=F32)
        pc["u"] = uw[:, 0:DV_A]
        pc["w"] = uw[:, DV_A:2 * DV_A].astype(BF16)

    s_heads = [[st_ref[bi, :, h * DV_A:(h + 1) * DV_A] for h in range(H_A)] for bi in range(nb)]
    for n in range(nc):
        rows = slice(n * c, (n + 1) * c)
        for bi in range(nb):
            pc = pre[bi * nc + n]
            v_parts, qs_parts = [], []
            for h in range(H_A):
                hr = slice(h * c, (h + 1) * c)
                wq = jnp.concatenate([pc["w"][hr], pc["qg"][hr]], axis=0)
                wq_s = jnp.dot(wq, s_heads[bi][h].astype(BF16), preferred_element_type=F32)
                v_parts.append(pc["u"][hr] - wq_s[0:c])
                qs_parts.append(wq_s[c:2 * c])
            v_new = jnp.concatenate(v_parts, axis=0).astype(BF16)
            o = jnp.concatenate(qs_parts, axis=0) + jnp.dot(pc["a_qk"], v_new, preferred_element_type=F32)
            for h in range(H_A):
                hr = slice(h * c, (h + 1) * c)
                s_heads[bi][h] = s_heads[bi][h] * pc["dec"][h] + _dot_tn(pc["kdec"][hr], v_new[hr])
            on = o * lax.rsqrt(jnp.mean(o * o, axis=1, keepdims=True) + EPS) * gn_ref[...]
            on = jnp.concatenate([on[h * c:(h + 1) * c, :] for h in range(H_A)], axis=1)
            oa_ref[bi, rows, :] = on * _silu(z_ref[bi, rows, :])
    for bi in range(nb):
        for h in range(H_A):
            st_ref[bi, :, h * DV_A:(h + 1) * DV_A] = s_heads[bi][h]

    @pl.when(pl.program_id(0) == pl.num_programs(0) - 1)
    def _():
        s_ref[...] = st_ref[...]


def _gdn_prompt(conv_in, z, a_rows, b_rows, conv_w, alog_row, dt_row, gnorm, nc=4):
    b, l, wc = conv_in.shape
    n = l // CHUNK_A
    nc = min(nc, n)
    t = nc * CHUNK_A
    r = H_A * CHUNK_A
    blk = lambda i: (0, i, 0)
    gate = lambda i: (0, i, 0, 0)
    a_rows = a_rows.reshape(b, n // nc, nc, r)
    b_rows = b_rows.reshape(b, n // nc, nc, r)
    kern = functools.partial(_gdn_kernel, nc=nc)
    oa, s = pl.pallas_call(
        kern, grid=(n // nc,),
        in_specs=[pl.BlockSpec((b, t, wc), blk), pl.BlockSpec((b, t, W_V_A), blk),
                  pl.BlockSpec((b, 1, nc, r), gate), pl.BlockSpec((b, 1, nc, r), gate),
                  _const_spec(conv_w.shape), _const_spec(alog_row.shape), _const_spec(dt_row.shape),
                  _const_spec(gnorm.shape)],
        out_specs=[pl.BlockSpec((b, t, W_V_A), blk), _const_spec((b, DK_A, H_A * DV_A))],
        out_shape=[jax.ShapeDtypeStruct((b, l, W_V_A), F32),
                   jax.ShapeDtypeStruct((b, DK_A, H_A * DV_A), F32)],
        scratch_shapes=[pltpu.VMEM((b, t + 8, wc), F32), pltpu.VMEM((b, DK_A, H_A * DV_A), F32)],
        compiler_params=_cparams("arbitrary"), name="gdn_prompt")(
            conv_in, z, a_rows, b_rows, conv_w, alog_row, dt_row, gnorm)
    return oa, s


def _cfm_tail(y, bd_ref, lg_ref, lb_ref, w2_ref, res):
    y = y + bd_ref[...]
    yc = y - jnp.mean(y, axis=-1, keepdims=True)
    yn = yc * lax.rsqrt(jnp.mean(yc * yc, axis=-1, keepdims=True) + EPS) * lg_ref[...] + lb_ref[...]
    return res + jnp.dot(_silu(yn).astype(BF16), w2_ref[...], preferred_element_type=F32)


def _cfm_kernel(u_ref, x_ref, wd_ref, bd_ref, lg_ref, lb_ref, w2_ref, o_ref, xs_ref, sh_ref):
    t = u_ref.shape[1]
    halo = 32

    @pl.when(pl.program_id(1) == 0)
    def _():
        xs_ref[0:halo, :] = jnp.zeros((halo, xs_ref.shape[1]), F32)

    @pl.when(pl.program_id(1) != 0)
    def _():
        xs_ref[0:halo, :] = xs_ref[t:t + halo, :]

    xs_ref[halo:halo + t, :] = u_ref[0]
    y = None
    for r in range(8):
        offs = [o for o in range(halo - (CONV_C - 1), halo + 1) if o % 8 == r]
        if not offs:
            continue
        if r:
            n_rows = max(offs) - r + t
            sh_ref[0:n_rows, :] = xs_ref[r:r + n_rows, :]
        win_ref = sh_ref if r else xs_ref
        for o in offs:
            jj = o - (halo - (CONV_C - 1))
            term = win_ref[o - r:o - r + t, :] * wd_ref[jj:jj + 1, :]
            y = term if y is None else y + term
    o_ref[0] = _cfm_tail(y, bd_ref, lg_ref, lb_ref, w2_ref, x_ref[0])


def _cfm_prompt(u, x, w_dw, b_dw, ln_g, ln_b, w2, t=512):
    b, l, d = u.shape
    t = min(t, l)
    blk = lambda bi, i: (bi, i, 0)
    return pl.pallas_call(
        _cfm_kernel, grid=(b, l // t),
        in_specs=[pl.BlockSpec((1, t, d), blk), pl.BlockSpec((1, t, d), blk), _const_spec(w_dw.shape),
                  _const_spec(b_dw.shape), _const_spec(ln_g.shape), _const_spec(ln_b.shape),
                  _const_spec(w2.shape)],
        out_specs=pl.BlockSpec((1, t, d), blk),
        out_shape=jax.ShapeDtypeStruct((b, l, d), F32),
        scratch_shapes=[pltpu.VMEM((t + 32, d), F32), pltpu.VMEM((t + 32, d), F32)],
        compiler_params=_cparams("parallel", "arbitrary"), name="cfm_prompt")(
            u, x, w_dw, b_dw, ln_g, ln_b, w2)


def _cfm_sample_kernel(buf_ref, u_ref, x_ref, wd_ref, bd_ref, lg_ref, lb_ref, w2_ref, o_ref):
    y = u_ref[...] * wd_ref[CONV_C - 1:CONV_C, :]
    for jj in range(CONV_C - 1):
        y = y + buf_ref[jj] * wd_ref[jj:jj + 1, :]
    o_ref[...] = _cfm_tail(y, bd_ref, lg_ref, lb_ref, w2_ref, x_ref[...])


def _cfm_sample(buf_t, u, x, w_dw, b_dw, ln_g, ln_b, w2):
    args = (buf_t, u, x, w_dw, b_dw, ln_g, ln_b, w2)
    return pl.pallas_call(
        _cfm_sample_kernel, grid=(1,), in_specs=[_const_spec(a.shape) for a in args],
        out_specs=_const_spec(x.shape), out_shape=jax.ShapeDtypeStruct(x.shape, F32),
        compiler_params=_cparams("arbitrary"), name="cfm_sample")(*args)


def _rows8(row):
    return jnp.broadcast_to(row, (8, row.shape[1]))


def _gdn_sample_kernel(f_ref, cw_ref, sm_ref, al_ref, dt_ref, z_ref, gn_ref, s_ref, oa_ref, so_ref):
    y = _silu(jnp.sum(f_ref[0] * cw_ref[...], axis=0, keepdims=True))
    sm = sm_ref[0]
    g = -jnp.exp(al_ref[...]) * _softplus(sm + dt_ref[...])
    beta = jax.nn.sigmoid(sm)
    row_id = _iota((8, DK_A), 0)
    outs = []
    for h in range(H_A):
        q = y[:, h * DK_A:(h + 1) * DK_A]
        k = y[:, W_QK_A + h * DK_A:W_QK_A + (h + 1) * DK_A]
        v = y[:, 2 * W_QK_A + h * DV_A:2 * W_QK_A + (h + 1) * DV_A]
        q = q * lax.rsqrt(jnp.sum(q * q, axis=1, keepdims=True) + EPS) * (DK_A ** -0.5)
        k = k * lax.rsqrt(jnp.sum(k * k, axis=1, keepdims=True) + EPS)
        eg = jnp.exp(g[:, SM_A + h:SM_A + h + 1])
        bh = beta[:, SM_B + h:SM_B + h + 1]
        s = s_ref[0, h]
        lhs = jnp.where(row_id == 0, _rows8(k), jnp.where(row_id == 1, _rows8(q), 0.0)).astype(BF16)
        rs = jnp.dot(lhs, s.astype(BF16), preferred_element_type=F32)
        v_new = bh * (v - eg * rs[0:1, :])
        o = eg * rs[1:2, :] + jnp.sum(q * k, axis=1, keepdims=True) * v_new
        so_ref[0, h] = s * eg + _col_from_row(k, DK_A) * v_new
        outs.append(o * lax.rsqrt(jnp.mean(o * o, axis=1, keepdims=True) + EPS) * gn_ref[...])
    oa_ref[0] = jnp.concatenate(outs, axis=1) * _silu(z_ref[0])


def _gdn_sample(full, conv_w, small, alog_row, dt_row, z, gnorm, state):
    db = full.shape[0]
    r3 = lambda i: (i, 0, 0)
    r4 = lambda i: (i, 0, 0, 0)
    return pl.pallas_call(
        _gdn_sample_kernel, grid=(db,),
        in_specs=[pl.BlockSpec((1,) + full.shape[1:], r3), _const_spec(conv_w.shape),
                  pl.BlockSpec((1, 1, LANES), r3), _const_spec(alog_row.shape), _const_spec(dt_row.shape),
                  pl.BlockSpec((1, 1, W_V_A), r3), _const_spec(gnorm.shape),
                  pl.BlockSpec((1,) + state.shape[1:], r4)],
        out_specs=[pl.BlockSpec((1, 1, W_V_A), r3), pl.BlockSpec((1,) + state.shape[1:], r4)],
        out_shape=[jax.ShapeDtypeStruct((db, 1, W_V_A), F32), jax.ShapeDtypeStruct(state.shape, F32)],
        compiler_params=_cparams("parallel"), name="gdn_sample")(
            full, conv_w, small, alog_row, dt_row, z, gnorm, state)


def _xattn_sample_kernel(q_ref, mk_ref, mv_ref, o_ref):
    n_rows = mk_ref.shape[0]
    n_mem = n_rows // 8
    dh = 2 * LANES
    q8 = q_ref[0] * (dh ** -0.5)
    red = jnp.sum(mk_ref[...].reshape(n_mem, 8, LANES) * q8[None], axis=-1, keepdims=True)
    own_lane = _iota((LANES, 8, LANES), 0) == _iota((LANES, 8, LANES), 2)
    n_t = n_mem // LANES
    tiles = [jnp.sum(jnp.where(own_lane, red[j * LANES:(j + 1) * LANES], 0.0), axis=0) for j in range(n_t)]
    s8 = jnp.concatenate(tiles, axis=1)
    s = s8[0:H_X] + s8[H_X:2 * H_X]
    p = jnp.exp(s - jnp.max(s, axis=-1, keepdims=True))
    inv_l = 1.0 / jnp.sum(p, axis=-1, keepdims=True)
    p8 = jnp.concatenate([p, p], axis=0)
    ones = jnp.ones((LANES, LANES), BF16)
    acc = None
    for j in range(n_t):
        z = jnp.where(own_lane, p8[None, :, j * LANES:(j + 1) * LANES], 0.0).reshape(LANES * 8, LANES)
        p3 = jnp.dot(z.astype(BF16), ones, preferred_element_type=F32).reshape(LANES, 8, LANES)
        v3 = mv_ref[j * LANES * 8:(j + 1) * LANES * 8, :].reshape(LANES, 8, LANES)
        part = jnp.sum(p3 * v3, axis=0)
        acc = part if acc is None else acc + part
    o_ref[0] = acc * jnp.concatenate([inv_l, inv_l], axis=0)


def _mem_rows(cache):
    dp, db, nm = cache.shape[:3]
    x = cache.reshape(dp, db, nm, H_X, 2, LANES)
    return jnp.transpose(x, (0, 1, 2, 4, 3, 5)).reshape(dp, db, nm * 8, LANES)


def _xattn_sample(q, mk_rows, mv_rows, layer):
    db, d = q.shape
    n_rows = mk_rows.shape[2]
    q8 = jnp.transpose(q.reshape(db, H_X, 2, LANES), (0, 2, 1, 3)).reshape(db, 8, LANES)
    r3 = lambda i: (i, 0, 0)
    mem = lambda i: (layer, i, 0, 0)
    o8 = pl.pallas_call(
        _xattn_sample_kernel, grid=(db,),
        in_specs=[pl.BlockSpec((1, 8, LANES), r3), pl.BlockSpec((None, None, n_rows, LANES), mem),
                  pl.BlockSpec((None, None, n_rows, LANES), mem)],
        out_specs=pl.BlockSpec((1, 8, LANES), r3), out_shape=jax.ShapeDtypeStruct((db, 8, LANES), F32),
        compiler_params=_cparams("parallel"), name="xattn_sample")(q8, mk_rows, mv_rows)
    return jnp.transpose(o8.reshape(db, 2, H_X, LANES), (0, 2, 1, 3)).reshape(db, d)


def _head_rows(h):
    return slice(h * DH_B, (h + 1) * DH_B)


def _fox_sample_kernel(pt_ref, q_ref, kn_ref, vn_ref, f_ref, bf_ref, *refs, g_pages):
    k_refs = refs[0:g_pages]
    v_refs = refs[g_pages:2 * g_pages]
    lf_refs = refs[2 * g_pages:3 * g_pages]
    o_ref, lfn_ref, qb_ref, m_ref, l_ref, acc_ref, carry_ref = refs[3 * g_pages:]
    del pt_ref
    w = H_B * DH_B
    gi = pl.program_id(1)
    bcast = lambda col: jnp.broadcast_to(col, (H_B, LANES))
    head_id = _iota((H_B, LANES), 0)

    def rows_to_tile(rows):
        n = rows[0].shape[1]
        out = jnp.broadcast_to(rows[0], (H_B, n))
        for h in range(1, H_B):
            out = jnp.where(head_id[:, 0:n] == h, rows[h], out)
        return out

    @pl.when(gi == 0)
    def _():
        q = q_ref[0] * (DH_B ** -0.5)
        qb_ref[...] = jnp.broadcast_to(_col_from_row(q, w), (w, LANES))
        lfn = _log_sigmoid(f_ref[0] + bf_ref[...])
        lfn_ref[0] = lfn
        qk = q * kn_ref[0]
        s_new = rows_to_tile([jnp.sum(qk[:, _head_rows(h)], axis=1, keepdims=True) for h in range(H_B)])
        m_ref[...] = bcast(s_new)
        l_ref[...] = jnp.ones_like(l_ref)
        acc_ref[...] = jnp.where(_iota((w, LANES), 1) == 0, _col_from_row(vn_ref[0], w), 0.0)
        carry_ref[...] = bcast(lfn)

    later = (_iota((PAGE, PAGE), 0) > _iota((PAGE, PAGE), 1)).astype(BF16)
    m = m_ref[:, 0:1]
    l = l_ref[:, 0:1]
    carry = carry_ref[:, 0:1]
    logits = [None] * g_pages
    top = None
    suffix = _dot3_r(jnp.concatenate([lf_refs[jj][...] for jj in range(g_pages)], axis=0), later)
    for jj in reversed(range(g_pages)):
        s_t = rows_to_tile([jnp.sum(k_refs[jj][h] * qb_ref[_head_rows(h), :], axis=0, keepdims=True)
                            for h in range(H_B)])
        lf_t = lf_refs[jj][...]
        logits[jj] = s_t + suffix[jj * H_B:(jj + 1) * H_B, :] + carry
        carry = carry + jnp.sum(lf_t, axis=1, keepdims=True)
        top = logits[jj] if top is None else jnp.maximum(top, logits[jj])
    m_new = jnp.maximum(m, jnp.max(top, axis=1, keepdims=True))
    alpha = jnp.exp(m - m_new)
    probs = [jnp.exp(lg - m_new) for lg in logits]
    p_sum = probs[0]
    for p_t in probs[1:]:
        p_sum = p_sum + p_t
    for h in range(H_B):
        acc_h = acc_ref[_head_rows(h), :] * alpha[h:h + 1, :]
        for jj in range(g_pages):
            acc_h = acc_h + v_refs[jj][h] * probs[jj][h:h + 1, :]
        acc_ref[_head_rows(h), :] = acc_h
    l = alpha * l + jnp.sum(p_sum, axis=1, keepdims=True)
    m_ref[...] = bcast(m_new)
    l_ref[...] = bcast(l)
    carry_ref[...] = bcast(carry)

    @pl.when(gi == pl.num_programs(1) - 1)
    def _():
        inv = 1.0 / l
        den = jnp.concatenate([jnp.broadcast_to(inv[h:h + 1, :], (DH_B, 1)) for h in range(H_B)], axis=0)
        col = jnp.sum(acc_ref[...], axis=1, keepdims=True) * den
        eye = _iota((w, w), 0) == _iota((w, w), 1)
        o_ref[0] = jnp.sum(jnp.where(eye, jnp.broadcast_to(col, (w, w)), 0.0), axis=0, keepdims=True)


def _fox_sample(q, k_new, v_new, f_col, bf_col, cache_kt, cache_vt, cache_lft, page_table, g_pages=16):
    db = q.shape[0]
    w = H_B * DH_B
    n_pages = page_table.shape[1]
    g_pages = min(g_pages, n_pages)
    ng = n_pages // g_pages
    r3 = lambda i, g, pt: (i, 0, 0)

    def page_map(nd):
        return [(lambda i, g, pt, jj=jj: (pt[i * n_pages + (ng - 1 - g) * g_pages + jj],) + (0,) * nd)
                for jj in range(g_pages)]

    in_specs = [pl.BlockSpec((1, 1, w), r3)] * 3 + [pl.BlockSpec((1, H_B, 1), r3),
                                                    pl.BlockSpec(bf_col.shape, lambda i, g, pt: (0, 0))]
    in_specs += [pl.BlockSpec((None, H_B, DH_B, PAGE), mp) for mp in page_map(3)] * 2
    in_specs += [pl.BlockSpec((None, H_B, PAGE), mp) for mp in page_map(2)]
    grid_spec = pltpu.PrefetchScalarGridSpec(
        num_scalar_prefetch=1, grid=(db, ng), in_specs=in_specs,
        out_specs=[pl.BlockSpec((1, 1, w), r3), pl.BlockSpec((1, H_B, 1), r3)],
        scratch_shapes=[pltpu.VMEM((w, LANES), F32), pltpu.VMEM((H_B, LANES), F32), pltpu.VMEM((H_B, LANES), F32),
                        pltpu.VMEM((w, LANES), F32), pltpu.VMEM((H_B, LANES), F32)])
    kern = functools.partial(_fox_sample_kernel, g_pages=g_pages)
    return pl.pallas_call(
        kern, grid_spec=grid_spec,
        out_shape=[jax.ShapeDtypeStruct((db, 1, w), F32), jax.ShapeDtypeStruct((db, H_B, 1), F32)],
        compiler_params=_cparams("parallel", "arbitrary"), name="fox_sample")(
            page_table.reshape(-1), q, k_new, v_new, f_col, bf_col,
            *([cache_kt] * g_pages), *([cache_vt] * g_pages), *([cache_lft] * g_pages))


def _row(v):
    return v.reshape(1, -1).astype(F32)


def _pad_lanes(v, start):
    return jnp.zeros((1, LANES), F32).at[0, start:start + v.shape[0]].set(v)


def _chunk_rows(cols, b, l):
    n = l // CHUNK_A
    return cols.reshape(b, n, CHUNK_A, H_A).transpose(0, 1, 3, 2).reshape(b, n, H_A * CHUNK_A)


def kernel(x_prompt, x_sample, mem_prompt, cache_fox_k, cache_fox_v, cache_fox_logf, page_table, state_gdn, state_gdn_conv, state_cfm_conv, cache_mem_k, cache_mem_v, norm_mix, w_in_e, conv_a, a_log, dt_bias, gnorm_a, b_f, w_out_e, w_pw1, b_pw1, w_dw, b_dw, ln_g, ln_b, w_pw2, norm_mem, norm_x, w_xq, w_xkv, w_xo, norm_f, w_up, w_down, norm_out):
    b, l, d = x_prompt.shape
    db = x_sample.shape[0]
    n_mem = mem_prompt.shape[1]
    depth = norm_mix.shape[0]
    dh_x = d // H_X
    bf = lambda w: w.astype(BF16)

    w_in = w_in_e[0]
    off_aa = W_CONV_A + W_V_A
    off_qb = off_aa + 2 * H_A
    off_fb = off_qb + 3 * W_B
    w_small = jnp.concatenate([w_in[:, off_aa:off_qb], w_in[:, off_fb:]], axis=1)
    w_small = jnp.pad(w_small, ((0, 0), (0, LANES - w_small.shape[1])))
    w_in_r = bf(jnp.concatenate([w_in[:, :off_aa], w_in[:, off_qb:off_fb], w_small], axis=1))
    in_widths = (W_CONV_A, W_V_A, W_B, W_B, W_B, LANES)
    off_kb = off_qb + W_B
    off_vb = off_kb + W_B
    w_in_p = bf(jnp.concatenate([w_in[:, :off_aa], w_in[:, off_qb:off_kb], w_in[:, off_vb:off_fb], w_small], axis=1))
    in_widths_p = (W_CONV_A, W_V_A, W_B, W_B, LANES)
    w_in_t = [bf(w_in[:, off_kb:off_vb].T), bf(w_in[:, off_vb:off_fb].T), bf(w_small.T)]
    w_out_a = bf(w_out_e[0][:W_V_A])
    w_out_b = bf(w_out_e[0][W_V_A:])
    w_xq_b = [bf(w_xq[i]) for i in range(depth)]
    w_xo_b = [bf(w_xo[i]) for i in range(depth)]
    w_up_b = [bf(w_up[i]) for i in range(depth)]
    w_down_b = [bf(w_down[i]) for i in range(depth)]
    w_pw1_b = bf(w_pw1[0])
    w_pw2_b = bf(w_pw2[0])
    alog_rows = _row(jnp.repeat(a_log[0], CHUNK_A))
    dt_rows = _row(jnp.repeat(dt_bias[0], CHUNK_A))
    gn = _row(gnorm_a[0])

    memf = mem_prompt.reshape(b * n_mem, d)
    mem_k, mem_v = [], []
    for i in range(depth):
        mk, mv = _linear([memf], [bf(w_xkv[i])], gain=_row(norm_mem[i]), out_widths=(d, d), name="mem_kv")
        mem_k.append(mk.reshape(b, n_mem, d))
        mem_v.append(mv.reshape(b, n_mem, d))
    mem_k_prompt = jnp.stack(mem_k).reshape(depth, b, n_mem, H_X, dh_x)
    mem_v_prompt = jnp.stack(mem_v).reshape(depth, b, n_mem, H_X, dh_x)

    xp = x_prompt.reshape(b * l, d)
    conv_in, z, q, v, small, k_t, v_t, small_t = _linear(
        [xp], [w_in_p], gain=_row(norm_mix[0]), out_widths=in_widths_p, wts=w_in_t, t_batch=(b, l), name="in_proj")
    fox_k_prompt = jnp.transpose(k_t.reshape(b, H_B, DH_B, l), (0, 3, 1, 2))[None]
    fox_v_prompt = jnp.transpose(v_t.reshape(b, H_B, DH_B, l), (0, 3, 1, 2))[None]
    conv_in3 = conv_in.reshape(b, l, W_CONV_A)
    gdn_conv_prompt = conv_in3[:, l - (CONV_A - 1):, :][None]
    oa, s_fin = _gdn_prompt(conv_in3, z.reshape(b, l, W_V_A),
                            _chunk_rows(small[:, SM_A:SM_A + H_A], b, l),
                            _chunk_rows(small[:, SM_B:SM_B + H_A], b, l),
                            conv_a[0], alog_rows, dt_rows, gn)
    gdn_state_prompt = s_fin.reshape(b, DK_A, H_A, DV_A).transpose(0, 2, 1, 3)[None]
    qf, kft, vf, lft = _fox_prep(q.reshape(b, l, W_B), k_t, v.reshape(b, l, W_B), small.reshape(b, l, LANES),
                                 small_t, _pad_lanes(b_f[0], SM_F), b_f[0].reshape(H_B, 1))
    fox_logf_prompt = jnp.transpose(lft, (0, 2, 1))[None]
    ob = _fox_flash(qf, kft, vf)
    x = _linear([oa.reshape(b * l, W_V_A), ob.reshape(b * l, W_B)], [w_out_a, w_out_b], res=xp, name="out_proj")
    x = _xattn_prompt(x.reshape(b, l, d), _row(norm_x[0]), w_xq_b[0], bf(mem_k[0]), bf(mem_v[0]), w_xo_b[0])
    x = _mlp(x.reshape(b * l, d), _row(norm_f[0]), w_up_b[0], w_down_b[0])
    glu = _linear([x], [w_pw1_b], gain=_row(norm_mix[1]), bias=_row(b_pw1[0]), act="glu", name="pw1_glu")
    glu3 = glu.reshape(b, l, d)
    cfm_conv_prompt = glu3[:, l - (CONV_C - 1):, :][None]
    x = _cfm_prompt(glu3, x.reshape(b, l, d), w_dw[0], _row(b_dw[0]), _row(ln_g[0]), _row(ln_b[0]), w_pw2_b)
    x = _xattn_prompt(x, _row(norm_x[1]), w_xq_b[1], bf(mem_k[1]), bf(mem_v[1]), w_xo_b[1])
    y_prompt = _mlp(x.reshape(b * l, d), _row(norm_f[1]), w_up_b[1], w_down_b[1],
                    final_gain=_row(norm_out)).reshape(b, l, d)

    xs = x_sample.reshape(db, d)
    conv_s, z_s, q_s, k_s, v_s, small_s = _linear([xs], [w_in_r], gain=_row(norm_mix[0]),
                                                  out_widths=in_widths, name="in_proj_s")
    fox_k_sample = k_s.reshape(1, db, 1, H_B, DH_B)
    fox_v_sample = v_s.reshape(1, db, 1, H_B, DH_B)
    full = jnp.concatenate([state_gdn_conv[0], conv_s[:, None, :]], axis=1)
    gdn_conv_sample = full[:, 1:, :][None]
    oa_s, s_new = _gdn_sample(full, conv_a[0], small_s.reshape(db, 1, LANES), _pad_lanes(a_log[0], SM_A),
                              _pad_lanes(dt_bias[0], SM_A), z_s.reshape(db, 1, W_V_A), gn, state_gdn[0])
    gdn_state_sample = s_new[None]
    ob_s, lf_s = _fox_sample(q_s.reshape(db, 1, W_B), k_s.reshape(db, 1, W_B), v_s.reshape(db, 1, W_B),
                             small_s[:, SM_F:SM_F + H_B].reshape(db, H_B, 1), b_f[0].reshape(H_B, 1),
                             jnp.transpose(cache_fox_k[0], (0, 2, 3, 1)), jnp.transpose(cache_fox_v[0], (0, 2, 3, 1)),
                             jnp.transpose(cache_fox_logf[0], (0, 2, 1)), page_table)
    fox_logf_sample = lf_s.reshape(1, db, 1, H_B)
    x = _linear([oa_s.reshape(db, W_V_A), ob_s.reshape(db, W_B)], [w_out_a, w_out_b], res=xs, name="out_proj_s")

    mk_rows = _mem_rows(cache_mem_k)
    mv_rows = _mem_rows(cache_mem_v)

    def xattn_s(x, i):
        qx = _linear([x], [w_xq_b[i]], gain=_row(norm_x[i]), name="xq_s")
        o = _xattn_sample(qx, mk_rows, mv_rows, i)
        return _linear([o], [w_xo_b[i]], res=x, name="xo_s")

    x = xattn_s(x, 0)
    x = _mlp(x, _row(norm_f[0]), w_up_b[0], w_down_b[0])
    glu_s = _linear([x], [w_pw1_b], gain=_row(norm_mix[1]), bias=_row(b_pw1[0]), act="glu", name="pw1_glu_s")
    cfm_conv_sample = jnp.concatenate([state_cfm_conv[0][:, 1:, :], glu_s[:, None, :]], axis=1)[None]
    x = _cfm_sample(state_cfm_conv[0].transpose(1, 0, 2), glu_s, x, w_dw[0], _row(b_dw[0]), _row(ln_g[0]),
                    _row(ln_b[0]), w_pw2_b)
    x = xattn_s(x, 1)
    y_sample = _mlp(x, _row(norm_f[1]), w_up_b[1], w_down_b[1], final_gain=_row(norm_out)).reshape(db, 1, d)

    return (y_prompt, y_sample, fox_k_prompt, fox_v_prompt, fox_logf_prompt, fox_k_sample, fox_v_sample,
            fox_logf_sample, gdn_state_prompt, gdn_conv_prompt, gdn_state_sample, gdn_conv_sample,
            cfm_conv_prompt, cfm_conv_sample, mem_k_prompt, mem_v_prompt)
```

```python
import functools
import math

import jax
import jax.numpy as jnp
import numpy as np
from jax import lax
from jax.experimental import pallas as pl
from jax.experimental.pallas import tpu as pltpu

F32 = jnp.float32
BF16 = jnp.bfloat16
EPS = 1e-6
NEG = -1e30
LOG2E = 1.4426950408889634

H_A = 4
DK_A = 128
DV_A = 128
CONV_A = 4
CHUNK_A = 64
H_B = 8
DH_B = 64
H_X = 4
CONV_C = 31
PAGE = 128

W_QK_A = H_A * DK_A
W_V_A = H_A * DV_A
W_CONV_A = 2 * W_QK_A + W_V_A
W_B = H_B * DH_B
LANES = 128
SM_A = 0
SM_B = H_A
SM_F = 2 * H_A

VMEM_LIMIT = 56 * 1024 * 1024


def _cparams(*sem):
    return pltpu.CompilerParams(dimension_semantics=sem, vmem_limit_bytes=VMEM_LIMIT)


def _const_spec(shape):
    nd = len(shape)
    return pl.BlockSpec(shape, lambda *_: (0,) * nd)


def _rms(x, g):
    return x * lax.rsqrt(jnp.mean(x * x, axis=-1, keepdims=True) + EPS) * g


def _silu(x):
    return x * jax.nn.sigmoid(x)


def _softplus(x):
    return jnp.maximum(x, 0.0) + jnp.log1p(jnp.exp(-jnp.abs(x)))


def _split3(x):
    h = x.astype(BF16)
    r = x - h.astype(F32)
    m = r.astype(BF16)
    l = (r - m.astype(F32)).astype(BF16)
    return h, m, l


def _dot3_l(mat01, x):
    h, m, l = _split3(x)
    d = lambda t: jnp.dot(mat01, t, preferred_element_type=F32)
    return d(h) + d(m) + d(l)


def _dot3_r(x, mat01):
    h, m, l = _split3(x)
    d = lambda t: jnp.dot(t, mat01, preferred_element_type=F32)
    return d(h) + d(m) + d(l)


def _dot_nt(a, b):
    return lax.dot_general(a, b, (((1,), (1,)), ((), ())), preferred_element_type=F32)


def _dot_tn(a, b):
    return lax.dot_general(a, b, (((0,), (0,)), ((), ())), preferred_element_type=F32)


def _iota(shape, dim):
    return lax.broadcasted_iota(jnp.int32, shape, dim)


def _col_from_row(row, n):
    eye = _iota((n, n), 0) == _iota((n, n), 1)
    return jnp.sum(jnp.where(eye, jnp.broadcast_to(row, (n, n)), 0.0), axis=1, keepdims=True)


def _linear_kernel(*refs, n_in, n_t, has_gain, has_bias, has_res, act, out_widths, chunk):
    it = iter(refs)
    x_refs = [next(it) for _ in range(n_in)]
    w_refs = [next(it) for _ in range(n_in)]
    wt_refs = [next(it) for _ in range(n_t)]
    gain_ref = next(it) if has_gain else None
    bias_ref = next(it) if has_bias else None
    res_ref = next(it) if has_res else None
    out_refs = [next(it) for _ in out_widths]
    outt_refs = [next(it) for _ in range(n_t)]
    xs = []
    for i, xr in enumerate(x_refs):
        x = xr[...]
        if has_gain and i == 0:
            x = _rms(x, gain_ref[...])
        xs.append(x.astype(BF16))
    n_total = sum(out_widths)

    def mm(col0, cw):
        acc = None
        for xb, wr in zip(xs, w_refs):
            d = jnp.dot(xb, wr[:, col0:col0 + cw], preferred_element_type=F32)
            acc = d if acc is None else acc + d
        if has_bias:
            acc = acc + bias_ref[:, col0:col0 + cw]
        return acc

    off = 0
    for o_ref, width in zip(out_refs, out_widths):
        for c0 in range(0, width, chunk):
            cw = min(chunk, width - c0)
            y = mm(off + c0, cw)
            if act == "glu":
                y = y * jax.nn.sigmoid(mm(n_total + off + c0, cw))
            if has_res:
                y = y + res_ref[:, off + c0:off + c0 + cw]
            o_ref[:, c0:c0 + cw] = y
        off += width
    for wt_ref, ot_ref in zip(wt_refs, outt_refs):
        ot_ref[0] = _dot_nt(wt_ref[...], xs[0])


def _linear(xs, ws, *, gain=None, bias=None, res=None, act=None, out_widths=None, wts=(), t_batch=None,
            tm=512, chunk=512, name="linear"):
    m = xs[0].shape[0]
    tm = min(tm, m)
    assert m % tm == 0
    n_mm = ws[0].shape[1]
    n_out = n_mm // 2 if act == "glu" else n_mm
    if out_widths is None:
        out_widths = (n_out,)
    assert sum(out_widths) == n_out
    row = lambda i: (i, 0)
    in_specs = [pl.BlockSpec((tm, x.shape[1]), row) for x in xs]
    in_specs += [_const_spec(w.shape) for w in ws]
    in_specs += [_const_spec(w.shape) for w in wts]
    args = list(xs) + list(ws) + list(wts)
    if gain is not None:
        in_specs.append(_const_spec(gain.shape)); args.append(gain)
    if bias is not None:
        in_specs.append(_const_spec(bias.shape)); args.append(bias)
    if res is not None:
        in_specs.append(pl.BlockSpec((tm, res.shape[1]), row)); args.append(res)
    out_shape = [jax.ShapeDtypeStruct((m, w), F32) for w in out_widths]
    out_specs = [pl.BlockSpec((tm, w), row) for w in out_widths]
    if wts:
        tb, tl = t_batch
        assert tb * tl == m and tl % tm == 0
        per = tl // tm
        out_shape += [jax.ShapeDtypeStruct((tb, w.shape[0], tl), F32) for w in wts]
        out_specs += [pl.BlockSpec((1, w.shape[0], tm), lambda i: (i // per, 0, i % per)) for w in wts]
    kern = functools.partial(_linear_kernel, n_in=len(xs), n_t=len(wts), has_gain=gain is not None,
                             has_bias=bias is not None, has_res=res is not None, act=act,
                             out_widths=tuple(out_widths), chunk=chunk)
    outs = pl.pallas_call(kern, grid=(m // tm,), in_specs=in_specs, out_specs=out_specs,
                          out_shape=out_shape, compiler_params=_cparams("parallel"), name=name)(*args)
    return outs if len(outs) > 1 else outs[0]


def _mlp_kernel(*refs, chunk, final_norm):
    if final_norm:
        x_ref, g_ref, wu_ref, wd_ref, go_ref, o_ref = refs
    else:
        x_ref, g_ref, wu_ref, wd_ref, o_ref = refs
    x = x_ref[...]
    xn = _rms(x, g_ref[...]).astype(BF16)
    acc = x
    d_ff = wu_ref.shape[1]
    for c0 in range(0, d_ff, chunk):
        h = jnp.dot(xn, wu_ref[:, c0:c0 + chunk], preferred_element_type=F32)
        h = jnp.square(jnp.maximum(h, 0.0)).astype(BF16)
        acc = acc + jnp.dot(h, wd_ref[c0:c0 + chunk, :], preferred_element_type=F32)
    if final_norm:
        acc = _rms(acc, go_ref[...])
    o_ref[...] = acc


def _mlp(x, gain, w_up, w_down, final_gain=None, tm=512, chunk=512):
    m, d = x.shape
    tm = min(tm, m)
    row = lambda i: (i, 0)
    in_specs = [pl.BlockSpec((tm, d), row), _const_spec(gain.shape), _const_spec(w_up.shape),
                _const_spec(w_down.shape)]
    args = [x, gain, w_up, w_down]
    if final_gain is not None:
        in_specs.append(_const_spec(final_gain.shape)); args.append(final_gain)
    kern = functools.partial(_mlp_kernel, chunk=chunk, final_norm=final_gain is not None)
    return pl.pallas_call(kern, grid=(m // tm,), in_specs=in_specs,
                          out_specs=pl.BlockSpec((tm, d), row),
                          out_shape=jax.ShapeDtypeStruct((m, d), F32),
                          compiler_params=_cparams("parallel"), name="mlp")(*args)


def _xattn_kernel(x_ref, g_ref, wq_ref, mk_ref, mv_ref, wo_ref, o_ref):
    x = x_ref[0]
    d = x.shape[1]
    dh = d // H_X
    xn = _rms(x, g_ref[...]).astype(BF16)
    q = jnp.dot(xn, wq_ref[...], preferred_element_type=F32) * (dh ** -0.5)
    q = q.astype(BF16)
    head = lambda h: slice(h * dh, (h + 1) * dh)
    scores = lambda h: _dot_nt(q[:, head(h)], mk_ref[0, :, head(h)])
    outs = []
    s_next = scores(0)
    for h in range(H_X):
        sl = head(h)
        s = s_next
        if h + 1 < H_X:
            s_next = scores(h + 1)
        p = jnp.exp(s - jnp.max(s, axis=-1, keepdims=True))
        l = jnp.sum(p, axis=-1, keepdims=True)
        o = jnp.dot(p.astype(BF16), mv_ref[0, :, sl], preferred_element_type=F32)
        outs.append((o * (1.0 / l)).astype(BF16))
    o = jnp.concatenate(outs, axis=1)
    o_ref[0] = x + jnp.dot(o, wo_ref[...], preferred_element_type=F32)


def _xattn_prompt(x, gain, wq, mk, mv, wo, tm=1024):
    b, l, d = x.shape
    tm = min(tm, l)
    nm = mk.shape[1]
    blk = lambda bi, i: (bi, i, 0)
    mem = lambda bi, i: (bi, 0, 0)
    return pl.pallas_call(
        _xattn_kernel, grid=(b, l // tm),
        in_specs=[pl.BlockSpec((1, tm, d), blk), _const_spec(gain.shape), _const_spec(wq.shape),
                  pl.BlockSpec((1, nm, d), mem), pl.BlockSpec((1, nm, d), mem), _const_spec(wo.shape)],
        out_specs=pl.BlockSpec((1, tm, d), blk),
        out_shape=jax.ShapeDtypeStruct((b, l, d), F32),
        compiler_params=_cparams("parallel", "parallel"), name="xattn_prompt")(x, gain, wq, mk, mv, wo)


def _log_sigmoid(x):
    return jnp.minimum(x, 0.0) - jnp.log1p(jnp.exp(-jnp.abs(x)))


def _head_lanes(x, h):
    p = h // 2
    blk = x[:, p * LANES:(p + 1) * LANES]
    if h % 2:
        blk = pltpu.roll(blk, DH_B, 1)
    return blk


def _fox_prep(q, kt, v, small, smt, bf_ref, bfc_ref, first, qf_ref, kft_ref, vf_ref, lft_ref, carry_ref,
              carryt_ref, between=None):
    t = q.shape[0]

    @pl.when(first)
    def _():
        carry_ref[...] = jnp.zeros_like(carry_ref)
        carryt_ref[...] = jnp.zeros_like(carryt_ref)

    lane = _iota((t, LANES), 1)
    gate_lane = (lane >= SM_F) & (lane < SM_F + H_B)
    logf = jnp.where(gate_lane, _log_sigmoid(small + bf_ref[...]), 0.0)
    before = _iota((t, t), 0) >= _iota((t, t), 1)
    c = _dot3_l(before.astype(BF16), logf) + carry_ref[0:1, :]
    carry_ref[0:1, :] = c[t - 1:t, :]
    c1, c2, c3 = _split3(c * LOG2E)
    c1, c2, c3 = c1.astype(F32), c2.astype(F32), c3.astype(F32)
    lft = _log_sigmoid(smt + bfc_ref[...])
    lft_ref[0] = lft
    ct = _dot3_r(lft, (_iota((t, t), 0) <= _iota((t, t), 1)).astype(BF16)) + carryt_ref[:, 0:1]
    carryt_ref[...] = jnp.broadcast_to(ct[:, t - 1:t], carryt_ref.shape)
    r1, r2, r3 = _split3(ct * LOG2E)
    r1, r2, r3 = r1.astype(F32), r2.astype(F32), r3.astype(F32)
    if between is not None:
        between()
    q = q * (DH_B ** -0.5 * LOG2E)
    low = lane < DH_B
    row = _iota((DH_B, t), 0)
    for h in range(H_B):
        col = slice(SM_F + h, SM_F + h + 1)
        b1, b2, b3 = c1[:, col], c2[:, col], c3[:, col]
        qx = jnp.where(lane == DH_B, b1, jnp.where(lane == DH_B + 1, b2, jnp.where(
            lane == DH_B + 2, b3, jnp.where(lane < DH_B + 6, 1.0, 0.0))))
        kx = jnp.where(row < 3, 1.0, jnp.where(row == 3, -r1[h:h + 1, :], jnp.where(
            row == 4, -r2[h:h + 1, :], jnp.where(row == 5, -r3[h:h + 1, :], 0.0))))
        vx = jnp.where(lane == DH_B, 1.0, 0.0)
        qf_ref[0, h] = jnp.where(low, _head_lanes(q, h), qx).astype(BF16)
        kft_ref[0, h] = jnp.concatenate([kt[h * DH_B:(h + 1) * DH_B, :], kx], axis=0).astype(BF16)
        vf_ref[0, h] = jnp.where(low, _head_lanes(v, h), vx).astype(BF16)


def _in_proj_prompt_kernel(x_ref, w_ref, wkt_ref, wvt_ref, wft_ref, g_ref, bf_ref, bfc_ref,
                           conv_ref, z_ref, sm_ref, kt_ref, vt_ref, qf_ref, kft_ref, vf_ref, lft_ref,
                           carry_ref, carryt_ref, *, per_seq):
    xn = _rms(x_ref[...], g_ref[...]).astype(BF16)
    mm = lambda c0, cw: jnp.dot(xn, w_ref[:, c0:c0 + cw], preferred_element_type=F32)
    off = W_CONV_A + W_V_A
    q = mm(off, W_B)
    v = mm(off + W_B, W_B)
    small = mm(off + 2 * W_B, LANES)
    sm_ref[...] = small
    kt = _dot_nt(wkt_ref[...], xn)
    kt_ref[0] = kt
    smt = _dot_nt(wft_ref[...], xn)

    def rest():
        vt_ref[0] = _dot_nt(wvt_ref[...], xn)
        for c0 in range(0, W_CONV_A, W_B):
            conv_ref[:, c0:c0 + W_B] = mm(c0, W_B)
        z_ref[...] = mm(W_CONV_A, W_V_A)

    _fox_prep(q, kt, v, small, smt, bf_ref, bfc_ref, pl.program_id(0) % per_seq == 0,
              qf_ref, kft_ref, vf_ref, lft_ref, carry_ref, carryt_ref, between=rest)


def _in_proj_prompt(x, w, wkt, wvt, wft, gain, bf_row, bf_col, b, l, tm=512):
    m, d = x.shape
    tm = min(tm, l)
    per = l // tm
    row = lambda i: (i, 0)
    tblk = lambda i: (i // per, 0, i % per)
    hblk = lambda i: (i // per, 0, i % per, 0)
    hshape = jax.ShapeDtypeStruct((b, H_B, l, LANES), BF16)
    consts = (w, wkt, wvt, wft, gain, bf_row, bf_col)
    kern = functools.partial(_in_proj_prompt_kernel, per_seq=per)
    return pl.pallas_call(
        kern, grid=(m // tm,),
        in_specs=[pl.BlockSpec((tm, d), row)] + [_const_spec(a.shape) for a in consts],
        out_specs=[pl.BlockSpec((tm, W_CONV_A), row), pl.BlockSpec((tm, W_V_A), row), pl.BlockSpec((tm, LANES), row),
                   pl.BlockSpec((1, W_B, tm), tblk), pl.BlockSpec((1, W_B, tm), tblk),
                   pl.BlockSpec((1, H_B, tm, LANES), hblk),
                   pl.BlockSpec((1, H_B, LANES, tm), lambda i: (i // per, 0, 0, i % per)),
                   pl.BlockSpec((1, H_B, tm, LANES), hblk), pl.BlockSpec((1, H_B, tm), tblk)],
        out_shape=[jax.ShapeDtypeStruct((m, W_CONV_A), F32), jax.ShapeDtypeStruct((m, W_V_A), F32),
                   jax.ShapeDtypeStruct((m, LANES), F32), jax.ShapeDtypeStruct((b, W_B, l), F32),
                   jax.ShapeDtypeStruct((b, W_B, l), F32), hshape,
                   jax.ShapeDtypeStruct((b, H_B, LANES, l), BF16), hshape, jax.ShapeDtypeStruct((b, H_B, l), F32)],
        scratch_shapes=[pltpu.VMEM((8, LANES), F32), pltpu.VMEM((H_B, LANES), F32)],
        compiler_params=_cparams("arbitrary"), name="in_proj")(x, *consts)


LOOKAHEAD = 2


def _fox_flash_kernel(it_ref, jt_ref, fl_ref, qf_ref, kft_ref, vf_ref, o_ref, m_ref, acc_ref):
    p_id = pl.program_id(1)
    i = it_ref[p_id]
    j = jt_ref[p_id]
    flags = fl_ref[p_id]
    tq = qf_ref.shape[2]
    tk = kft_ref.shape[3]

    @pl.when(j == 0)
    def _():
        m_ref[...] = jnp.full_like(m_ref, NEG)
        acc_ref[...] = jnp.zeros_like(acc_ref)

    def heads(masked, row0=0):
        nr = tq - row0
        rs = slice(row0, tq)
        if masked:
            visible = _iota((nr, tk), 1) + j * tk <= _iota((nr, tk), 0) + (i * tq + row0)

        def scores(h):
            s = jnp.dot(qf_ref[0, h, rs, :], kft_ref[0, h], preferred_element_type=F32)
            return jnp.where(visible, s, NEG) if masked else s

        pending = [scores(h) for h in range(LOOKAHEAD)]
        for h in range(H_B):
            s = pending.pop(0)
            if h + LOOKAHEAD < H_B:
                pending.append(scores(h + LOOKAHEAD))
            m_prev = m_ref[h, rs, :]
            m_new = jnp.maximum(m_prev, jnp.max(s, axis=-1, keepdims=True))
            alpha = jnp.exp2(m_prev - m_new)
            p = jnp.exp2(s - jnp.concatenate([m_new] * (tk // LANES), axis=1)).astype(BF16)
            acc_ref[h, rs, :] = alpha * acc_ref[h, rs, :] + jnp.dot(p, vf_ref[0, h], preferred_element_type=F32)
            m_ref[h, rs, :] = m_new

    @pl.when((flags & 5) == 0)
    def _():
        heads(False)

    @pl.when((flags & 5) == 1)
    def _():
        heads(True)

    @pl.when((flags & 4) == 4)
    def _():
        heads(True, tq // 2)

    @pl.when((flags & 2) == 2)
    def _():
        lane = _iota((tq, LANES), 1)
        for pr in range(H_B // 2):
            a0 = acc_ref[2 * pr]
            a1 = acc_ref[2 * pr + 1]
            o0 = a0 * (1.0 / a0[:, DH_B:DH_B + 1])
            o1 = a1 * (1.0 / a1[:, DH_B:DH_B + 1])
            o_ref[0, :, pr * LANES:(pr + 1) * LANES] = jnp.where(lane < DH_B, o0, pltpu.roll(o1, DH_B, 1))


def _fox_flash(qf, kft, vf, tq=1024, tk=512):
    b, hh, l, _ = qf.shape
    tk = min(tk, l)
    tq = tq if l % tq == 0 else tk
    assert l % tq == 0 and l % tk == 0
    its, jts, fls = [], [], []
    for i in range(l // tq):
        j_last = ((i + 1) * tq - 1) // tk
        for j in range(j_last + 1):
            crosses = (j + 1) * tk - 1 > i * tq
            lower_only = j * tk > i * tq + tq // 2 - 1
            its.append(i)
            jts.append(j)
            fls.append(int(crosses) + 2 * int(j == j_last) + 4 * int(lower_only))
    it, jt, fl = (jnp.asarray(np.array(a, np.int32)) for a in (its, jts, fls))
    qmap = lambda bi, p, it, jt, fl: (bi, 0, it[p], 0)
    kmap = lambda bi, p, it, jt, fl: (bi, 0, jt[p], 0)
    ktmap = lambda bi, p, it, jt, fl: (bi, 0, 0, jt[p])
    omap = lambda bi, p, it, jt, fl: (bi, it[p], 0)
    grid_spec = pltpu.PrefetchScalarGridSpec(
        num_scalar_prefetch=3, grid=(b, len(its)),
        in_specs=[pl.BlockSpec((1, hh, tq, LANES), qmap), pl.BlockSpec((1, hh, LANES, tk), ktmap),
                  pl.BlockSpec((1, hh, tk, LANES), kmap)],
        out_specs=pl.BlockSpec((1, tq, hh * DH_B), omap),
        scratch_shapes=[pltpu.VMEM((hh, tq, LANES), F32), pltpu.VMEM((hh, tq, LANES), F32)])
    return pl.pallas_call(
        _fox_flash_kernel, grid_spec=grid_spec,
        out_shape=jax.ShapeDtypeStruct((b, l, hh * DH_B), F32),
        compiler_params=_cparams("parallel", "arbitrary"), name="fox_flash")(it, jt, fl, qf, kft, vf)


def _gdn_kernel(x_ref, z_ref, a_ref, b_ref, cw_ref, al_ref, dt_ref, gn_ref, oa_ref, s_ref,
                xs_ref, st_ref, *, nc):
    c = CHUNK_A
    r = H_A * c
    t = nc * c
    halo = 8
    nb = x_ref.shape[0]

    @pl.when(pl.program_id(0) == 0)
    def _():
        xs_ref[:, 0:halo, :] = jnp.zeros((nb, halo, xs_ref.shape[2]), F32)
        st_ref[...] = jnp.zeros_like(st_ref)

    @pl.when(pl.program_id(0) != 0)
    def _():
        xs_ref[:, 0:halo, :] = xs_ref[:, t:t + halo, :]

    ri = _iota((r, r), 0)
    ci = _iota((r, r), 1)
    same = (ri // c) == (ci // c)
    incl = same & (ri >= ci)
    strict = same & (ri > ci)
    cum = (same & (ri <= ci)).astype(BF16)
    eye = ri == ci
    last = ((ci % c) == (c - 1)) & same

    def stack(a):
        return jnp.concatenate([a[:, h * DK_A:(h + 1) * DK_A] for h in range(H_A)], axis=0)

    pre = []
    for bi in range(nb):
        xs_ref[bi, halo:halo + t, :] = x_ref[bi]
        y = None
        for jj in range(CONV_A):
            o = halo - (CONV_A - 1) + jj
            term = xs_ref[bi, o:o + t, :] * cw_ref[jj:jj + 1, :]
            y = term if y is None else y + term
        y = _silu(y)
        g_rows = -jnp.exp(al_ref[...]) * _softplus(a_ref[bi, 0] + dt_ref[...])
        beta_rows = jax.nn.sigmoid(b_ref[bi, 0])
        gc_rows = _dot3_r(g_rows, cum)
        for n in range(nc):
            rows = slice(n * c, (n + 1) * c)
            gc_r = gc_rows[n:n + 1, :]
            gcb = jnp.broadcast_to(gc_r, (r, r))
            gc_c = jnp.sum(jnp.where(eye, gcb, 0.0), axis=1, keepdims=True)
            gl_c = jnp.sum(jnp.where(last, gcb, 0.0), axis=1, keepdims=True)
            beta_c = jnp.sum(jnp.where(eye, jnp.broadcast_to(beta_rows[n:n + 1, :], (r, r)), 0.0),
                             axis=1, keepdims=True)
            q = stack(y[rows, 0:W_QK_A])
            k = stack(y[rows, W_QK_A:2 * W_QK_A])
            v = stack(y[rows, 2 * W_QK_A:W_CONV_A])
            q = q * lax.rsqrt(jnp.sum(q * q, axis=1, keepdims=True) + EPS) * (DK_A ** -0.5)
            k = k * lax.rsqrt(jnp.sum(k * k, axis=1, keepdims=True) + EPS)
            decay = jnp.exp(jnp.where(incl, gc_c - gc_r, -jnp.inf))
            kb = k * beta_c
            k16 = k.astype(BF16)
            eg = jnp.exp(gc_c)
            pre.append(dict(decay=decay, kk=_dot_nt(kb.astype(BF16), k16), qk=_dot_nt(q.astype(BF16), k16),
                            rhs=jnp.concatenate([v * beta_c, kb * eg], axis=1).astype(BF16),
                            qg=(q * eg).astype(BF16), kdec=(k * jnp.exp(gl_c - gc_c)).astype(BF16),
                            dec=[jnp.exp(gl_c[h * c:h * c + 1, :]) for h in range(H_A)]))
    for pc in pre:
        pc["a_qk"] = jnp.where(incl, pc["qk"] * pc["decay"], 0.0).astype(BF16)
        pc["mneg"] = jnp.where(strict, -(pc["kk"] * pc["decay"]), 0.0)
        pc["tinv"] = jnp.where(eye, 1.0, 0.0) + pc["mneg"]
    for _ in range(int(math.log2(c)) - 1):
        for pc in pre:
            m16 = pc["mneg"].astype(BF16)
            pc["mneg"] = jnp.dot(m16, m16, preferred_element_type=F32)
        for pc in pre:
            pc["tinv"] = pc["tinv"] + jnp.dot(pc["tinv"].astype(BF16), pc["mneg"].astype(BF16),
                                              preferred_element_type=F32)
    for pc in pre:
        uw = jnp.dot(pc["tinv"].astype(BF16), pc["rhs"], preferred_element_type=F32)
        pc["u"] = uw[:, 0:DV_A]
        pc["w"] = uw[:, DV_A:2 * DV_A].astype(BF16)

    s_heads = [[st_ref[bi, :, h * DV_A:(h + 1) * DV_A] for h in range(H_A)] for bi in range(nb)]
    for n in range(nc):
        rows = slice(n * c, (n + 1) * c)
        for bi in range(nb):
            pc = pre[bi * nc + n]
            v_parts, qs_parts = [], []
            for h in range(H_A):
                hr = slice(h * c, (h + 1) * c)
                wq = jnp.concatenate([pc["w"][hr], pc["qg"][hr]], axis=0)
                wq_s = jnp.dot(wq, s_heads[bi][h].astype(BF16), preferred_element_type=F32)
                v_parts.append(pc["u"][hr] - wq_s[0:c])
                qs_parts.append(wq_s[c:2 * c])
            v_new = jnp.concatenate(v_parts, axis=0).astype(BF16)
            o = jnp.concatenate(qs_parts, axis=0) + jnp.dot(pc["a_qk"], v_new, preferred_element_type=F32)
            for h in range(H_A):
                hr = slice(h * c, (h + 1) * c)
                s_heads[bi][h] = s_heads[bi][h] * pc["dec"][h] + _dot_tn(pc["kdec"][hr], v_new[hr])
            on = o * lax.rsqrt(jnp.mean(o * o, axis=1, keepdims=True) + EPS) * gn_ref[...]
            on = jnp.concatenate([on[h * c:(h + 1) * c, :] for h in range(H_A)], axis=1)
            oa_ref[bi, rows, :] = on * _silu(z_ref[bi, rows, :])
    for bi in range(nb):
        for h in range(H_A):
            st_ref[bi, :, h * DV_A:(h + 1) * DV_A] = s_heads[bi][h]

    @pl.when(pl.program_id(0) == pl.num_programs(0) - 1)
    def _():
        s_ref[...] = st_ref[...]


def _gdn_prompt(conv_in, z, a_rows, b_rows, conv_w, alog_row, dt_row, gnorm, nc=4):
    b, l, wc = conv_in.shape
    n = l // CHUNK_A
    nc = min(nc, n)
    t = nc * CHUNK_A
    r = H_A * CHUNK_A
    blk = lambda i: (0, i, 0)
    gate = lambda i: (0, i, 0, 0)
    a_rows = a_rows.reshape(b, n // nc, nc, r)
    b_rows = b_rows.reshape(b, n // nc, nc, r)
    kern = functools.partial(_gdn_kernel, nc=nc)
    oa, s = pl.pallas_call(
        kern, grid=(n // nc,),
        in_specs=[pl.BlockSpec((b, t, wc), blk), pl.BlockSpec((b, t, W_V_A), blk),
                  pl.BlockSpec((b, 1, nc, r), gate), pl.BlockSpec((b, 1, nc, r), gate),
                  _const_spec(conv_w.shape), _const_spec(alog_row.shape), _const_spec(dt_row.shape),
                  _const_spec(gnorm.shape)],
        out_specs=[pl.BlockSpec((b, t, W_V_A), blk), _const_spec((b, DK_A, H_A * DV_A))],
        out_shape=[jax.ShapeDtypeStruct((b, l, W_V_A), F32),
                   jax.ShapeDtypeStruct((b, DK_A, H_A * DV_A), F32)],
        scratch_shapes=[pltpu.VMEM((b, t + 8, wc), F32), pltpu.VMEM((b, DK_A, H_A * DV_A), F32)],
        compiler_params=_cparams("arbitrary"), name="gdn_prompt")(
            conv_in, z, a_rows, b_rows, conv_w, alog_row, dt_row, gnorm)
    return oa, s


def _cfm_tail(y, bd_ref, lg_ref, lb_ref, w2_ref, res):
    y = y + bd_ref[...]
    yc = y - jnp.mean(y, axis=-1, keepdims=True)
    yn = yc * lax.rsqrt(jnp.mean(yc * yc, axis=-1, keepdims=True) + EPS) * lg_ref[...] + lb_ref[...]
    return res + jnp.dot(_silu(yn).astype(BF16), w2_ref[...], preferred_element_type=F32)


def _cfm_kernel(u_ref, x_ref, wd_ref, bd_ref, lg_ref, lb_ref, w2_ref, o_ref, xs_ref, sh_ref):
    t = u_ref.shape[1]
    halo = 32

    @pl.when(pl.program_id(1) == 0)
    def _():
        xs_ref[0:halo, :] = jnp.zeros((halo, xs_ref.shape[1]), F32)

    @pl.when(pl.program_id(1) != 0)
    def _():
        xs_ref[0:halo, :] = xs_ref[t:t + halo, :]

    xs_ref[halo:halo + t, :] = u_ref[0]
    y = None
    for r in range(8):
        offs = [o for o in range(halo - (CONV_C - 1), halo + 1) if o % 8 == r]
        if not offs:
            continue
        if r:
            n_rows = max(offs) - r + t
            sh_ref[0:n_rows, :] = xs_ref[r:r + n_rows, :]
        win_ref = sh_ref if r else xs_ref
        for o in offs:
            jj = o - (halo - (CONV_C - 1))
            term = win_ref[o - r:o - r + t, :] * wd_ref[jj:jj + 1, :]
            y = term if y is None else y + term
    o_ref[0] = _cfm_tail(y, bd_ref, lg_ref, lb_ref, w2_ref, x_ref[0])


def _cfm_prompt(u, x, w_dw, b_dw, ln_g, ln_b, w2, t=512):
    b, l, d = u.shape
    t = min(t, l)
    blk = lambda bi, i: (bi, i, 0)
    return pl.pallas_call(
        _cfm_kernel, grid=(b, l // t),
        in_specs=[pl.BlockSpec((1, t, d), blk), pl.BlockSpec((1, t, d), blk), _const_spec(w_dw.shape),
                  _const_spec(b_dw.shape), _const_spec(ln_g.shape), _const_spec(ln_b.shape),
                  _const_spec(w2.shape)],
        out_specs=pl.BlockSpec((1, t, d), blk),
        out_shape=jax.ShapeDtypeStruct((b, l, d), F32),
        scratch_shapes=[pltpu.VMEM((t + 32, d), F32), pltpu.VMEM((t + 32, d), F32)],
        compiler_params=_cparams("parallel", "arbitrary"), name="cfm_prompt")(
            u, x, w_dw, b_dw, ln_g, ln_b, w2)


def _cfm_sample_kernel(buf_ref, u_ref, x_ref, wd_ref, bd_ref, lg_ref, lb_ref, w2_ref, o_ref):
    y = u_ref[...] * wd_ref[CONV_C - 1:CONV_C, :]
    for jj in range(CONV_C - 1):
        y = y + buf_ref[jj] * wd_ref[jj:jj + 1, :]
    o_ref[...] = _cfm_tail(y, bd_ref, lg_ref, lb_ref, w2_ref, x_ref[...])


def _cfm_sample(buf_t, u, x, w_dw, b_dw, ln_g, ln_b, w2):
    args = (buf_t, u, x, w_dw, b_dw, ln_g, ln_b, w2)
    return pl.pallas_call(
        _cfm_sample_kernel, grid=(1,), in_specs=[_const_spec(a.shape) for a in args],
        out_specs=_const_spec(x.shape), out_shape=jax.ShapeDtypeStruct(x.shape, F32),
        compiler_params=_cparams("arbitrary"), name="cfm_sample")(*args)


def _rows8(row):
    return jnp.broadcast_to(row, (8, row.shape[1]))


def _gdn_sample_kernel(f_ref, cw_ref, sm_ref, al_ref, dt_ref, z_ref, gn_ref, s_ref, oa_ref, so_ref):
    y = _silu(jnp.sum(f_ref[0] * cw_ref[...], axis=0, keepdims=True))
    sm = sm_ref[0]
    g = -jnp.exp(al_ref[...]) * _softplus(sm + dt_ref[...])
    beta = jax.nn.sigmoid(sm)
    row_id = _iota((8, DK_A), 0)
    outs = []
    for h in range(H_A):
        q = y[:, h * DK_A:(h + 1) * DK_A]
        k = y[:, W_QK_A + h * DK_A:W_QK_A + (h + 1) * DK_A]
        v = y[:, 2 * W_QK_A + h * DV_A:2 * W_QK_A + (h + 1) * DV_A]
        q = q * lax.rsqrt(jnp.sum(q * q, axis=1, keepdims=True) + EPS) * (DK_A ** -0.5)
        k = k * lax.rsqrt(jnp.sum(k * k, axis=1, keepdims=True) + EPS)
        eg = jnp.exp(g[:, SM_A + h:SM_A + h + 1])
        bh = beta[:, SM_B + h:SM_B + h + 1]
        s = s_ref[0, h]
        lhs = jnp.where(row_id == 0, _rows8(k), jnp.where(row_id == 1, _rows8(q), 0.0)).astype(BF16)
        rs = jnp.dot(lhs, s.astype(BF16), preferred_element_type=F32)
        v_new = bh * (v - eg * rs[0:1, :])
        o = eg * rs[1:2, :] + jnp.sum(q * k, axis=1, keepdims=True) * v_new
        so_ref[0, h] = s * eg + _col_from_row(k, DK_A) * v_new
        outs.append(o * lax.rsqrt(jnp.mean(o * o, axis=1, keepdims=True) + EPS) * gn_ref[...])
    oa_ref[0] = jnp.concatenate(outs, axis=1) * _silu(z_ref[0])


def _gdn_sample(full, conv_w, small, alog_row, dt_row, z, gnorm, state):
    db = full.shape[0]
    r3 = lambda i: (i, 0, 0)
    r4 = lambda i: (i, 0, 0, 0)
    return pl.pallas_call(
        _gdn_sample_kernel, grid=(db,),
        in_specs=[pl.BlockSpec((1,) + full.shape[1:], r3), _const_spec(conv_w.shape),
                  pl.BlockSpec((1, 1, LANES), r3), _const_spec(alog_row.shape), _const_spec(dt_row.shape),
                  pl.BlockSpec((1, 1, W_V_A), r3), _const_spec(gnorm.shape),
                  pl.BlockSpec((1,) + state.shape[1:], r4)],
        out_specs=[pl.BlockSpec((1, 1, W_V_A), r3), pl.BlockSpec((1,) + state.shape[1:], r4)],
        out_shape=[jax.ShapeDtypeStruct((db, 1, W_V_A), F32), jax.ShapeDtypeStruct(state.shape, F32)],
        compiler_params=_cparams("parallel"), name="gdn_sample")(
            full, conv_w, small, alog_row, dt_row, z, gnorm, state)


def _xattn_sample_kernel(q_ref, mk_ref, mv_ref, o_ref):
    n_rows = mk_ref.shape[0]
    n_mem = n_rows // 8
    dh = 2 * LANES
    q8 = q_ref[0] * (dh ** -0.5)
    red = jnp.sum(mk_ref[...].reshape(n_mem, 8, LANES) * q8[None], axis=-1, keepdims=True)
    own_lane = _iota((LANES, 8, LANES), 0) == _iota((LANES, 8, LANES), 2)
    n_t = n_mem // LANES
    tiles = [jnp.sum(jnp.where(own_lane, red[j * LANES:(j + 1) * LANES], 0.0), axis=0) for j in range(n_t)]
    s8 = jnp.concatenate(tiles, axis=1)
    s = s8[0:H_X] + s8[H_X:2 * H_X]
    p = jnp.exp(s - jnp.max(s, axis=-1, keepdims=True))
    inv_l = 1.0 / jnp.sum(p, axis=-1, keepdims=True)
    p8 = jnp.concatenate([p, p], axis=0)
    ones = jnp.ones((LANES, LANES), BF16)
    acc = None
    for j in range(n_t):
        z = jnp.where(own_lane, p8[None, :, j * LANES:(j + 1) * LANES], 0.0).reshape(LANES * 8, LANES)
        p3 = jnp.dot(z.astype(BF16), ones, preferred_element_type=F32).reshape(LANES, 8, LANES)
        v3 = mv_ref[j * LANES * 8:(j + 1) * LANES * 8, :].reshape(LANES, 8, LANES)
        part = jnp.sum(p3 * v3, axis=0)
        acc = part if acc is None else acc + part
    o_ref[0] = acc * jnp.concatenate([inv_l, inv_l], axis=0)


def _mem_rows(cache):
    dp, db, nm = cache.shape[:3]
    x = cache.reshape(dp, db, nm, H_X, 2, LANES)
    return jnp.transpose(x, (0, 1, 2, 4, 3, 5)).reshape(dp, db, nm * 8, LANES)


def _xattn_sample(q, mk_rows, mv_rows, layer):
    db, d = q.shape
    n_rows = mk_rows.shape[2]
    q8 = jnp.transpose(q.reshape(db, H_X, 2, LANES), (0, 2, 1, 3)).reshape(db, 8, LANES)
    r3 = lambda i: (i, 0, 0)
    mem = lambda i: (layer, i, 0, 0)
    o8 = pl.pallas_call(
        _xattn_sample_kernel, grid=(db,),
        in_specs=[pl.BlockSpec((1, 8, LANES), r3), pl.BlockSpec((None, None, n_rows, LANES), mem),
                  pl.BlockSpec((None, None, n_rows, LANES), mem)],
        out_specs=pl.BlockSpec((1, 8, LANES), r3), out_shape=jax.ShapeDtypeStruct((db, 8, LANES), F32),
        compiler_params=_cparams("parallel"), name="xattn_sample")(q8, mk_rows, mv_rows)
    return jnp.transpose(o8.reshape(db, 2, H_X, LANES), (0, 2, 1, 3)).reshape(db, d)


def _head_rows(h):
    return slice(h * DH_B, (h + 1) * DH_B)


def _fox_sample_kernel(pt_ref, q_ref, kn_ref, vn_ref, f_ref, bf_ref, *refs, g_pages):
    k_refs = refs[0:g_pages]
    v_refs = refs[g_pages:2 * g_pages]
    lf_refs = refs[2 * g_pages:3 * g_pages]
    o_ref, lfn_ref, qb_ref, m_ref, l_ref, acc_ref, carry_ref = refs[3 * g_pages:]
    del pt_ref
    w = H_B * DH_B
    gi = pl.program_id(1)
    bcast = lambda col: jnp.broadcast_to(col, (H_B, LANES))
    head_id = _iota((H_B, LANES), 0)

    def rows_to_tile(rows):
        n = rows[0].shape[1]
        out = jnp.broadcast_to(rows[0], (H_B, n))
        for h in range(1, H_B):
            out = jnp.where(head_id[:, 0:n] == h, rows[h], out)
        return out

    @pl.when(gi == 0)
    def _():
        q = q_ref[0] * (DH_B ** -0.5)
        qb_ref[...] = jnp.broadcast_to(_col_from_row(q, w), (w, LANES))
        lfn = _log_sigmoid(f_ref[0] + bf_ref[...])
        lfn_ref[0] = lfn
        qk = q * kn_ref[0]
        s_new = rows_to_tile([jnp.sum(qk[:, _head_rows(h)], axis=1, keepdims=True) for h in range(H_B)])
        m_ref[...] = bcast(s_new)
        l_ref[...] = jnp.ones_like(l_ref)
        acc_ref[...] = jnp.where(_iota((w, LANES), 1) == 0, _col_from_row(vn_ref[0], w), 0.0)
        carry_ref[...] = bcast(lfn)

    later = (_iota((PAGE, PAGE), 0) > _iota((PAGE, PAGE), 1)).astype(BF16)
    m = m_ref[:, 0:1]
    l = l_ref[:, 0:1]
    carry = carry_ref[:, 0:1]
    logits = [None] * g_pages
    top = None
    suffix = _dot3_r(jnp.concatenate([lf_refs[jj][...] for jj in range(g_pages)], axis=0), later)
    for jj in reversed(range(g_pages)):
        s_t = rows_to_tile([jnp.sum(k_refs[jj][h] * qb_ref[_head_rows(h), :], axis=0, keepdims=True)
                            for h in range(H_B)])
        lf_t = lf_refs[jj][...]
        logits[jj] = s_t + suffix[jj * H_B:(jj + 1) * H_B, :] + carry
        carry = carry + jnp.sum(lf_t, axis=1, keepdims=True)
        top = logits[jj] if top is None else jnp.maximum(top, logits[jj])
    m_new = jnp.maximum(m, jnp.max(top, axis=1, keepdims=True))
    alpha = jnp.exp(m - m_new)
    probs = [jnp.exp(lg - m_new) for lg in logits]
    p_sum = probs[0]
    for p_t in probs[1:]:
        p_sum = p_sum + p_t
    for h in range(H_B):
        acc_h = acc_ref[_head_rows(h), :] * alpha[h:h + 1, :]
        for jj in range(g_pages):
            acc_h = acc_h + v_refs[jj][h] * probs[jj][h:h + 1, :]
        acc_ref[_head_rows(h), :] = acc_h
    l = alpha * l + jnp.sum(p_sum, axis=1, keepdims=True)
    m_ref[...] = bcast(m_new)
    l_ref[...] = bcast(l)
    carry_ref[...] = bcast(carry)

    @pl.when(gi == pl.num_programs(1) - 1)
    def _():
        inv = 1.0 / l
        den = jnp.concatenate([jnp.broadcast_to(inv[h:h + 1, :], (DH_B, 1)) for h in range(H_B)], axis=0)
        col = jnp.sum(acc_ref[...], axis=1, keepdims=True) * den
        eye = _iota((w, w), 0) == _iota((w, w), 1)
        o_ref[0] = jnp.sum(jnp.where(eye, jnp.broadcast_to(col, (w, w)), 0.0), axis=0, keepdims=True)


def _fox_sample(q, k_new, v_new, f_col, bf_col, cache_kt, cache_vt, cache_lft, page_table, g_pages=16):
    db = q.shape[0]
    w = H_B * DH_B
    n_pages = page_table.shape[1]
    g_pages = min(g_pages, n_pages)
    ng = n_pages // g_pages
    r3 = lambda i, g, pt: (i, 0, 0)

    def page_map(nd):
        return [(lambda i, g, pt, jj=jj: (pt[i * n_pages + (ng - 1 - g) * g_pages + jj],) + (0,) * nd)
                for jj in range(g_pages)]

    in_specs = [pl.BlockSpec((1, 1, w), r3)] * 3 + [pl.BlockSpec((1, H_B, 1), r3),
                                                    pl.BlockSpec(bf_col.shape, lambda i, g, pt: (0, 0))]
    in_specs += [pl.BlockSpec((None, H_B, DH_B, PAGE), mp) for mp in page_map(3)] * 2
    in_specs += [pl.BlockSpec((None, H_B, PAGE), mp) for mp in page_map(2)]
    grid_spec = pltpu.PrefetchScalarGridSpec(
        num_scalar_prefetch=1, grid=(db, ng), in_specs=in_specs,
        out_specs=[pl.BlockSpec((1, 1, w), r3), pl.BlockSpec((1, H_B, 1), r3)],
        scratch_shapes=[pltpu.VMEM((w, LANES), F32), pltpu.VMEM((H_B, LANES), F32), pltpu.VMEM((H_B, LANES), F32),
                        pltpu.VMEM((w, LANES), F32), pltpu.VMEM((H_B, LANES), F32)])
    kern = functools.partial(_fox_sample_kernel, g_pages=g_pages)
    return pl.pallas_call(
        kern, grid_spec=grid_spec,
        out_shape=[jax.ShapeDtypeStruct((db, 1, w), F32), jax.ShapeDtypeStruct((db, H_B, 1), F32)],
        compiler_params=_cparams("parallel", "arbitrary"), name="fox_sample")(
            page_table.reshape(-1), q, k_new, v_new, f_col, bf_col,
            *([cache_kt] * g_pages), *([cache_vt] * g_pages), *([cache_lft] * g_pages))


def _row(v):
    return v.reshape(1, -1).astype(F32)


def _pad_lanes(v, start):
    return jnp.zeros((1, LANES), F32).at[0, start:start + v.shape[0]].set(v)


def _chunk_rows(cols, b, l):
    n = l // CHUNK_A
    return cols.reshape(b, n, CHUNK_A, H_A).transpose(0, 1, 3, 2).reshape(b, n, H_A * CHUNK_A)


def kernel(x_prompt, x_sample, mem_prompt, cache_fox_k, cache_fox_v, cache_fox_logf, page_table, state_gdn, state_gdn_conv, state_cfm_conv, cache_mem_k, cache_mem_v, norm_mix, w_in_e, conv_a, a_log, dt_bias, gnorm_a, b_f, w_out_e, w_pw1, b_pw1, w_dw, b_dw, ln_g, ln_b, w_pw2, norm_mem, norm_x, w_xq, w_xkv, w_xo, norm_f, w_up, w_down, norm_out):
    b, l, d = x_prompt.shape
    db = x_sample.shape[0]
    n_mem = mem_prompt.shape[1]
    depth = norm_mix.shape[0]
    dh_x = d // H_X
    bf = lambda w: w.astype(BF16)

    w_in = w_in_e[0]
    off_aa = W_CONV_A + W_V_A
    off_qb = off_aa + 2 * H_A
    off_fb = off_qb + 3 * W_B
    w_small = jnp.concatenate([w_in[:, off_aa:off_qb], w_in[:, off_fb:]], axis=1)
    w_small = jnp.pad(w_small, ((0, 0), (0, LANES - w_small.shape[1])))
    w_in_r = bf(jnp.concatenate([w_in[:, :off_aa], w_in[:, off_qb:off_fb], w_small], axis=1))
    in_widths = (W_CONV_A, W_V_A, W_B, W_B, W_B, LANES)
    off_kb = off_qb + W_B
    off_vb = off_kb + W_B
    w_in_p = bf(jnp.concatenate([w_in[:, :off_aa], w_in[:, off_qb:off_kb], w_in[:, off_vb:off_fb], w_small], axis=1))
    w_in_t = [bf(w_in[:, off_kb:off_vb].T), bf(w_in[:, off_vb:off_fb].T), bf(w_in[:, off_fb:].T)]
    w_out_a = bf(w_out_e[0][:W_V_A])
    w_out_b = bf(w_out_e[0][W_V_A:])
    w_xq_b = [bf(w_xq[i]) for i in range(depth)]
    w_xo_b = [bf(w_xo[i]) for i in range(depth)]
    w_up_b = [bf(w_up[i]) for i in range(depth)]
    w_down_b = [bf(w_down[i]) for i in range(depth)]
    w_pw1_b = bf(w_pw1[0])
    w_pw2_b = bf(w_pw2[0])
    alog_rows = _row(jnp.repeat(a_log[0], CHUNK_A))
    dt_rows = _row(jnp.repeat(dt_bias[0], CHUNK_A))
    gn = _row(gnorm_a[0])

    memf = mem_prompt.reshape(b * n_mem, d)
    mem_k, mem_v = [], []
    for i in range(depth):
        mk, mv = _linear([memf], [bf(w_xkv[i])], gain=_row(norm_mem[i]), out_widths=(d, d), name="mem_kv")
        mem_k.append(mk.reshape(b, n_mem, d))
        mem_v.append(mv.reshape(b, n_mem, d))
    mem_k_prompt = jnp.stack(mem_k).reshape(depth, b, n_mem, H_X, dh_x)
    mem_v_prompt = jnp.stack(mem_v).reshape(depth, b, n_mem, H_X, dh_x)

    xp = x_prompt.reshape(b * l, d)
    conv_in, z, small, k_t, v_t, qf, kft, vf, lft = _in_proj_prompt(
        xp, w_in_p, w_in_t[0], w_in_t[1], w_in_t[2], _row(norm_mix[0]), _pad_lanes(b_f[0], SM_F),
        b_f[0].reshape(H_B, 1), b, l)
    fox_k_prompt = jnp.transpose(k_t.reshape(b, H_B, DH_B, l), (0, 3, 1, 2))[None]
    fox_v_prompt = jnp.transpose(v_t.reshape(b, H_B, DH_B, l), (0, 3, 1, 2))[None]
    conv_in3 = conv_in.reshape(b, l, W_CONV_A)
    gdn_conv_prompt = conv_in3[:, l - (CONV_A - 1):, :][None]
    oa, s_fin = _gdn_prompt(conv_in3, z.reshape(b, l, W_V_A),
                            _chunk_rows(small[:, SM_A:SM_A + H_A], b, l),
                            _chunk_rows(small[:, SM_B:SM_B + H_A], b, l),
                            conv_a[0], alog_rows, dt_rows, gn)
    gdn_state_prompt = s_fin.reshape(b, DK_A, H_A, DV_A).transpose(0, 2, 1, 3)[None]
    fox_logf_prompt = jnp.transpose(lft, (0, 2, 1))[None]
    ob = _fox_flash(qf, kft, vf)
    x = _linear([oa.reshape(b * l, W_V_A), ob.reshape(b * l, W_B)], [w_out_a, w_out_b], res=xp, name="out_proj")
    x = _xattn_prompt(x.reshape(b, l, d), _row(norm_x[0]), w_xq_b[0], bf(mem_k[0]), bf(mem_v[0]), w_xo_b[0])
    x = _mlp(x.reshape(b * l, d), _row(norm_f[0]), w_up_b[0], w_down_b[0])
    glu = _linear([x], [w_pw1_b], gain=_row(norm_mix[1]), bias=_row(b_pw1[0]), act="glu", name="pw1_glu")
    glu3 = glu.reshape(b, l, d)
    cfm_conv_prompt = glu3[:, l - (CONV_C - 1):, :][None]
    x = _cfm_prompt(glu3, x.reshape(b, l, d), w_dw[0], _row(b_dw[0]), _row(ln_g[0]), _row(ln_b[0]), w_pw2_b)
    x = _xattn_prompt(x, _row(norm_x[1]), w_xq_b[1], bf(mem_k[1]), bf(mem_v[1]), w_xo_b[1])
    y_prompt = _mlp(x.reshape(b * l, d), _row(norm_f[1]), w_up_b[1], w_down_b[1],
                    final_gain=_row(norm_out)).reshape(b, l, d)

    xs = x_sample.reshape(db, d)
    conv_s, z_s, q_s, k_s, v_s, small_s = _linear([xs], [w_in_r], gain=_row(norm_mix[0]),
                                                  out_widths=in_widths, name="in_proj_s")
    fox_k_sample = k_s.reshape(1, db, 1, H_B, DH_B)
    fox_v_sample = v_s.reshape(1, db, 1, H_B, DH_B)
    full = jnp.concatenate([state_gdn_conv[0], conv_s[:, None, :]], axis=1)
    gdn_conv_sample = full[:, 1:, :][None]
    oa_s, s_new = _gdn_sample(full, conv_a[0], small_s.reshape(db, 1, LANES), _pad_lanes(a_log[0], SM_A),
                              _pad_lanes(dt_bias[0], SM_A), z_s.reshape(db, 1, W_V_A), gn, state_gdn[0])
    gdn_state_sample = s_new[None]
    ob_s, lf_s = _fox_sample(q_s.reshape(db, 1, W_B), k_s.reshape(db, 1, W_B), v_s.reshape(db, 1, W_B),
                             small_s[:, SM_F:SM_F + H_B].reshape(db, H_B, 1), b_f[0].reshape(H_B, 1),
                             jnp.transpose(cache_fox_k[0], (0, 2, 3, 1)), jnp.transpose(cache_fox_v[0], (0, 2, 3, 1)),
                             jnp.transpose(cache_fox_logf[0], (0, 2, 1)), page_table)
    fox_logf_sample = lf_s.reshape(1, db, 1, H_B)
    x = _linear([oa_s.reshape(db, W_V_A), ob_s.reshape(db, W_B)], [w_out_a, w_out_b], res=xs, name="out_proj_s")

    mk_rows = _mem_rows(cache_mem_k)
    mv_rows = _mem_rows(cache_mem_v)

    def xattn_s(x, i):
        qx = _linear([x], [w_xq_b[i]], gain=_row(norm_x[i]), name="xq_s")
        o = _xattn_sample(qx, mk_rows, mv_rows, i)
        return _linear([o], [w_xo_b[i]], res=x, name="xo_s")

    x = xattn_s(x, 0)
    x = _mlp(x, _row(norm_f[0]), w_up_b[0], w_down_b[0])
    glu_s = _linear([x], [w_pw1_b], gain=_row(norm_mix[1]), bias=_row(b_pw1[0]), act="glu", name="pw1_glu_s")
    cfm_conv_sample = jnp.concatenate([state_cfm_conv[0][:, 1:, :], glu_s[:, None, :]], axis=1)[None]
    x = _cfm_sample(state_cfm_conv[0].transpose(1, 0, 2), glu_s, x, w_dw[0], _row(b_dw[0]), _row(ln_g[0]),
                    _row(ln_b[0]), w_pw2_b)
    x = xattn_s(x, 1)
    y_sample = _mlp(x, _row(norm_f[1]), w_up_b[1], w_down_b[1], final_gain=_row(norm_out)).reshape(db, 1, d)

    return (y_prompt, y_sample, fox_k_prompt, fox_v_prompt, fox_logf_prompt, fox_k_sample, fox_v_sample,
            fox_logf_sample, gdn_state_prompt, gdn_conv_prompt, gdn_state_sample, gdn_conv_sample,
            cfm_conv_prompt, cfm_conv_sample, mem_k_prompt, mem_v_prompt)
```

```python
import functools
import math

import jax
import jax.numpy as jnp
import numpy as np
from jax import lax
from jax.experimental import pallas as pl
from jax.experimental.pallas import tpu as pltpu

F32 = jnp.float32
BF16 = jnp.bfloat16
EPS = 1e-6
NEG = -1e30
LOG2E = 1.4426950408889634

H_A = 4
DK_A = 128
DV_A = 128
CONV_A = 4
CHUNK_A = 64
H_B = 8
DH_B = 64
H_X = 4
CONV_C = 31
PAGE = 128

W_QK_A = H_A * DK_A
W_V_A = H_A * DV_A
W_CONV_A = 2 * W_QK_A + W_V_A
W_B = H_B * DH_B
LANES = 128
SM_A = 0
SM_B = H_A
SM_F = 2 * H_A

VMEM_LIMIT = 56 * 1024 * 1024


def _cparams(*sem):
    return pltpu.CompilerParams(dimension_semantics=sem, vmem_limit_bytes=VMEM_LIMIT)


def _const_spec(shape):
    nd = len(shape)
    return pl.BlockSpec(shape, lambda *_: (0,) * nd)


def _rms(x, g):
    return x * lax.rsqrt(jnp.mean(x * x, axis=-1, keepdims=True) + EPS) * g


def _silu(x):
    return x * jax.nn.sigmoid(x)


def _softplus(x):
    return jnp.maximum(x, 0.0) + jnp.log1p(jnp.exp(-jnp.abs(x)))


def _split3(x):
    h = x.astype(BF16)
    r = x - h.astype(F32)
    m = r.astype(BF16)
    l = (r - m.astype(F32)).astype(BF16)
    return h, m, l


def _dot3_l(mat01, x):
    h, m, l = _split3(x)
    d = lambda t: jnp.dot(mat01, t, preferred_element_type=F32)
    return d(h) + d(m) + d(l)


def _dot3_r(x, mat01):
    h, m, l = _split3(x)
    d = lambda t: jnp.dot(t, mat01, preferred_element_type=F32)
    return d(h) + d(m) + d(l)


def _dot_nt(a, b):
    return lax.dot_general(a, b, (((1,), (1,)), ((), ())), preferred_element_type=F32)


def _dot_tn(a, b):
    return lax.dot_general(a, b, (((0,), (0,)), ((), ())), preferred_element_type=F32)


def _iota(shape, dim):
    return lax.broadcasted_iota(jnp.int32, shape, dim)


def _col_from_row(row, n):
    eye = _iota((n, n), 0) == _iota((n, n), 1)
    return jnp.sum(jnp.where(eye, jnp.broadcast_to(row, (n, n)), 0.0), axis=1, keepdims=True)


def _linear_kernel(*refs, n_in, n_t, has_gain, has_bias, has_res, act, out_widths, chunk):
    it = iter(refs)
    x_refs = [next(it) for _ in range(n_in)]
    w_refs = [next(it) for _ in range(n_in)]
    wt_refs = [next(it) for _ in range(n_t)]
    gain_ref = next(it) if has_gain else None
    bias_ref = next(it) if has_bias else None
    res_ref = next(it) if has_res else None
    out_refs = [next(it) for _ in out_widths]
    outt_refs = [next(it) for _ in range(n_t)]
    xs = []
    for i, xr in enumerate(x_refs):
        x = xr[...]
        if has_gain and i == 0:
            x = _rms(x, gain_ref[...])
        xs.append(x.astype(BF16))
    n_total = sum(out_widths)

    def mm(col0, cw):
        acc = None
        for xb, wr in zip(xs, w_refs):
            d = jnp.dot(xb, wr[:, col0:col0 + cw], preferred_element_type=F32)
            acc = d if acc is None else acc + d
        if has_bias:
            acc = acc + bias_ref[:, col0:col0 + cw]
        return acc

    off = 0
    for o_ref, width in zip(out_refs, out_widths):
        for c0 in range(0, width, chunk):
            cw = min(chunk, width - c0)
            y = mm(off + c0, cw)
            if act == "glu":
                y = y * jax.nn.sigmoid(mm(n_total + off + c0, cw))
            if has_res:
                y = y + res_ref[:, off + c0:off + c0 + cw]
            o_ref[:, c0:c0 + cw] = y
        off += width
    for wt_ref, ot_ref in zip(wt_refs, outt_refs):
        ot_ref[0] = _dot_nt(wt_ref[...], xs[0])


def _linear(xs, ws, *, gain=None, bias=None, res=None, act=None, out_widths=None, wts=(), t_batch=None,
            tm=512, chunk=512, name="linear"):
    m = xs[0].shape[0]
    tm = min(tm, m)
    assert m % tm == 0
    n_mm = ws[0].shape[1]
    n_out = n_mm // 2 if act == "glu" else n_mm
    if out_widths is None:
        out_widths = (n_out,)
    assert sum(out_widths) == n_out
    row = lambda i: (i, 0)
    in_specs = [pl.BlockSpec((tm, x.shape[1]), row) for x in xs]
    in_specs += [_const_spec(w.shape) for w in ws]
    in_specs += [_const_spec(w.shape) for w in wts]
    args = list(xs) + list(ws) + list(wts)
    if gain is not None:
        in_specs.append(_const_spec(gain.shape)); args.append(gain)
    if bias is not None:
        in_specs.append(_const_spec(bias.shape)); args.append(bias)
    if res is not None:
        in_specs.append(pl.BlockSpec((tm, res.shape[1]), row)); args.append(res)
    out_shape = [jax.ShapeDtypeStruct((m, w), F32) for w in out_widths]
    out_specs = [pl.BlockSpec((tm, w), row) for w in out_widths]
    if wts:
        tb, tl = t_batch
        assert tb * tl == m and tl % tm == 0
        per = tl // tm
        out_shape += [jax.ShapeDtypeStruct((tb, w.shape[0], tl), F32) for w in wts]
        out_specs += [pl.BlockSpec((1, w.shape[0], tm), lambda i: (i // per, 0, i % per)) for w in wts]
    kern = functools.partial(_linear_kernel, n_in=len(xs), n_t=len(wts), has_gain=gain is not None,
                             has_bias=bias is not None, has_res=res is not None, act=act,
                             out_widths=tuple(out_widths), chunk=chunk)
    outs = pl.pallas_call(kern, grid=(m // tm,), in_specs=in_specs, out_specs=out_specs,
                          out_shape=out_shape, compiler_params=_cparams("parallel"), name=name)(*args)
    return outs if len(outs) > 1 else outs[0]


def _mlp_kernel(*refs, chunk, final_norm):
    if final_norm:
        x_ref, g_ref, wu_ref, wd_ref, go_ref, o_ref = refs
    else:
        x_ref, g_ref, wu_ref, wd_ref, o_ref = refs
    x = x_ref[...]
    xn = _rms(x, g_ref[...]).astype(BF16)
    acc = x
    d_ff = wu_ref.shape[1]
    for c0 in range(0, d_ff, chunk):
        h = jnp.dot(xn, wu_ref[:, c0:c0 + chunk], preferred_element_type=F32)
        h = jnp.square(jnp.maximum(h, 0.0)).astype(BF16)
        acc = acc + jnp.dot(h, wd_ref[c0:c0 + chunk, :], preferred_element_type=F32)
    if final_norm:
        acc = _rms(acc, go_ref[...])
    o_ref[...] = acc


def _mlp(x, gain, w_up, w_down, final_gain=None, tm=512, chunk=512):
    m, d = x.shape
    tm = min(tm, m)
    row = lambda i: (i, 0)
    in_specs = [pl.BlockSpec((tm, d), row), _const_spec(gain.shape), _const_spec(w_up.shape),
                _const_spec(w_down.shape)]
    args = [x, gain, w_up, w_down]
    if final_gain is not None:
        in_specs.append(_const_spec(final_gain.shape)); args.append(final_gain)
    kern = functools.partial(_mlp_kernel, chunk=chunk, final_norm=final_gain is not None)
    return pl.pallas_call(kern, grid=(m // tm,), in_specs=in_specs,
                          out_specs=pl.BlockSpec((tm, d), row),
                          out_shape=jax.ShapeDtypeStruct((m, d), F32),
                          compiler_params=_cparams("parallel"), name="mlp")(*args)


def _xattn_kernel(*refs, n_pre):
    x_ref = refs[0]
    pre_refs = refs[1:1 + n_pre]
    pw_refs = refs[1 + n_pre:1 + 2 * n_pre]
    g_ref, wq_ref, mk_ref, mv_ref, wo_ref, o_ref = refs[1 + 2 * n_pre:]
    x = x_ref[0]
    for p_ref, w_ref in zip(pre_refs, pw_refs):
        x = x + jnp.dot(p_ref[0].astype(BF16), w_ref[...], preferred_element_type=F32)
    d = x.shape[1]
    dh = d // H_X
    xn = _rms(x, g_ref[...]).astype(BF16)
    q = jnp.dot(xn, wq_ref[...], preferred_element_type=F32) * (dh ** -0.5)
    q = q.astype(BF16)
    head = lambda h: slice(h * dh, (h + 1) * dh)
    scores = lambda h: _dot_nt(q[:, head(h)], mk_ref[0, :, head(h)])
    outs = []
    s_next = scores(0)
    for h in range(H_X):
        sl = head(h)
        s = s_next
        if h + 1 < H_X:
            s_next = scores(h + 1)
        p = jnp.exp(s - jnp.max(s, axis=-1, keepdims=True))
        l = jnp.sum(p, axis=-1, keepdims=True)
        o = jnp.dot(p.astype(BF16), mv_ref[0, :, sl], preferred_element_type=F32)
        outs.append((o * (1.0 / l)).astype(BF16))
    o = jnp.concatenate(outs, axis=1)
    o_ref[0] = x + jnp.dot(o, wo_ref[...], preferred_element_type=F32)


def _xattn_prompt(x, gain, wq, mk, mv, wo, pre=(), pre_w=(), tm=1024):
    b, l, d = x.shape
    tm = min(tm, l)
    nm = mk.shape[1]
    blk = lambda bi, i: (bi, i, 0)
    mem = lambda bi, i: (bi, 0, 0)
    kern = functools.partial(_xattn_kernel, n_pre=len(pre))
    return pl.pallas_call(
        kern, grid=(b, l // tm),
        in_specs=[pl.BlockSpec((1, tm, d), blk)] + [pl.BlockSpec((1, tm, p.shape[2]), blk) for p in pre]
        + [_const_spec(w.shape) for w in pre_w]
        + [_const_spec(gain.shape), _const_spec(wq.shape),
           pl.BlockSpec((1, nm, d), mem), pl.BlockSpec((1, nm, d), mem), _const_spec(wo.shape)],
        out_specs=pl.BlockSpec((1, tm, d), blk),
        out_shape=jax.ShapeDtypeStruct((b, l, d), F32),
        compiler_params=_cparams("parallel", "parallel"), name="xattn_prompt")(
            x, *pre, *pre_w, gain, wq, mk, mv, wo)


def _log_sigmoid(x):
    return jnp.minimum(x, 0.0) - jnp.log1p(jnp.exp(-jnp.abs(x)))


def _head_lanes(x, h):
    p = h // 2
    blk = x[:, p * LANES:(p + 1) * LANES]
    if h % 2:
        blk = pltpu.roll(blk, DH_B, 1)
    return blk


def _fox_prep(q, kt, v, small, smt, bf_ref, bfc_ref, first, qf_ref, kft_ref, vf_ref, lft_ref, carry_ref,
              carryt_ref, between=None):
    t = q.shape[0]

    @pl.when(first)
    def _():
        carry_ref[...] = jnp.zeros_like(carry_ref)
        carryt_ref[...] = jnp.zeros_like(carryt_ref)

    lane = _iota((t, LANES), 1)
    gate_lane = (lane >= SM_F) & (lane < SM_F + H_B)
    logf = jnp.where(gate_lane, _log_sigmoid(small + bf_ref[...]), 0.0)
    before = _iota((t, t), 0) >= _iota((t, t), 1)
    c = _dot3_l(before.astype(BF16), logf) + carry_ref[0:1, :]
    carry_ref[0:1, :] = c[t - 1:t, :]
    c1, c2, c3 = _split3(c * LOG2E)
    c1, c2, c3 = c1.astype(F32), c2.astype(F32), c3.astype(F32)
    lft = _log_sigmoid(smt + bfc_ref[...])
    lft_ref[0] = lft
    ct = _dot3_r(lft, (_iota((t, t), 0) <= _iota((t, t), 1)).astype(BF16)) + carryt_ref[:, 0:1]
    carryt_ref[...] = jnp.broadcast_to(ct[:, t - 1:t], carryt_ref.shape)
    r1, r2, r3 = _split3(ct * LOG2E)
    r1, r2, r3 = r1.astype(F32), r2.astype(F32), r3.astype(F32)
    if between is not None:
        between()
    q = q * (DH_B ** -0.5 * LOG2E)
    low = lane < DH_B
    row = _iota((DH_B, t), 0)
    for h in range(H_B):
        col = slice(SM_F + h, SM_F + h + 1)
        b1, b2, b3 = c1[:, col], c2[:, col], c3[:, col]
        qx = jnp.where(lane == DH_B, b1, jnp.where(lane == DH_B + 1, b2, jnp.where(
            lane == DH_B + 2, b3, jnp.where(lane < DH_B + 6, 1.0, 0.0))))
        kx = jnp.where(row < 3, 1.0, jnp.where(row == 3, -r1[h:h + 1, :], jnp.where(
            row == 4, -r2[h:h + 1, :], jnp.where(row == 5, -r3[h:h + 1, :], 0.0))))
        vx = jnp.where(lane == DH_B, 1.0, 0.0)
        qf_ref[0, h] = jnp.where(low, _head_lanes(q, h), qx).astype(BF16)
        kft_ref[0, h] = jnp.concatenate([kt[h * DH_B:(h + 1) * DH_B, :], kx], axis=0).astype(BF16)
        vf_ref[0, h] = jnp.where(low, _head_lanes(v, h), vx).astype(BF16)


def _in_proj_prompt_kernel(x_ref, w_ref, wkt_ref, wvt_ref, wft_ref, g_ref, bf_ref, bfc_ref,
                           conv_ref, z_ref, sm_ref, kt_ref, vt_ref, qf_ref, kft_ref, vf_ref, lft_ref,
                           carry_ref, carryt_ref, *, per_seq):
    xn = _rms(x_ref[...], g_ref[...]).astype(BF16)
    mm = lambda c0, cw: jnp.dot(xn, w_ref[:, c0:c0 + cw], preferred_element_type=F32)
    off = W_CONV_A + W_V_A
    q = mm(off, W_B)
    v = mm(off + W_B, W_B)
    small = mm(off + 2 * W_B, LANES)
    sm_ref[...] = small
    kt = _dot_nt(wkt_ref[...], xn)
    kt_ref[0] = kt
    smt = _dot_nt(wft_ref[...], xn)

    def rest():
        vt_ref[0] = _dot_nt(wvt_ref[...], xn)
        for c0 in range(0, W_CONV_A, W_B):
            conv_ref[:, c0:c0 + W_B] = mm(c0, W_B)
        z_ref[...] = mm(W_CONV_A, W_V_A)

    _fox_prep(q, kt, v, small, smt, bf_ref, bfc_ref, pl.program_id(0) % per_seq == 0,
              qf_ref, kft_ref, vf_ref, lft_ref, carry_ref, carryt_ref, between=rest)


def _in_proj_prompt(x, w, wkt, wvt, wft, gain, bf_row, bf_col, b, l, tm=512):
    m, d = x.shape
    tm = min(tm, l)
    per = l // tm
    row = lambda i: (i, 0)
    tblk = lambda i: (i // per, 0, i % per)
    hblk = lambda i: (i // per, 0, i % per, 0)
    hshape = jax.ShapeDtypeStruct((b, H_B, l, LANES), BF16)
    consts = (w, wkt, wvt, wft, gain, bf_row, bf_col)
    kern = functools.partial(_in_proj_prompt_kernel, per_seq=per)
    return pl.pallas_call(
        kern, grid=(m // tm,),
        in_specs=[pl.BlockSpec((tm, d), row)] + [_const_spec(a.shape) for a in consts],
        out_specs=[pl.BlockSpec((tm, W_CONV_A), row), pl.BlockSpec((tm, W_V_A), row), pl.BlockSpec((tm, LANES), row),
                   pl.BlockSpec((1, W_B, tm), tblk), pl.BlockSpec((1, W_B, tm), tblk),
                   pl.BlockSpec((1, H_B, tm, LANES), hblk),
                   pl.BlockSpec((1, H_B, LANES, tm), lambda i: (i // per, 0, 0, i % per)),
                   pl.BlockSpec((1, H_B, tm, LANES), hblk), pl.BlockSpec((1, H_B, tm), tblk)],
        out_shape=[jax.ShapeDtypeStruct((m, W_CONV_A), F32), jax.ShapeDtypeStruct((m, W_V_A), F32),
                   jax.ShapeDtypeStruct((m, LANES), F32), jax.ShapeDtypeStruct((b, W_B, l), F32),
                   jax.ShapeDtypeStruct((b, W_B, l), F32), hshape,
                   jax.ShapeDtypeStruct((b, H_B, LANES, l), BF16), hshape, jax.ShapeDtypeStruct((b, H_B, l), F32)],
        scratch_shapes=[pltpu.VMEM((8, LANES), F32), pltpu.VMEM((H_B, LANES), F32)],
        compiler_params=_cparams("arbitrary"), name="in_proj")(x, *consts)


LOOKAHEAD = 2


def _fox_flash_kernel(it_ref, jt_ref, fl_ref, qf_ref, kft_ref, vf_ref, o_ref, m_ref, acc_ref):
    p_id = pl.program_id(1)
    i = it_ref[p_id]
    j = jt_ref[p_id]
    flags = fl_ref[p_id]
    tq = qf_ref.shape[2]
    tk = kft_ref.shape[3]

    @pl.when(j == 0)
    def _():
        m_ref[...] = jnp.full_like(m_ref, NEG)
        acc_ref[...] = jnp.zeros_like(acc_ref)

    def heads(masked, row0=0):
        nr = tq - row0
        rs = slice(row0, tq)
        if masked:
            visible = _iota((nr, tk), 1) + j * tk <= _iota((nr, tk), 0) + (i * tq + row0)

        def scores(h):
            s = jnp.dot(qf_ref[0, h, rs, :], kft_ref[0, h], preferred_element_type=F32)
            return jnp.where(visible, s, NEG) if masked else s

        pending = [scores(h) for h in range(LOOKAHEAD)]
        for h in range(H_B):
            s = pending.pop(0)
            if h + LOOKAHEAD < H_B:
                pending.append(scores(h + LOOKAHEAD))
            m_prev = m_ref[h, rs, :]
            m_new = jnp.maximum(m_prev, jnp.max(s, axis=-1, keepdims=True))
            alpha = jnp.exp2(m_prev - m_new)
            p = jnp.exp2(s - jnp.concatenate([m_new] * (tk // LANES), axis=1)).astype(BF16)
            acc_ref[h, rs, :] = alpha * acc_ref[h, rs, :] + jnp.dot(p, vf_ref[0, h], preferred_element_type=F32)
            m_ref[h, rs, :] = m_new

    @pl.when((flags & 5) == 0)
    def _():
        heads(False)

    @pl.when((flags & 5) == 1)
    def _():
        heads(True)

    @pl.when((flags & 4) == 4)
    def _():
        heads(True, tq // 2)

    @pl.when((flags & 2) == 2)
    def _():
        lane = _iota((tq, LANES), 1)
        for pr in range(H_B // 2):
            a0 = acc_ref[2 * pr]
            a1 = acc_ref[2 * pr + 1]
            o0 = a0 * (1.0 / a0[:, DH_B:DH_B + 1])
            o1 = a1 * (1.0 / a1[:, DH_B:DH_B + 1])
            o_ref[0, :, pr * LANES:(pr + 1) * LANES] = jnp.where(lane < DH_B, o0, pltpu.roll(o1, DH_B, 1))


def _fox_flash(qf, kft, vf, tq=1024, tk=512):
    b, hh, l, _ = qf.shape
    tk = min(tk, l)
    tq = tq if l % tq == 0 else tk
    assert l % tq == 0 and l % tk == 0
    its, jts, fls = [], [], []
    for i in range(l // tq):
        j_last = ((i + 1) * tq - 1) // tk
        for j in range(j_last + 1):
            crosses = (j + 1) * tk - 1 > i * tq
            lower_only = j * tk > i * tq + tq // 2 - 1
            its.append(i)
            jts.append(j)
            fls.append(int(crosses) + 2 * int(j == j_last) + 4 * int(lower_only))
    it, jt, fl = (jnp.asarray(np.array(a, np.int32)) for a in (its, jts, fls))
    qmap = lambda bi, p, it, jt, fl: (bi, 0, it[p], 0)
    kmap = lambda bi, p, it, jt, fl: (bi, 0, jt[p], 0)
    ktmap = lambda bi, p, it, jt, fl: (bi, 0, 0, jt[p])
    omap = lambda bi, p, it, jt, fl: (bi, it[p], 0)
    grid_spec = pltpu.PrefetchScalarGridSpec(
        num_scalar_prefetch=3, grid=(b, len(its)),
        in_specs=[pl.BlockSpec((1, hh, tq, LANES), qmap), pl.BlockSpec((1, hh, LANES, tk), ktmap),
                  pl.BlockSpec((1, hh, tk, LANES), kmap)],
        out_specs=pl.BlockSpec((1, tq, hh * DH_B), omap),
        scratch_shapes=[pltpu.VMEM((hh, tq, LANES), F32), pltpu.VMEM((hh, tq, LANES), F32)])
    return pl.pallas_call(
        _fox_flash_kernel, grid_spec=grid_spec,
        out_shape=jax.ShapeDtypeStruct((b, l, hh * DH_B), F32),
        compiler_params=_cparams("parallel", "arbitrary"), name="fox_flash")(it, jt, fl, qf, kft, vf)


def _gdn_kernel(x_ref, z_ref, a_ref, b_ref, cw_ref, al_ref, dt_ref, gn_ref, oa_ref, s_ref,
                xs_ref, st_ref, *, nc):
    c = CHUNK_A
    r = H_A * c
    t = nc * c
    halo = 8
    nb = x_ref.shape[0]

    @pl.when(pl.program_id(0) == 0)
    def _():
        xs_ref[:, 0:halo, :] = jnp.zeros((nb, halo, xs_ref.shape[2]), F32)
        st_ref[...] = jnp.zeros_like(st_ref)

    @pl.when(pl.program_id(0) != 0)
    def _():
        xs_ref[:, 0:halo, :] = xs_ref[:, t:t + halo, :]

    ri = _iota((r, r), 0)
    ci = _iota((r, r), 1)
    same = (ri // c) == (ci // c)
    incl = same & (ri >= ci)
    strict = same & (ri > ci)
    cum = (same & (ri <= ci)).astype(BF16)
    eye = ri == ci
    last = ((ci % c) == (c - 1)) & same

    def stack(a):
        return jnp.concatenate([a[:, h * DK_A:(h + 1) * DK_A] for h in range(H_A)], axis=0)

    pre = []
    for bi in range(nb):
        xs_ref[bi, halo:halo + t, :] = x_ref[bi]
        y = None
        for jj in range(CONV_A):
            o = halo - (CONV_A - 1) + jj
            term = xs_ref[bi, o:o + t, :] * cw_ref[jj:jj + 1, :]
            y = term if y is None else y + term
        y = _silu(y)
        g_rows = -jnp.exp(al_ref[...]) * _softplus(a_ref[bi, 0] + dt_ref[...])
        beta_rows = jax.nn.sigmoid(b_ref[bi, 0])
        gc_rows = _dot3_r(g_rows, cum)
        for n in range(nc):
            rows = slice(n * c, (n + 1) * c)
            gc_r = gc_rows[n:n + 1, :]
            gcb = jnp.broadcast_to(gc_r, (r, r))
            gc_c = jnp.sum(jnp.where(eye, gcb, 0.0), axis=1, keepdims=True)
            gl_c = jnp.sum(jnp.where(last, gcb, 0.0), axis=1, keepdims=True)
            beta_c = jnp.sum(jnp.where(eye, jnp.broadcast_to(beta_rows[n:n + 1, :], (r, r)), 0.0),
                             axis=1, keepdims=True)
            q = stack(y[rows, 0:W_QK_A])
            k = stack(y[rows, W_QK_A:2 * W_QK_A])
            v = stack(y[rows, 2 * W_QK_A:W_CONV_A])
            q = q * lax.rsqrt(jnp.sum(q * q, axis=1, keepdims=True) + EPS) * (DK_A ** -0.5)
            k = k * lax.rsqrt(jnp.sum(k * k, axis=1, keepdims=True) + EPS)
            decay = jnp.exp(jnp.where(incl, gc_c - gc_r, -jnp.inf))
            kb = k * beta_c
            k16 = k.astype(BF16)
            eg = jnp.exp(gc_c)
            pre.append(dict(decay=decay, kk=_dot_nt(kb.astype(BF16), k16), qk=_dot_nt(q.astype(BF16), k16),
                            rhs=jnp.concatenate([v * beta_c, kb * eg], axis=1).astype(BF16),
                            qg=(q * eg).astype(BF16), kdec=(k * jnp.exp(gl_c - gc_c)).astype(BF16),
                            dec=[jnp.exp(gl_c[h * c:h * c + 1, :]) for h in range(H_A)]))
    for pc in pre:
        pc["a_qk"] = jnp.where(incl, pc["qk"] * pc["decay"], 0.0).astype(BF16)
        pc["mneg"] = jnp.where(strict, -(pc["kk"] * pc["decay"]), 0.0)
        pc["tinv"] = jnp.where(eye, 1.0, 0.0) + pc["mneg"]
    for _ in range(int(math.log2(c)) - 1):
        for pc in pre:
            m16 = pc["mneg"].astype(BF16)
            pc["mneg"] = jnp.dot(m16, m16, preferred_element_type=F32)
        for pc in pre:
            pc["tinv"] = pc["tinv"] + jnp.dot(pc["tinv"].astype(BF16), pc["mneg"].astype(BF16),
                                              preferred_element_type=F32)
    for pc in pre:
        uw = jnp.dot(pc["tinv"].astype(BF16), pc["rhs"], preferred_element_type=F32)
        pc["u"] = uw[:, 0:DV_A]
        pc["w"] = uw[:, DV_A:2 * DV_A].astype(BF16)

    s_heads = [[st_ref[bi, :, h * DV_A:(h + 1) * DV_A] for h in range(H_A)] for bi in range(nb)]
    for n in range(nc):
        rows = slice(n * c, (n + 1) * c)
        for bi in range(nb):
            pc = pre[bi * nc + n]
            v_parts, qs_parts = [], []
            for h in range(H_A):
                hr = slice(h * c, (h + 1) * c)
                wq = jnp.concatenate([pc["w"][hr], pc["qg"][hr]], axis=0)
                wq_s = jnp.dot(wq, s_heads[bi][h].astype(BF16), preferred_element_type=F32)
                v_parts.append(pc["u"][hr] - wq_s[0:c])
                qs_parts.append(wq_s[c:2 * c])
            v_new = jnp.concatenate(v_parts, axis=0).astype(BF16)
            o = jnp.concatenate(qs_parts, axis=0) + jnp.dot(pc["a_qk"], v_new, preferred_element_type=F32)
            for h in range(H_A):
                hr = slice(h * c, (h + 1) * c)
                s_heads[bi][h] = s_heads[bi][h] * pc["dec"][h] + _dot_tn(pc["kdec"][hr], v_new[hr])
            on = o * lax.rsqrt(jnp.mean(o * o, axis=1, keepdims=True) + EPS) * gn_ref[...]
            on = jnp.concatenate([on[h * c:(h + 1) * c, :] for h in range(H_A)], axis=1)
            oa_ref[bi, rows, :] = on * _silu(z_ref[bi, rows, :])
    for bi in range(nb):
        for h in range(H_A):
            st_ref[bi, :, h * DV_A:(h + 1) * DV_A] = s_heads[bi][h]

    @pl.when(pl.program_id(0) == pl.num_programs(0) - 1)
    def _():
        s_ref[...] = st_ref[...]


def _gdn_prompt(conv_in, z, a_rows, b_rows, conv_w, alog_row, dt_row, gnorm, nc=4):
    b, l, wc = conv_in.shape
    n = l // CHUNK_A
    nc = min(nc, n)
    t = nc * CHUNK_A
    r = H_A * CHUNK_A
    blk = lambda i: (0, i, 0)
    gate = lambda i: (0, i, 0, 0)
    a_rows = a_rows.reshape(b, n // nc, nc, r)
    b_rows = b_rows.reshape(b, n // nc, nc, r)
    kern = functools.partial(_gdn_kernel, nc=nc)
    oa, s = pl.pallas_call(
        kern, grid=(n // nc,),
        in_specs=[pl.BlockSpec((b, t, wc), blk), pl.BlockSpec((b, t, W_V_A), blk),
                  pl.BlockSpec((b, 1, nc, r), gate), pl.BlockSpec((b, 1, nc, r), gate),
                  _const_spec(conv_w.shape), _const_spec(alog_row.shape), _const_spec(dt_row.shape),
                  _const_spec(gnorm.shape)],
        out_specs=[pl.BlockSpec((b, t, W_V_A), blk), _const_spec((b, DK_A, H_A * DV_A))],
        out_shape=[jax.ShapeDtypeStruct((b, l, W_V_A), F32),
                   jax.ShapeDtypeStruct((b, DK_A, H_A * DV_A), F32)],
        scratch_shapes=[pltpu.VMEM((b, t + 8, wc), F32), pltpu.VMEM((b, DK_A, H_A * DV_A), F32)],
        compiler_params=_cparams("arbitrary"), name="gdn_prompt")(
            conv_in, z, a_rows, b_rows, conv_w, alog_row, dt_row, gnorm)
    return oa, s


def _cfm_tail(y, bd_ref, lg_ref, lb_ref, w2_ref, res):
    y = y + bd_ref[...]
    yc = y - jnp.mean(y, axis=-1, keepdims=True)
    yn = yc * lax.rsqrt(jnp.mean(yc * yc, axis=-1, keepdims=True) + EPS) * lg_ref[...] + lb_ref[...]
    return res + jnp.dot(_silu(yn).astype(BF16), w2_ref[...], preferred_element_type=F32)


def _cfm_kernel(u_ref, x_ref, wd_ref, bd_ref, lg_ref, lb_ref, w2_ref, o_ref, xs_ref, sh_ref):
    t = u_ref.shape[1]
    halo = 32

    @pl.when(pl.program_id(1) == 0)
    def _():
        xs_ref[0:halo, :] = jnp.zeros((halo, xs_ref.shape[1]), F32)

    @pl.when(pl.program_id(1) != 0)
    def _():
        xs_ref[0:halo, :] = xs_ref[t:t + halo, :]

    xs_ref[halo:halo + t, :] = u_ref[0]
    y = None
    for r in range(8):
        offs = [o for o in range(halo - (CONV_C - 1), halo + 1) if o % 8 == r]
        if not offs:
            continue
        if r:
            n_rows = max(offs) - r + t
            sh_ref[0:n_rows, :] = xs_ref[r:r + n_rows, :]
        win_ref = sh_ref if r else xs_ref
        for o in offs:
            jj = o - (halo - (CONV_C - 1))
            term = win_ref[o - r:o - r + t, :] * wd_ref[jj:jj + 1, :]
            y = term if y is None else y + term
    o_ref[0] = _cfm_tail(y, bd_ref, lg_ref, lb_ref, w2_ref, x_ref[0])


def _cfm_prompt(u, x, w_dw, b_dw, ln_g, ln_b, w2, t=512):
    b, l, d = u.shape
    t = min(t, l)
    blk = lambda bi, i: (bi, i, 0)
    return pl.pallas_call(
        _cfm_kernel, grid=(b, l // t),
        in_specs=[pl.BlockSpec((1, t, d), blk), pl.BlockSpec((1, t, d), blk), _const_spec(w_dw.shape),
                  _const_spec(b_dw.shape), _const_spec(ln_g.shape), _const_spec(ln_b.shape),
                  _const_spec(w2.shape)],
        out_specs=pl.BlockSpec((1, t, d), blk),
        out_shape=jax.ShapeDtypeStruct((b, l, d), F32),
        scratch_shapes=[pltpu.VMEM((t + 32, d), F32), pltpu.VMEM((t + 32, d), F32)],
        compiler_params=_cparams("parallel", "arbitrary"), name="cfm_prompt")(
            u, x, w_dw, b_dw, ln_g, ln_b, w2)


def _cfm_sample_kernel(buf_ref, u_ref, x_ref, wd_ref, bd_ref, lg_ref, lb_ref, w2_ref, o_ref):
    y = u_ref[...] * wd_ref[CONV_C - 1:CONV_C, :]
    for jj in range(CONV_C - 1):
        y = y + buf_ref[jj] * wd_ref[jj:jj + 1, :]
    o_ref[...] = _cfm_tail(y, bd_ref, lg_ref, lb_ref, w2_ref, x_ref[...])


def _cfm_sample(buf_t, u, x, w_dw, b_dw, ln_g, ln_b, w2):
    args = (buf_t, u, x, w_dw, b_dw, ln_g, ln_b, w2)
    return pl.pallas_call(
        _cfm_sample_kernel, grid=(1,), in_specs=[_const_spec(a.shape) for a in args],
        out_specs=_const_spec(x.shape), out_shape=jax.ShapeDtypeStruct(x.shape, F32),
        compiler_params=_cparams("arbitrary"), name="cfm_sample")(*args)


def _rows8(row):
    return jnp.broadcast_to(row, (8, row.shape[1]))


def _gdn_sample_kernel(f_ref, cw_ref, sm_ref, al_ref, dt_ref, z_ref, gn_ref, s_ref, oa_ref, so_ref):
    y = _silu(jnp.sum(f_ref[0] * cw_ref[...], axis=0, keepdims=True))
    sm = sm_ref[0]
    g = -jnp.exp(al_ref[...]) * _softplus(sm + dt_ref[...])
    beta = jax.nn.sigmoid(sm)
    row_id = _iota((8, DK_A), 0)
    outs = []
    for h in range(H_A):
        q = y[:, h * DK_A:(h + 1) * DK_A]
        k = y[:, W_QK_A + h * DK_A:W_QK_A + (h + 1) * DK_A]
        v = y[:, 2 * W_QK_A + h * DV_A:2 * W_QK_A + (h + 1) * DV_A]
        q = q * lax.rsqrt(jnp.sum(q * q, axis=1, keepdims=True) + EPS) * (DK_A ** -0.5)
        k = k * lax.rsqrt(jnp.sum(k * k, axis=1, keepdims=True) + EPS)
        eg = jnp.exp(g[:, SM_A + h:SM_A + h + 1])
        bh = beta[:, SM_B + h:SM_B + h + 1]
        s = s_ref[0, h]
        lhs = jnp.where(row_id == 0, _rows8(k), jnp.where(row_id == 1, _rows8(q), 0.0)).astype(BF16)
        rs = jnp.dot(lhs, s.astype(BF16), preferred_element_type=F32)
        v_new = bh * (v - eg * rs[0:1, :])
        o = eg * rs[1:2, :] + jnp.sum(q * k, axis=1, keepdims=True) * v_new
        so_ref[0, h] = s * eg + _col_from_row(k, DK_A) * v_new
        outs.append(o * lax.rsqrt(jnp.mean(o * o, axis=1, keepdims=True) + EPS) * gn_ref[...])
    oa_ref[0] = jnp.concatenate(outs, axis=1) * _silu(z_ref[0])


def _gdn_sample(full, conv_w, small, alog_row, dt_row, z, gnorm, state):
    db = full.shape[0]
    r3 = lambda i: (i, 0, 0)
    r4 = lambda i: (i, 0, 0, 0)
    return pl.pallas_call(
        _gdn_sample_kernel, grid=(db,),
        in_specs=[pl.BlockSpec((1,) + full.shape[1:], r3), _const_spec(conv_w.shape),
                  pl.BlockSpec((1, 1, LANES), r3), _const_spec(alog_row.shape), _const_spec(dt_row.shape),
                  pl.BlockSpec((1, 1, W_V_A), r3), _const_spec(gnorm.shape),
                  pl.BlockSpec((1,) + state.shape[1:], r4)],
        out_specs=[pl.BlockSpec((1, 1, W_V_A), r3), pl.BlockSpec((1,) + state.shape[1:], r4)],
        out_shape=[jax.ShapeDtypeStruct((db, 1, W_V_A), F32), jax.ShapeDtypeStruct(state.shape, F32)],
        compiler_params=_cparams("parallel"), name="gdn_sample")(
            full, conv_w, small, alog_row, dt_row, z, gnorm, state)


def _xattn_sample_kernel(q_ref, mk_ref, mv_ref, o_ref):
    n_rows = mk_ref.shape[0]
    n_mem = n_rows // 8
    dh = 2 * LANES
    q8 = q_ref[0] * (dh ** -0.5)
    red = jnp.sum(mk_ref[...].reshape(n_mem, 8, LANES) * q8[None], axis=-1, keepdims=True)
    own_lane = _iota((LANES, 8, LANES), 0) == _iota((LANES, 8, LANES), 2)
    n_t = n_mem // LANES
    tiles = [jnp.sum(jnp.where(own_lane, red[j * LANES:(j + 1) * LANES], 0.0), axis=0) for j in range(n_t)]
    s8 = jnp.concatenate(tiles, axis=1)
    s = s8[0:H_X] + s8[H_X:2 * H_X]
    p = jnp.exp(s - jnp.max(s, axis=-1, keepdims=True))
    inv_l = 1.0 / jnp.sum(p, axis=-1, keepdims=True)
    p8 = jnp.concatenate([p, p], axis=0)
    ones = jnp.ones((LANES, LANES), BF16)
    acc = None
    for j in range(n_t):
        z = jnp.where(own_lane, p8[None, :, j * LANES:(j + 1) * LANES], 0.0).reshape(LANES * 8, LANES)
        p3 = jnp.dot(z.astype(BF16), ones, preferred_element_type=F32).reshape(LANES, 8, LANES)
        v3 = mv_ref[j * LANES * 8:(j + 1) * LANES * 8, :].reshape(LANES, 8, LANES)
        part = jnp.sum(p3 * v3, axis=0)
        acc = part if acc is None else acc + part
    o_ref[0] = acc * jnp.concatenate([inv_l, inv_l], axis=0)


def _mem_rows(cache):
    dp, db, nm = cache.shape[:3]
    x = cache.reshape(dp, db, nm, H_X, 2, LANES)
    return jnp.transpose(x, (0, 1, 2, 4, 3, 5)).reshape(dp, db, nm * 8, LANES)


def _xattn_sample(q, mk_rows, mv_rows, layer):
    db, d = q.shape
    n_rows = mk_rows.shape[2]
    q8 = jnp.transpose(q.reshape(db, H_X, 2, LANES), (0, 2, 1, 3)).reshape(db, 8, LANES)
    r3 = lambda i: (i, 0, 0)
    mem = lambda i: (layer, i, 0, 0)
    o8 = pl.pallas_call(
        _xattn_sample_kernel, grid=(db,),
        in_specs=[pl.BlockSpec((1, 8, LANES), r3), pl.BlockSpec((None, None, n_rows, LANES), mem),
                  pl.BlockSpec((None, None, n_rows, LANES), mem)],
        out_specs=pl.BlockSpec((1, 8, LANES), r3), out_shape=jax.ShapeDtypeStruct((db, 8, LANES), F32),
        compiler_params=_cparams("parallel"), name="xattn_sample")(q8, mk_rows, mv_rows)
    return jnp.transpose(o8.reshape(db, 2, H_X, LANES), (0, 2, 1, 3)).reshape(db, d)


def _head_rows(h):
    return slice(h * DH_B, (h + 1) * DH_B)


def _fox_sample_kernel(pt_ref, q_ref, kn_ref, vn_ref, f_ref, bf_ref, *refs, g_pages):
    k_refs = refs[0:g_pages]
    v_refs = refs[g_pages:2 * g_pages]
    lf_refs = refs[2 * g_pages:3 * g_pages]
    o_ref, lfn_ref, qb_ref, m_ref, l_ref, acc_ref, carry_ref = refs[3 * g_pages:]
    del pt_ref
    w = H_B * DH_B
    gi = pl.program_id(1)
    bcast = lambda col: jnp.broadcast_to(col, (H_B, LANES))
    head_id = _iota((H_B, LANES), 0)

    def rows_to_tile(rows):
        n = rows[0].shape[1]
        out = jnp.broadcast_to(rows[0], (H_B, n))
        for h in range(1, H_B):
            out = jnp.where(head_id[:, 0:n] == h, rows[h], out)
        return out

    @pl.when(gi == 0)
    def _():
        q = q_ref[0] * (DH_B ** -0.5)
        qb_ref[...] = jnp.broadcast_to(_col_from_row(q, w), (w, LANES))
        lfn = _log_sigmoid(f_ref[0] + bf_ref[...])
        lfn_ref[0] = lfn
        qk = q * kn_ref[0]
        s_new = rows_to_tile([jnp.sum(qk[:, _head_rows(h)], axis=1, keepdims=True) for h in range(H_B)])
        m_ref[...] = bcast(s_new)
        l_ref[...] = jnp.ones_like(l_ref)
        acc_ref[...] = jnp.where(_iota((w, LANES), 1) == 0, _col_from_row(vn_ref[0], w), 0.0)
        carry_ref[...] = bcast(lfn)

    later = (_iota((PAGE, PAGE), 0) > _iota((PAGE, PAGE), 1)).astype(BF16)
    m = m_ref[:, 0:1]
    l = l_ref[:, 0:1]
    carry = carry_ref[:, 0:1]
    logits = [None] * g_pages
    top = None
    suffix = _dot3_r(jnp.concatenate([lf_refs[jj][...] for jj in range(g_pages)], axis=0), later)
    for jj in reversed(range(g_pages)):
        s_t = rows_to_tile([jnp.sum(k_refs[jj][h] * qb_ref[_head_rows(h), :], axis=0, keepdims=True)
                            for h in range(H_B)])
        lf_t = lf_refs[jj][...]
        logits[jj] = s_t + suffix[jj * H_B:(jj + 1) * H_B, :] + carry
        carry = carry + jnp.sum(lf_t, axis=1, keepdims=True)
        top = logits[jj] if top is None else jnp.maximum(top, logits[jj])
    m_new = jnp.maximum(m, jnp.max(top, axis=1, keepdims=True))
    alpha = jnp.exp(m - m_new)
    probs = [jnp.exp(lg - m_new) for lg in logits]
    p_sum = probs[0]
    for p_t in probs[1:]:
        p_sum = p_sum + p_t
    for h in range(H_B):
        acc_h = acc_ref[_head_rows(h), :] * alpha[h:h + 1, :]
        for jj in range(g_pages):
            acc_h = acc_h + v_refs[jj][h] * probs[jj][h:h + 1, :]
        acc_ref[_head_rows(h), :] = acc_h
    l = alpha * l + jnp.sum(p_sum, axis=1, keepdims=True)
    m_ref[...] = bcast(m_new)
    l_ref[...] = bcast(l)
    carry_ref[...] = bcast(carry)

    @pl.when(gi == pl.num_programs(1) - 1)
    def _():
        inv = 1.0 / l
        den = jnp.concatenate([jnp.broadcast_to(inv[h:h + 1, :], (DH_B, 1)) for h in range(H_B)], axis=0)
        col = jnp.sum(acc_ref[...], axis=1, keepdims=True) * den
        eye = _iota((w, w), 0) == _iota((w, w), 1)
        o_ref[0] = jnp.sum(jnp.where(eye, jnp.broadcast_to(col, (w, w)), 0.0), axis=0, keepdims=True)


def _fox_sample(q, k_new, v_new, f_col, bf_col, cache_kt, cache_vt, cache_lft, page_table, g_pages=32):
    db = q.shape[0]
    w = H_B * DH_B
    n_pages = page_table.shape[1]
    g_pages = min(g_pages, n_pages)
    ng = n_pages // g_pages
    r3 = lambda i, g, pt: (i, 0, 0)

    def page_map(nd):
        return [(lambda i, g, pt, jj=jj: (pt[i * n_pages + (ng - 1 - g) * g_pages + jj],) + (0,) * nd)
                for jj in range(g_pages)]

    in_specs = [pl.BlockSpec((1, 1, w), r3)] * 3 + [pl.BlockSpec((1, H_B, 1), r3),
                                                    pl.BlockSpec(bf_col.shape, lambda i, g, pt: (0, 0))]
    in_specs += [pl.BlockSpec((None, H_B, DH_B, PAGE), mp) for mp in page_map(3)] * 2
    in_specs += [pl.BlockSpec((None, H_B, PAGE), mp) for mp in page_map(2)]
    grid_spec = pltpu.PrefetchScalarGridSpec(
        num_scalar_prefetch=1, grid=(db, ng), in_specs=in_specs,
        out_specs=[pl.BlockSpec((1, 1, w), r3), pl.BlockSpec((1, H_B, 1), r3)],
        scratch_shapes=[pltpu.VMEM((w, LANES), F32), pltpu.VMEM((H_B, LANES), F32), pltpu.VMEM((H_B, LANES), F32),
                        pltpu.VMEM((w, LANES), F32), pltpu.VMEM((H_B, LANES), F32)])
    kern = functools.partial(_fox_sample_kernel, g_pages=g_pages)
    return pl.pallas_call(
        kern, grid_spec=grid_spec,
        out_shape=[jax.ShapeDtypeStruct((db, 1, w), F32), jax.ShapeDtypeStruct((db, H_B, 1), F32)],
        compiler_params=_cparams("parallel", "arbitrary"), name="fox_sample")(
            page_table.reshape(-1), q, k_new, v_new, f_col, bf_col,
            *([cache_kt] * g_pages), *([cache_vt] * g_pages), *([cache_lft] * g_pages))


def _row(v):
    return v.reshape(1, -1).astype(F32)


def _pad_lanes(v, start):
    return jnp.zeros((1, LANES), F32).at[0, start:start + v.shape[0]].set(v)


def _chunk_rows(cols, b, l):
    n = l // CHUNK_A
    return cols.reshape(b, n, CHUNK_A, H_A).transpose(0, 1, 3, 2).reshape(b, n, H_A * CHUNK_A)


def kernel(x_prompt, x_sample, mem_prompt, cache_fox_k, cache_fox_v, cache_fox_logf, page_table, state_gdn, state_gdn_conv, state_cfm_conv, cache_mem_k, cache_mem_v, norm_mix, w_in_e, conv_a, a_log, dt_bias, gnorm_a, b_f, w_out_e, w_pw1, b_pw1, w_dw, b_dw, ln_g, ln_b, w_pw2, norm_mem, norm_x, w_xq, w_xkv, w_xo, norm_f, w_up, w_down, norm_out):
    b, l, d = x_prompt.shape
    db = x_sample.shape[0]
    n_mem = mem_prompt.shape[1]
    depth = norm_mix.shape[0]
    dh_x = d // H_X
    bf = lambda w: w.astype(BF16)

    w_in = w_in_e[0]
    off_aa = W_CONV_A + W_V_A
    off_qb = off_aa + 2 * H_A
    off_fb = off_qb + 3 * W_B
    w_small = jnp.concatenate([w_in[:, off_aa:off_qb], w_in[:, off_fb:]], axis=1)
    w_small = jnp.pad(w_small, ((0, 0), (0, LANES - w_small.shape[1])))
    w_in_r = bf(jnp.concatenate([w_in[:, :off_aa], w_in[:, off_qb:off_fb], w_small], axis=1))
    in_widths = (W_CONV_A, W_V_A, W_B, W_B, W_B, LANES)
    off_kb = off_qb + W_B
    off_vb = off_kb + W_B
    w_in_p = bf(jnp.concatenate([w_in[:, :off_aa], w_in[:, off_qb:off_kb], w_in[:, off_vb:off_fb], w_small], axis=1))
    w_in_t = [bf(w_in[:, off_kb:off_vb].T), bf(w_in[:, off_vb:off_fb].T), bf(w_in[:, off_fb:].T)]
    w_out_a = bf(w_out_e[0][:W_V_A])
    w_out_b = bf(w_out_e[0][W_V_A:])
    w_xq_b = [bf(w_xq[i]) for i in range(depth)]
    w_xo_b = [bf(w_xo[i]) for i in range(depth)]
    w_up_b = [bf(w_up[i]) for i in range(depth)]
    w_down_b = [bf(w_down[i]) for i in range(depth)]
    w_pw1_b = bf(w_pw1[0])
    w_pw2_b = bf(w_pw2[0])
    alog_rows = _row(jnp.repeat(a_log[0], CHUNK_A))
    dt_rows = _row(jnp.repeat(dt_bias[0], CHUNK_A))
    gn = _row(gnorm_a[0])

    memf = mem_prompt.reshape(b * n_mem, d)
    mem_k, mem_v = [], []
    for i in range(depth):
        mk, mv = _linear([memf], [bf(w_xkv[i])], gain=_row(norm_mem[i]), out_widths=(d, d), name="mem_kv")
        mem_k.append(mk.reshape(b, n_mem, d))
        mem_v.append(mv.reshape(b, n_mem, d))
    mem_k_prompt = jnp.stack(mem_k).reshape(depth, b, n_mem, H_X, dh_x)
    mem_v_prompt = jnp.stack(mem_v).reshape(depth, b, n_mem, H_X, dh_x)

    xp = x_prompt.reshape(b * l, d)
    conv_in, z, small, k_t, v_t, qf, kft, vf, lft = _in_proj_prompt(
        xp, w_in_p, w_in_t[0], w_in_t[1], w_in_t[2], _row(norm_mix[0]), _pad_lanes(b_f[0], SM_F),
        b_f[0].reshape(H_B, 1), b, l)
    fox_k_prompt = jnp.transpose(k_t.reshape(b, H_B, DH_B, l), (0, 3, 1, 2))[None]
    fox_v_prompt = jnp.transpose(v_t.reshape(b, H_B, DH_B, l), (0, 3, 1, 2))[None]
    conv_in3 = conv_in.reshape(b, l, W_CONV_A)
    gdn_conv_prompt = conv_in3[:, l - (CONV_A - 1):, :][None]
    oa, s_fin = _gdn_prompt(conv_in3, z.reshape(b, l, W_V_A),
                            _chunk_rows(small[:, SM_A:SM_A + H_A], b, l),
                            _chunk_rows(small[:, SM_B:SM_B + H_A], b, l),
                            conv_a[0], alog_rows, dt_rows, gn)
    gdn_state_prompt = s_fin.reshape(b, DK_A, H_A, DV_A).transpose(0, 2, 1, 3)[None]
    fox_logf_prompt = jnp.transpose(lft, (0, 2, 1))[None]
    ob = _fox_flash(qf, kft, vf)
    x = _xattn_prompt(x_prompt, _row(norm_x[0]), w_xq_b[0], bf(mem_k[0]), bf(mem_v[0]), w_xo_b[0],
                      pre=(oa, ob), pre_w=(w_out_a, w_out_b))
    x = _mlp(x.reshape(b * l, d), _row(norm_f[0]), w_up_b[0], w_down_b[0])
    glu = _linear([x], [w_pw1_b], gain=_row(norm_mix[1]), bias=_row(b_pw1[0]), act="glu", name="pw1_glu")
    glu3 = glu.reshape(b, l, d)
    cfm_conv_prompt = glu3[:, l - (CONV_C - 1):, :][None]
    x = _cfm_prompt(glu3, x.reshape(b, l, d), w_dw[0], _row(b_dw[0]), _row(ln_g[0]), _row(ln_b[0]), w_pw2_b)
    x = _xattn_prompt(x, _row(norm_x[1]), w_xq_b[1], bf(mem_k[1]), bf(mem_v[1]), w_xo_b[1])
    y_prompt = _mlp(x.reshape(b * l, d), _row(norm_f[1]), w_up_b[1], w_down_b[1],
                    final_gain=_row(norm_out)).reshape(b, l, d)

    xs = x_sample.reshape(db, d)
    conv_s, z_s, q_s, k_s, v_s, small_s = _linear([xs], [w_in_r], gain=_row(norm_mix[0]),
                                                  out_widths=in_widths, name="in_proj_s")
    fox_k_sample = k_s.reshape(1, db, 1, H_B, DH_B)
    fox_v_sample = v_s.reshape(1, db, 1, H_B, DH_B)
    full = jnp.concatenate([state_gdn_conv[0], conv_s[:, None, :]], axis=1)
    gdn_conv_sample = full[:, 1:, :][None]
    oa_s, s_new = _gdn_sample(full, conv_a[0], small_s.reshape(db, 1, LANES), _pad_lanes(a_log[0], SM_A),
                              _pad_lanes(dt_bias[0], SM_A), z_s.reshape(db, 1, W_V_A), gn, state_gdn[0])
    gdn_state_sample = s_new[None]
    ob_s, lf_s = _fox_sample(q_s.reshape(db, 1, W_B), k_s.reshape(db, 1, W_B), v_s.reshape(db, 1, W_B),
                             small_s[:, SM_F:SM_F + H_B].reshape(db, H_B, 1), b_f[0].reshape(H_B, 1),
                             jnp.transpose(cache_fox_k[0], (0, 2, 3, 1)), jnp.transpose(cache_fox_v[0], (0, 2, 3, 1)),
                             jnp.transpose(cache_fox_logf[0], (0, 2, 1)), page_table)
    fox_logf_sample = lf_s.reshape(1, db, 1, H_B)
    x = _linear([oa_s.reshape(db, W_V_A), ob_s.reshape(db, W_B)], [w_out_a, w_out_b], res=xs, name="out_proj_s")

    mk_rows = _mem_rows(cache_mem_k)
    mv_rows = _mem_rows(cache_mem_v)

    def xattn_s(x, i):
        qx = _linear([x], [w_xq_b[i]], gain=_row(norm_x[i]), name="xq_s")
        o = _xattn_sample(qx, mk_rows, mv_rows, i)
        return _linear([o], [w_xo_b[i]], res=x, name="xo_s")

    x = xattn_s(x, 0)
    x = _mlp(x, _row(norm_f[0]), w_up_b[0], w_down_b[0])
    glu_s = _linear([x], [w_pw1_b], gain=_row(norm_mix[1]), bias=_row(b_pw1[0]), act="glu", name="pw1_glu_s")
    cfm_conv_sample = jnp.concatenate([state_cfm_conv[0][:, 1:, :], glu_s[:, None, :]], axis=1)[None]
    x = _cfm_sample(state_cfm_conv[0].transpose(1, 0, 2), glu_s, x, w_dw[0], _row(b_dw[0]), _row(ln_g[0]),
                    _row(ln_b[0]), w_pw2_b)
    x = xattn_s(x, 1)
    y_sample = _mlp(x, _row(norm_f[1]), w_up_b[1], w_down_b[1], final_gain=_row(norm_out)).reshape(db, 1, d)

    return (y_prompt, y_sample, fox_k_prompt, fox_v_prompt, fox_logf_prompt, fox_k_sample, fox_v_sample,
            fox_logf_sample, gdn_state_prompt, gdn_conv_prompt, gdn_state_sample, gdn_conv_sample,
            cfm_conv_prompt, cfm_conv_sample, mem_k_prompt, mem_v_prompt)
```

```python
import functools
import math

import jax
import jax.numpy as jnp
import numpy as np
from jax import lax
from jax.experimental import pallas as pl
from jax.experimental.pallas import tpu as pltpu

F32 = jnp.float32
BF16 = jnp.bfloat16
EPS = 1e-6
NEG = -1e30
LOG2E = 1.4426950408889634

H_A = 4
DK_A = 128
DV_A = 128
CONV_A = 4
CHUNK_A = 64
H_B = 8
DH_B = 64
H_X = 4
CONV_C = 31
PAGE = 128

W_QK_A = H_A * DK_A
W_V_A = H_A * DV_A
W_CONV_A = 2 * W_QK_A + W_V_A
W_B = H_B * DH_B
LANES = 128
SM_A = 0
SM_B = H_A
SM_F = 2 * H_A

VMEM_LIMIT = 56 * 1024 * 1024


def _cparams(*sem):
    return pltpu.CompilerParams(dimension_semantics=sem, vmem_limit_bytes=VMEM_LIMIT)


def _const_spec(shape):
    nd = len(shape)
    return pl.BlockSpec(shape, lambda *_: (0,) * nd)


def _rms(x, g):
    return x * lax.rsqrt(jnp.mean(x * x, axis=-1, keepdims=True) + EPS) * g


def _silu(x):
    return x * jax.nn.sigmoid(x)


def _softplus(x):
    return jnp.maximum(x, 0.0) + jnp.log1p(jnp.exp(-jnp.abs(x)))


def _split3(x):
    h = x.astype(BF16)
    r = x - h.astype(F32)
    m = r.astype(BF16)
    l = (r - m.astype(F32)).astype(BF16)
    return h, m, l


def _dot3_l(mat01, x):
    h, m, l = _split3(x)
    d = lambda t: jnp.dot(mat01, t, preferred_element_type=F32)
    return d(h) + d(m) + d(l)


def _dot3_r(x, mat01):
    h, m, l = _split3(x)
    d = lambda t: jnp.dot(t, mat01, preferred_element_type=F32)
    return d(h) + d(m) + d(l)


def _dot_nt(a, b):
    return lax.dot_general(a, b, (((1,), (1,)), ((), ())), preferred_element_type=F32)


def _dot_tn(a, b):
    return lax.dot_general(a, b, (((0,), (0,)), ((), ())), preferred_element_type=F32)


def _iota(shape, dim):
    return lax.broadcasted_iota(jnp.int32, shape, dim)


def _col_from_row(row, n):
    eye = _iota((n, n), 0) == _iota((n, n), 1)
    return jnp.sum(jnp.where(eye, jnp.broadcast_to(row, (n, n)), 0.0), axis=1, keepdims=True)


def _linear_kernel(*refs, n_in, has_gain, has_bias, has_res, act, out_widths, chunk):
    it = iter(refs)
    x_refs = [next(it) for _ in range(n_in)]
    w_refs = [next(it) for _ in range(n_in)]
    gain_ref = next(it) if has_gain else None
    bias_ref = next(it) if has_bias else None
    res_ref = next(it) if has_res else None
    out_refs = list(it)
    xs = []
    for i, xr in enumerate(x_refs):
        x = xr[...]
        if has_gain and i == 0:
            x = _rms(x, gain_ref[...])
        xs.append(x.astype(BF16))
    n_total = sum(out_widths)

    def mm(col0, cw):
        acc = None
        for xb, wr in zip(xs, w_refs):
            d = jnp.dot(xb, wr[:, col0:col0 + cw], preferred_element_type=F32)
            acc = d if acc is None else acc + d
        if has_bias:
            acc = acc + bias_ref[:, col0:col0 + cw]
        return acc

    off = 0
    for o_ref, width in zip(out_refs, out_widths):
        for c0 in range(0, width, chunk):
            cw = min(chunk, width - c0)
            y = mm(off + c0, cw)
            if act == "glu":
                y = y * jax.nn.sigmoid(mm(n_total + off + c0, cw))
            if has_res:
                y = y + res_ref[:, off + c0:off + c0 + cw]
            o_ref[:, c0:c0 + cw] = y
        off += width


def _linear(xs, ws, *, gain=None, bias=None, res=None, act=None, out_widths=None, tm=512, chunk=512,
            name="linear"):
    m = xs[0].shape[0]
    tm = min(tm, m)
    assert m % tm == 0
    n_mm = ws[0].shape[1]
    n_out = n_mm // 2 if act == "glu" else n_mm
    if out_widths is None:
        out_widths = (n_out,)
    assert sum(out_widths) == n_out
    row = lambda i: (i, 0)
    in_specs = [pl.BlockSpec((tm, x.shape[1]), row) for x in xs]
    in_specs += [_const_spec(w.shape) for w in ws]
    args = list(xs) + list(ws)
    if gain is not None:
        in_specs.append(_const_spec(gain.shape)); args.append(gain)
    if bias is not None:
        in_specs.append(_const_spec(bias.shape)); args.append(bias)
    if res is not None:
        in_specs.append(pl.BlockSpec((tm, res.shape[1]), row)); args.append(res)
    out_shape = [jax.ShapeDtypeStruct((m, w), F32) for w in out_widths]
    out_specs = [pl.BlockSpec((tm, w), row) for w in out_widths]
    kern = functools.partial(_linear_kernel, n_in=len(xs), has_gain=gain is not None,
                             has_bias=bias is not None, has_res=res is not None, act=act,
                             out_widths=tuple(out_widths), chunk=chunk)
    outs = pl.pallas_call(kern, grid=(m // tm,), in_specs=in_specs, out_specs=out_specs,
                          out_shape=out_shape, compiler_params=_cparams("parallel"), name=name)(*args)
    return outs if len(outs) > 1 else outs[0]


def _mlp_kernel(*refs, chunk, final_norm):
    if final_norm:
        x_ref, g_ref, wu_ref, wd_ref, go_ref, o_ref = refs
    else:
        x_ref, g_ref, wu_ref, wd_ref, o_ref = refs
    x = x_ref[...]
    xn = _rms(x, g_ref[...]).astype(BF16)
    acc = x
    d_ff = wu_ref.shape[1]
    for c0 in range(0, d_ff, chunk):
        h = jnp.dot(xn, wu_ref[:, c0:c0 + chunk], preferred_element_type=F32)
        h = jnp.square(jnp.maximum(h, 0.0)).astype(BF16)
        acc = acc + jnp.dot(h, wd_ref[c0:c0 + chunk, :], preferred_element_type=F32)
    if final_norm:
        acc = _rms(acc, go_ref[...])
    o_ref[...] = acc


def _mlp(x, gain, w_up, w_down, final_gain=None, tm=512, chunk=512):
    m, d = x.shape
    tm = min(tm, m)
    row = lambda i: (i, 0)
    in_specs = [pl.BlockSpec((tm, d), row), _const_spec(gain.shape), _const_spec(w_up.shape),
                _const_spec(w_down.shape)]
    args = [x, gain, w_up, w_down]
    if final_gain is not None:
        in_specs.append(_const_spec(final_gain.shape)); args.append(final_gain)
    kern = functools.partial(_mlp_kernel, chunk=chunk, final_norm=final_gain is not None)
    return pl.pallas_call(kern, grid=(m // tm,), in_specs=in_specs,
                          out_specs=pl.BlockSpec((tm, d), row),
                          out_shape=jax.ShapeDtypeStruct((m, d), F32),
                          compiler_params=_cparams("parallel"), name="mlp")(*args)


def _xattn_kernel(*refs, n_pre):
    x_ref = refs[0]
    pre_refs = refs[1:1 + n_pre]
    pw_refs = refs[1 + n_pre:1 + 2 * n_pre]
    g_ref, wq_ref, mk_ref, mv_ref, wo_ref, o_ref = refs[1 + 2 * n_pre:]
    x = x_ref[0]
    for p_ref, w_ref in zip(pre_refs, pw_refs):
        x = x + jnp.dot(p_ref[0].astype(BF16), w_ref[...], preferred_element_type=F32)
    d = x.shape[1]
    dh = d // H_X
    xn = _rms(x, g_ref[...]).astype(BF16)
    q = jnp.dot(xn, wq_ref[...], preferred_element_type=F32) * (dh ** -0.5)
    q = q.astype(BF16)
    head = lambda h: slice(h * dh, (h + 1) * dh)
    scores = lambda h: _dot_nt(q[:, head(h)], mk_ref[0, :, head(h)])
    outs = []
    s_next = scores(0)
    for h in range(H_X):
        sl = head(h)
        s = s_next
        if h + 1 < H_X:
            s_next = scores(h + 1)
        p = jnp.exp(s - jnp.max(s, axis=-1, keepdims=True))
        l = jnp.sum(p, axis=-1, keepdims=True)
        o = jnp.dot(p.astype(BF16), mv_ref[0, :, sl], preferred_element_type=F32)
        outs.append((o * (1.0 / l)).astype(BF16))
    o = jnp.concatenate(outs, axis=1)
    o_ref[0] = x + jnp.dot(o, wo_ref[...], preferred_element_type=F32)


def _xattn_prompt(x, gain, wq, mk, mv, wo, pre=(), pre_w=(), tm=1024):
    b, l, d = x.shape
    tm = min(tm, l)
    nm = mk.shape[1]
    blk = lambda bi, i: (bi, i, 0)
    mem = lambda bi, i: (bi, 0, 0)
    kern = functools.partial(_xattn_kernel, n_pre=len(pre))
    return pl.pallas_call(
        kern, grid=(b, l // tm),
        in_specs=[pl.BlockSpec((1, tm, d), blk)] + [pl.BlockSpec((1, tm, p.shape[2]), blk) for p in pre]
        + [_const_spec(w.shape) for w in pre_w]
        + [_const_spec(gain.shape), _const_spec(wq.shape),
           pl.BlockSpec((1, nm, d), mem), pl.BlockSpec((1, nm, d), mem), _const_spec(wo.shape)],
        out_specs=pl.BlockSpec((1, tm, d), blk),
        out_shape=jax.ShapeDtypeStruct((b, l, d), F32),
        compiler_params=_cparams("parallel", "parallel"), name="xattn_prompt")(
            x, *pre, *pre_w, gain, wq, mk, mv, wo)


def _log_sigmoid(x):
    return jnp.minimum(x, 0.0) - jnp.log1p(jnp.exp(-jnp.abs(x)))


def _head_lanes(x, h):
    p = h // 2
    blk = x[:, p * LANES:(p + 1) * LANES]
    if h % 2:
        blk = pltpu.roll(blk, DH_B, 1)
    return blk


def _fox_prep(q, kt, v, small, smt, bf_ref, bfc_ref, first, qf_ref, kft_ref, vf_ref, lft_ref, carry_ref,
              carryt_ref, between=None):
    t = q.shape[0]

    @pl.when(first)
    def _():
        carry_ref[...] = jnp.zeros_like(carry_ref)
        carryt_ref[...] = jnp.zeros_like(carryt_ref)

    lane = _iota((t, LANES), 1)
    gate_lane = (lane >= SM_F) & (lane < SM_F + H_B)
    logf = jnp.where(gate_lane, _log_sigmoid(small + bf_ref[...]), 0.0)
    before = _iota((t, t), 0) >= _iota((t, t), 1)
    c = _dot3_l(before.astype(BF16), logf) + carry_ref[0:1, :]
    carry_ref[0:1, :] = c[t - 1:t, :]
    c1, c2, c3 = _split3(c * LOG2E)
    c1, c2, c3 = c1.astype(F32), c2.astype(F32), c3.astype(F32)
    lft = _log_sigmoid(smt + bfc_ref[...])
    lft_ref[0] = lft
    ct = _dot3_r(lft, (_iota((t, t), 0) <= _iota((t, t), 1)).astype(BF16)) + carryt_ref[:, 0:1]
    carryt_ref[...] = jnp.broadcast_to(ct[:, t - 1:t], carryt_ref.shape)
    r1, r2, r3 = _split3(ct * LOG2E)
    r1, r2, r3 = r1.astype(F32), r2.astype(F32), r3.astype(F32)
    if between is not None:
        between()
    q = q * (DH_B ** -0.5 * LOG2E)
    low = lane < DH_B
    row = _iota((DH_B, t), 0)
    for h in range(H_B):
        col = slice(SM_F + h, SM_F + h + 1)
        b1, b2, b3 = c1[:, col], c2[:, col], c3[:, col]
        qx = jnp.where(lane == DH_B, b1, jnp.where(lane == DH_B + 1, b2, jnp.where(
            lane == DH_B + 2, b3, jnp.where(lane < DH_B + 6, 1.0, 0.0))))
        kx = jnp.where(row < 3, 1.0, jnp.where(row == 3, -r1[h:h + 1, :], jnp.where(
            row == 4, -r2[h:h + 1, :], jnp.where(row == 5, -r3[h:h + 1, :], 0.0))))
        vx = jnp.where(lane == DH_B, 1.0, 0.0)
        qf_ref[0, h] = jnp.where(low, _head_lanes(q, h), qx).astype(BF16)
        kft_ref[0, h] = jnp.concatenate([kt[h * DH_B:(h + 1) * DH_B, :], kx], axis=0).astype(BF16)
        vf_ref[0, h] = jnp.where(low, _head_lanes(v, h), vx).astype(BF16)


def _in_proj_prompt_kernel(x_ref, w_ref, wkt_ref, wvt_ref, wft_ref, g_ref, bf_ref, bfc_ref,
                           conv_ref, z_ref, sm_ref, kt_ref, vt_ref, qf_ref, kft_ref, vf_ref, lft_ref,
                           carry_ref, carryt_ref, *, per_seq):
    xn = _rms(x_ref[...], g_ref[...]).astype(BF16)
    mm = lambda c0, cw: jnp.dot(xn, w_ref[:, c0:c0 + cw], preferred_element_type=F32)
    off = W_CONV_A + W_V_A
    q = mm(off, W_B)
    v = mm(off + W_B, W_B)
    small = mm(off + 2 * W_B, LANES)
    sm_ref[...] = small
    kt = _dot_nt(wkt_ref[...], xn)
    kt_ref[0] = kt
    smt = _dot_nt(wft_ref[...], xn)

    def rest():
        vt_ref[0] = _dot_nt(wvt_ref[...], xn)
        for c0 in range(0, W_CONV_A, W_B):
            conv_ref[:, c0:c0 + W_B] = mm(c0, W_B)
        z_ref[...] = mm(W_CONV_A, W_V_A)

    _fox_prep(q, kt, v, small, smt, bf_ref, bfc_ref, pl.program_id(0) % per_seq == 0,
              qf_ref, kft_ref, vf_ref, lft_ref, carry_ref, carryt_ref, between=rest)


def _in_proj_prompt(x, w, wkt, wvt, wft, gain, bf_row, bf_col, b, l, tm=512):
    m, d = x.shape
    tm = min(tm, l)
    per = l // tm
    row = lambda i: (i, 0)
    tblk = lambda i: (i // per, 0, i % per)
    hblk = lambda i: (i // per, 0, i % per, 0)
    hshape = jax.ShapeDtypeStruct((b, H_B, l, LANES), BF16)
    consts = (w, wkt, wvt, wft, gain, bf_row, bf_col)
    kern = functools.partial(_in_proj_prompt_kernel, per_seq=per)
    return pl.pallas_call(
        kern, grid=(m // tm,),
        in_specs=[pl.BlockSpec((tm, d), row)] + [_const_spec(a.shape) for a in consts],
        out_specs=[pl.BlockSpec((tm, W_CONV_A), row), pl.BlockSpec((tm, W_V_A), row), pl.BlockSpec((tm, LANES), row),
                   pl.BlockSpec((1, W_B, tm), tblk), pl.BlockSpec((1, W_B, tm), tblk),
                   pl.BlockSpec((1, H_B, tm, LANES), hblk),
                   pl.BlockSpec((1, H_B, LANES, tm), lambda i: (i // per, 0, 0, i % per)),
                   pl.BlockSpec((1, H_B, tm, LANES), hblk), pl.BlockSpec((1, H_B, tm), tblk)],
        out_shape=[jax.ShapeDtypeStruct((m, W_CONV_A), F32), jax.ShapeDtypeStruct((m, W_V_A), F32),
                   jax.ShapeDtypeStruct((m, LANES), F32), jax.ShapeDtypeStruct((b, W_B, l), F32),
                   jax.ShapeDtypeStruct((b, W_B, l), F32), hshape,
                   jax.ShapeDtypeStruct((b, H_B, LANES, l), BF16), hshape, jax.ShapeDtypeStruct((b, H_B, l), F32)],
        scratch_shapes=[pltpu.VMEM((8, LANES), F32), pltpu.VMEM((H_B, LANES), F32)],
        compiler_params=_cparams("arbitrary"), name="in_proj")(x, *consts)


LOOKAHEAD = 2


def _fox_flash_kernel(it_ref, jt_ref, fl_ref, qf_ref, kft_ref, vf_ref, o_ref, m_ref, acc_ref):
    p_id = pl.program_id(1)
    i = it_ref[p_id]
    j = jt_ref[p_id]
    flags = fl_ref[p_id]
    tq = qf_ref.shape[2]
    tk = kft_ref.shape[3]

    @pl.when(j == 0)
    def _():
        m_ref[...] = jnp.full_like(m_ref, NEG)
        acc_ref[...] = jnp.zeros_like(acc_ref)

    def heads(masked, row0=0):
        nr = tq - row0
        rs = slice(row0, tq)
        if masked:
            visible = _iota((nr, tk), 1) + j * tk <= _iota((nr, tk), 0) + (i * tq + row0)

        def scores(h):
            s = jnp.dot(qf_ref[0, h, rs, :], kft_ref[0, h], preferred_element_type=F32)
            return jnp.where(visible, s, NEG) if masked else s

        pending = [scores(h) for h in range(LOOKAHEAD)]
        for h in range(H_B):
            s = pending.pop(0)
            if h + LOOKAHEAD < H_B:
                pending.append(scores(h + LOOKAHEAD))
            m_prev = m_ref[h, rs, :]
            m_new = jnp.maximum(m_prev, jnp.max(s, axis=-1, keepdims=True))
            alpha = jnp.exp2(m_prev - m_new)
            p = jnp.exp2(s - jnp.concatenate([m_new] * (tk // LANES), axis=1)).astype(BF16)
            acc_ref[h, rs, :] = alpha * acc_ref[h, rs, :] + jnp.dot(p, vf_ref[0, h], preferred_element_type=F32)
            m_ref[h, rs, :] = m_new

    @pl.when((flags & 5) == 0)
    def _():
        heads(False)

    @pl.when((flags & 5) == 1)
    def _():
        heads(True)

    @pl.when((flags & 4) == 4)
    def _():
        heads(True, tq // 2)

    @pl.when((flags & 2) == 2)
    def _():
        lane = _iota((tq, LANES), 1)
        for pr in range(H_B // 2):
            a0 = acc_ref[2 * pr]
            a1 = acc_ref[2 * pr + 1]
            o0 = a0 * (1.0 / a0[:, DH_B:DH_B + 1])
            o1 = a1 * (1.0 / a1[:, DH_B:DH_B + 1])
            o_ref[0, :, pr * LANES:(pr + 1) * LANES] = jnp.where(lane < DH_B, o0, pltpu.roll(o1, DH_B, 1))


def _fox_flash(qf, kft, vf, tq=1024, tk=512):
    b, hh, l, _ = qf.shape
    tk = min(tk, l)
    tq = tq if l % tq == 0 else tk
    assert l % tq == 0 and l % tk == 0
    its, jts, fls = [], [], []
    for i in range(l // tq):
        j_last = ((i + 1) * tq - 1) // tk
        for j in range(j_last + 1):
            crosses = (j + 1) * tk - 1 > i * tq
            lower_only = j * tk > i * tq + tq // 2 - 1
            its.append(i)
            jts.append(j)
            fls.append(int(crosses) + 2 * int(j == j_last) + 4 * int(lower_only))
    it, jt, fl = (jnp.asarray(np.array(a, np.int32)) for a in (its, jts, fls))
    qmap = lambda bi, p, it, jt, fl: (bi, 0, it[p], 0)
    kmap = lambda bi, p, it, jt, fl: (bi, 0, jt[p], 0)
    ktmap = lambda bi, p, it, jt, fl: (bi, 0, 0, jt[p])
    omap = lambda bi, p, it, jt, fl: (bi, it[p], 0)
    grid_spec = pltpu.PrefetchScalarGridSpec(
        num_scalar_prefetch=3, grid=(b, len(its)),
        in_specs=[pl.BlockSpec((1, hh, tq, LANES), qmap), pl.BlockSpec((1, hh, LANES, tk), ktmap),
                  pl.BlockSpec((1, hh, tk, LANES), kmap)],
        out_specs=pl.BlockSpec((1, tq, hh * DH_B), omap),
        scratch_shapes=[pltpu.VMEM((hh, tq, LANES), F32), pltpu.VMEM((hh, tq, LANES), F32)])
    return pl.pallas_call(
        _fox_flash_kernel, grid_spec=grid_spec,
        out_shape=jax.ShapeDtypeStruct((b, l, hh * DH_B), F32),
        compiler_params=_cparams("parallel", "arbitrary"), name="fox_flash")(it, jt, fl, qf, kft, vf)


def _gdn_kernel(x_ref, z_ref, a_ref, b_ref, cw_ref, al_ref, dt_ref, gn_ref, oa_ref, s_ref,
                xs_ref, st_ref, *, nc):
    c = CHUNK_A
    r = H_A * c
    t = nc * c
    halo = 8
    nb = x_ref.shape[0]

    @pl.when(pl.program_id(0) == 0)
    def _():
        xs_ref[:, 0:halo, :] = jnp.zeros((nb, halo, xs_ref.shape[2]), F32)
        st_ref[...] = jnp.zeros_like(st_ref)

    @pl.when(pl.program_id(0) != 0)
    def _():
        xs_ref[:, 0:halo, :] = xs_ref[:, t:t + halo, :]

    ri = _iota((r, r), 0)
    ci = _iota((r, r), 1)
    same = (ri // c) == (ci // c)
    incl = same & (ri >= ci)
    strict = same & (ri > ci)
    cum = (same & (ri <= ci)).astype(BF16)
    eye = ri == ci
    last = ((ci % c) == (c - 1)) & same

    def stack(a):
        return jnp.concatenate([a[:, h * DK_A:(h + 1) * DK_A] for h in range(H_A)], axis=0)

    pre = []
    for bi in range(nb):
        xs_ref[bi, halo:halo + t, :] = x_ref[bi]
        y = None
        for jj in range(CONV_A):
            o = halo - (CONV_A - 1) + jj
            term = xs_ref[bi, o:o + t, :] * cw_ref[jj:jj + 1, :]
            y = term if y is None else y + term
        y = _silu(y)
        g_rows = -jnp.exp(al_ref[...]) * _softplus(a_ref[bi, 0] + dt_ref[...])
        beta_rows = jax.nn.sigmoid(b_ref[bi, 0])
        gc_rows = _dot3_r(g_rows, cum)
        for n in range(nc):
            rows = slice(n * c, (n + 1) * c)
            gc_r = gc_rows[n:n + 1, :]
            gcb = jnp.broadcast_to(gc_r, (r, r))
            gc_c = jnp.sum(jnp.where(eye, gcb, 0.0), axis=1, keepdims=True)
            gl_c = jnp.sum(jnp.where(last, gcb, 0.0), axis=1, keepdims=True)
            beta_c = jnp.sum(jnp.where(eye, jnp.broadcast_to(beta_rows[n:n + 1, :], (r, r)), 0.0),
                             axis=1, keepdims=True)
            q = stack(y[rows, 0:W_QK_A])
            k = stack(y[rows, W_QK_A:2 * W_QK_A])
            v = stack(y[rows, 2 * W_QK_A:W_CONV_A])
            q = q * lax.rsqrt(jnp.sum(q * q, axis=1, keepdims=True) + EPS) * (DK_A ** -0.5)
            k = k * lax.rsqrt(jnp.sum(k * k, axis=1, keepdims=True) + EPS)
            decay = jnp.exp(jnp.where(incl, gc_c - gc_r, -jnp.inf))
            kb = k * beta_c
            k16 = k.astype(BF16)
            eg = jnp.exp(gc_c)
            pre.append(dict(decay=decay, kk=_dot_nt(kb.astype(BF16), k16), qk=_dot_nt(q.astype(BF16), k16),
                            rhs=jnp.concatenate([v * beta_c, kb * eg], axis=1).astype(BF16),
                            qg=(q * eg).astype(BF16), kdec=(k * jnp.exp(gl_c - gc_c)).astype(BF16),
                            dec=[jnp.exp(gl_c[h * c:h * c + 1, :]) for h in range(H_A)]))
    for pc in pre:
        pc["a_qk"] = jnp.where(incl, pc["qk"] * pc["decay"], 0.0).astype(BF16)
        pc["mneg"] = jnp.where(strict, -(pc["kk"] * pc["decay"]), 0.0)
        pc["tinv"] = jnp.where(eye, 1.0, 0.0) + pc["mneg"]
    for _ in range(int(math.log2(c)) - 1):
        for pc in pre:
            m16 = pc["mneg"].astype(BF16)
            pc["mneg"] = jnp.dot(m16, m16, preferred_element_type=F32)
        for pc in pre:
            pc["tinv"] = pc["tinv"] + jnp.dot(pc["tinv"].astype(BF16), pc["mneg"].astype(BF16),
                                              preferred_element_type=F32)
    for pc in pre:
        uw = jnp.dot(pc["tinv"].astype(BF16), pc["rhs"], preferred_element_type=F32)
        pc["u"] = uw[:, 0:DV_A]
        pc["w"] = uw[:, DV_A:2 * DV_A].astype(BF16)

    s_heads = [[st_ref[bi, :, h * DV_A:(h + 1) * DV_A] for h in range(H_A)] for bi in range(nb)]
    for n in range(nc):
        rows = slice(n * c, (n + 1) * c)
        for bi in range(nb):
            pc = pre[bi * nc + n]
            v_parts, qs_parts = [], []
            for h in range(H_A):
                hr = slice(h * c, (h + 1) * c)
                wq = jnp.concatenate([pc["w"][hr], pc["qg"][hr]], axis=0)
                wq_s = jnp.dot(wq, s_heads[bi][h].astype(BF16), preferred_element_type=F32)
                v_parts.append(pc["u"][hr] - wq_s[0:c])
                qs_parts.append(wq_s[c:2 * c])
            v_new = jnp.concatenate(v_parts, axis=0).astype(BF16)
            o = jnp.concatenate(qs_parts, axis=0) + jnp.dot(pc["a_qk"], v_new, preferred_element_type=F32)
            for h in range(H_A):
                hr = slice(h * c, (h + 1) * c)
                s_heads[bi][h] = s_heads[bi][h] * pc["dec"][h] + _dot_tn(pc["kdec"][hr], v_new[hr])
            on = o * lax.rsqrt(jnp.mean(o * o, axis=1, keepdims=True) + EPS) * gn_ref[...]
            on = jnp.concatenate([on[h * c:(h + 1) * c, :] for h in range(H_A)], axis=1)
            oa_ref[bi, rows, :] = on * _silu(z_ref[bi, rows, :])
    for bi in range(nb):
        for h in range(H_A):
            st_ref[bi, :, h * DV_A:(h + 1) * DV_A] = s_heads[bi][h]

    @pl.when(pl.program_id(0) == pl.num_programs(0) - 1)
    def _():
        s_ref[...] = st_ref[...]


def _gdn_prompt(conv_in, z, a_rows, b_rows, conv_w, alog_row, dt_row, gnorm, nc=4):
    b, l, wc = conv_in.shape
    n = l // CHUNK_A
    nc = min(nc, n)
    t = nc * CHUNK_A
    r = H_A * CHUNK_A
    blk = lambda i: (0, i, 0)
    gate = lambda i: (0, i, 0, 0)
    a_rows = a_rows.reshape(b, n // nc, nc, r)
    b_rows = b_rows.reshape(b, n // nc, nc, r)
    kern = functools.partial(_gdn_kernel, nc=nc)
    oa, s = pl.pallas_call(
        kern, grid=(n // nc,),
        in_specs=[pl.BlockSpec((b, t, wc), blk), pl.BlockSpec((b, t, W_V_A), blk),
                  pl.BlockSpec((b, 1, nc, r), gate), pl.BlockSpec((b, 1, nc, r), gate),
                  _const_spec(conv_w.shape), _const_spec(alog_row.shape), _const_spec(dt_row.shape),
                  _const_spec(gnorm.shape)],
        out_specs=[pl.BlockSpec((b, t, W_V_A), blk), _const_spec((b, DK_A, H_A * DV_A))],
        out_shape=[jax.ShapeDtypeStruct((b, l, W_V_A), F32),
                   jax.ShapeDtypeStruct((b, DK_A, H_A * DV_A), F32)],
        scratch_shapes=[pltpu.VMEM((b, t + 8, wc), F32), pltpu.VMEM((b, DK_A, H_A * DV_A), F32)],
        compiler_params=_cparams("arbitrary"), name="gdn_prompt")(
            conv_in, z, a_rows, b_rows, conv_w, alog_row, dt_row, gnorm)
    return oa, s


def _cfm_tail(y, bd_ref, lg_ref, lb_ref, w2_ref, res):
    y = y + bd_ref[...]
    yc = y - jnp.mean(y, axis=-1, keepdims=True)
    yn = yc * lax.rsqrt(jnp.mean(yc * yc, axis=-1, keepdims=True) + EPS) * lg_ref[...] + lb_ref[...]
    return res + jnp.dot(_silu(yn).astype(BF16), w2_ref[...], preferred_element_type=F32)


def _cfm_kernel(u_ref, x_ref, wd_ref, bd_ref, lg_ref, lb_ref, w2_ref, o_ref, xs_ref, sh_ref):
    t = u_ref.shape[1]
    halo = 32

    @pl.when(pl.program_id(1) == 0)
    def _():
        xs_ref[0:halo, :] = jnp.zeros((halo, xs_ref.shape[1]), F32)

    @pl.when(pl.program_id(1) != 0)
    def _():
        xs_ref[0:halo, :] = xs_ref[t:t + halo, :]

    xs_ref[halo:halo + t, :] = u_ref[0]
    y = None
    for r in range(8):
        offs = [o for o in range(halo - (CONV_C - 1), halo + 1) if o % 8 == r]
        if not offs:
            continue
        if r:
            n_rows = max(offs) - r + t
            sh_ref[0:n_rows, :] = xs_ref[r:r + n_rows, :]
        win_ref = sh_ref if r else xs_ref
        for o in offs:
            jj = o - (halo - (CONV_C - 1))
            term = win_ref[o - r:o - r + t, :] * wd_ref[jj:jj + 1, :]
            y = term if y is None else y + term
    o_ref[0] = _cfm_tail(y, bd_ref, lg_ref, lb_ref, w2_ref, x_ref[0])


def _cfm_prompt(u, x, w_dw, b_dw, ln_g, ln_b, w2, t=512):
    b, l, d = u.shape
    t = min(t, l)
    blk = lambda bi, i: (bi, i, 0)
    return pl.pallas_call(
        _cfm_kernel, grid=(b, l // t),
        in_specs=[pl.BlockSpec((1, t, d), blk), pl.BlockSpec((1, t, d), blk), _const_spec(w_dw.shape),
                  _const_spec(b_dw.shape), _const_spec(ln_g.shape), _const_spec(ln_b.shape),
                  _const_spec(w2.shape)],
        out_specs=pl.BlockSpec((1, t, d), blk),
        out_shape=jax.ShapeDtypeStruct((b, l, d), F32),
        scratch_shapes=[pltpu.VMEM((t + 32, d), F32), pltpu.VMEM((t + 32, d), F32)],
        compiler_params=_cparams("parallel", "arbitrary"), name="cfm_prompt")(
            u, x, w_dw, b_dw, ln_g, ln_b, w2)


def _cfm_sample_kernel(buf_ref, u_ref, x_ref, wd_ref, bd_ref, lg_ref, lb_ref, w2_ref, o_ref):
    y = u_ref[...] * wd_ref[CONV_C - 1:CONV_C, :]
    for jj in range(CONV_C - 1):
        y = y + buf_ref[jj] * wd_ref[jj:jj + 1, :]
    o_ref[...] = _cfm_tail(y, bd_ref, lg_ref, lb_ref, w2_ref, x_ref[...])


def _cfm_sample(buf_t, u, x, w_dw, b_dw, ln_g, ln_b, w2):
    args = (buf_t, u, x, w_dw, b_dw, ln_g, ln_b, w2)
    return pl.pallas_call(
        _cfm_sample_kernel, grid=(1,), in_specs=[_const_spec(a.shape) for a in args],
        out_specs=_const_spec(x.shape), out_shape=jax.ShapeDtypeStruct(x.shape, F32),
        compiler_params=_cparams("arbitrary"), name="cfm_sample")(*args)


def _rows8(row):
    return jnp.broadcast_to(row, (8, row.shape[1]))


def _gdn_sample_kernel(f_ref, cw_ref, sm_ref, al_ref, dt_ref, z_ref, gn_ref, s_ref, oa_ref, so_ref):
    y = _silu(jnp.sum(f_ref[0] * cw_ref[...], axis=0, keepdims=True))
    sm = sm_ref[0]
    g = -jnp.exp(al_ref[...]) * _softplus(sm + dt_ref[...])
    beta = jax.nn.sigmoid(sm)
    row_id = _iota((8, DK_A), 0)
    outs = []
    for h in range(H_A):
        q = y[:, h * DK_A:(h + 1) * DK_A]
        k = y[:, W_QK_A + h * DK_A:W_QK_A + (h + 1) * DK_A]
        v = y[:, 2 * W_QK_A + h * DV_A:2 * W_QK_A + (h + 1) * DV_A]
        q = q * lax.rsqrt(jnp.sum(q * q, axis=1, keepdims=True) + EPS) * (DK_A ** -0.5)
        k = k * lax.rsqrt(jnp.sum(k * k, axis=1, keepdims=True) + EPS)
        eg = jnp.exp(g[:, SM_A + h:SM_A + h + 1])
        bh = beta[:, SM_B + h:SM_B + h + 1]
        s = s_ref[0, h]
        lhs = jnp.where(row_id == 0, _rows8(k), jnp.where(row_id == 1, _rows8(q), 0.0)).astype(BF16)
        rs = jnp.dot(lhs, s.astype(BF16), preferred_element_type=F32)
        v_new = bh * (v - eg * rs[0:1, :])
        o = eg * rs[1:2, :] + jnp.sum(q * k, axis=1, keepdims=True) * v_new
        so_ref[0, h] = s * eg + _col_from_row(k, DK_A) * v_new
        outs.append(o * lax.rsqrt(jnp.mean(o * o, axis=1, keepdims=True) + EPS) * gn_ref[...])
    oa_ref[0] = jnp.concatenate(outs, axis=1) * _silu(z_ref[0])


def _gdn_sample(full, conv_w, small, alog_row, dt_row, z, gnorm, state):
    db = full.shape[0]
    r3 = lambda i: (i, 0, 0)
    r4 = lambda i: (i, 0, 0, 0)
    return pl.pallas_call(
        _gdn_sample_kernel, grid=(db,),
        in_specs=[pl.BlockSpec((1,) + full.shape[1:], r3), _const_spec(conv_w.shape),
                  pl.BlockSpec((1, 1, LANES), r3), _const_spec(alog_row.shape), _const_spec(dt_row.shape),
                  pl.BlockSpec((1, 1, W_V_A), r3), _const_spec(gnorm.shape),
                  pl.BlockSpec((1,) + state.shape[1:], r4)],
        out_specs=[pl.BlockSpec((1, 1, W_V_A), r3), pl.BlockSpec((1,) + state.shape[1:], r4)],
        out_shape=[jax.ShapeDtypeStruct((db, 1, W_V_A), F32), jax.ShapeDtypeStruct(state.shape, F32)],
        compiler_params=_cparams("parallel"), name="gdn_sample")(
            full, conv_w, small, alog_row, dt_row, z, gnorm, state)


def _xattn_sample_kernel(q_ref, mk_ref, mv_ref, o_ref):
    n_rows = mk_ref.shape[0]
    n_mem = n_rows // 8
    dh = 2 * LANES
    q8 = q_ref[0] * (dh ** -0.5)
    red = jnp.sum(mk_ref[...].reshape(n_mem, 8, LANES) * q8[None], axis=-1, keepdims=True)
    own_lane = _iota((LANES, 8, LANES), 0) == _iota((LANES, 8, LANES), 2)
    n_t = n_mem // LANES
    tiles = [jnp.sum(jnp.where(own_lane, red[j * LANES:(j + 1) * LANES], 0.0), axis=0) for j in range(n_t)]
    s8 = jnp.concatenate(tiles, axis=1)
    s = s8[0:H_X] + s8[H_X:2 * H_X]
    p = jnp.exp(s - jnp.max(s, axis=-1, keepdims=True))
    inv_l = 1.0 / jnp.sum(p, axis=-1, keepdims=True)
    p8 = jnp.concatenate([p, p], axis=0)
    ones = jnp.ones((LANES, LANES), BF16)
    acc = None
    for j in range(n_t):
        z = jnp.where(own_lane, p8[None, :, j * LANES:(j + 1) * LANES], 0.0).reshape(LANES * 8, LANES)
        p3 = jnp.dot(z.astype(BF16), ones, preferred_element_type=F32).reshape(LANES, 8, LANES)
        v3 = mv_ref[j * LANES * 8:(j + 1) * LANES * 8, :].reshape(LANES, 8, LANES)
        part = jnp.sum(p3 * v3, axis=0)
        acc = part if acc is None else acc + part
    o_ref[0] = acc * jnp.concatenate([inv_l, inv_l], axis=0)


def _mem_rows(cache):
    dp, db, nm = cache.shape[:3]
    x = cache.reshape(dp, db, nm, H_X, 2, LANES)
    return jnp.transpose(x, (0, 1, 2, 4, 3, 5)).reshape(dp, db, nm * 8, LANES)


def _xattn_sample(q, mk_rows, mv_rows, layer):
    db, d = q.shape
    n_rows = mk_rows.shape[2]
    q8 = jnp.transpose(q.reshape(db, H_X, 2, LANES), (0, 2, 1, 3)).reshape(db, 8, LANES)
    r3 = lambda i: (i, 0, 0)
    mem = lambda i: (layer, i, 0, 0)
    o8 = pl.pallas_call(
        _xattn_sample_kernel, grid=(db,),
        in_specs=[pl.BlockSpec((1, 8, LANES), r3), pl.BlockSpec((None, None, n_rows, LANES), mem),
                  pl.BlockSpec((None, None, n_rows, LANES), mem)],
        out_specs=pl.BlockSpec((1, 8, LANES), r3), out_shape=jax.ShapeDtypeStruct((db, 8, LANES), F32),
        compiler_params=_cparams("parallel"), name="xattn_sample")(q8, mk_rows, mv_rows)
    return jnp.transpose(o8.reshape(db, 2, H_X, LANES), (0, 2, 1, 3)).reshape(db, d)


def _head_rows(h):
    return slice(h * DH_B, (h + 1) * DH_B)


def _fox_sample_kernel(pt_ref, q_ref, kn_ref, vn_ref, f_ref, bf_ref, *refs, g_pages):
    k_refs = refs[0:g_pages]
    v_refs = refs[g_pages:2 * g_pages]
    lf_refs = refs[2 * g_pages:3 * g_pages]
    o_ref, lfn_ref, qb_ref, m_ref, l_ref, acc_ref, carry_ref = refs[3 * g_pages:]
    del pt_ref
    w = H_B * DH_B
    gi = pl.program_id(1)
    bcast = lambda col: jnp.broadcast_to(col, (H_B, LANES))
    head_id = _iota((H_B, LANES), 0)

    def rows_to_tile(rows):
        n = rows[0].shape[1]
        out = jnp.broadcast_to(rows[0], (H_B, n))
        for h in range(1, H_B):
            out = jnp.where(head_id[:, 0:n] == h, rows[h], out)
        return out

    @pl.when(gi == 0)
    def _():
        q = q_ref[0] * (DH_B ** -0.5)
        qb_ref[...] = jnp.broadcast_to(_col_from_row(q, w), (w, LANES))
        lfn = _log_sigmoid(f_ref[0] + bf_ref[...])
        lfn_ref[0] = lfn
        qk = q * kn_ref[0]
        s_new = rows_to_tile([jnp.sum(qk[:, _head_rows(h)], axis=1, keepdims=True) for h in range(H_B)])
        m_ref[...] = bcast(s_new)
        l_ref[...] = jnp.ones_like(l_ref)
        acc_ref[...] = jnp.where(_iota((w, LANES), 1) == 0, _col_from_row(vn_ref[0], w), 0.0)
        carry_ref[...] = bcast(lfn)

    later = (_iota((PAGE, PAGE), 0) > _iota((PAGE, PAGE), 1)).astype(BF16)
    m = m_ref[:, 0:1]
    l = l_ref[:, 0:1]
    carry = carry_ref[:, 0:1]
    logits = [None] * g_pages
    top = None
    suffix = _dot3_r(jnp.concatenate([lf_refs[jj][...] for jj in range(g_pages)], axis=0), later)
    for jj in reversed(range(g_pages)):
        s_t = rows_to_tile([jnp.sum(k_refs[jj][h] * qb_ref[_head_rows(h), :], axis=0, keepdims=True)
                            for h in range(H_B)])
        lf_t = lf_refs[jj][...]
        logits[jj] = s_t + suffix[jj * H_B:(jj + 1) * H_B, :] + carry
        carry = carry + jnp.sum(lf_t, axis=1, keepdims=True)
        top = logits[jj] if top is None else jnp.maximum(top, logits[jj])
    m_new = jnp.maximum(m, jnp.max(top, axis=1, keepdims=True))
    alpha = jnp.exp(m - m_new)
    probs = [jnp.exp(lg - m_new) for lg in logits]
    p_sum = probs[0]
    for p_t in probs[1:]:
        p_sum = p_sum + p_t
    for h in range(H_B):
        acc_h = acc_ref[_head_rows(h), :] * alpha[h:h + 1, :]
        for jj in range(g_pages):
            acc_h = acc_h + v_refs[jj][h] * probs[jj][h:h + 1, :]
        acc_ref[_head_rows(h), :] = acc_h
    l = alpha * l + jnp.sum(p_sum, axis=1, keepdims=True)
    m_ref[...] = bcast(m_new)
    l_ref[...] = bcast(l)
    carry_ref[...] = bcast(carry)

    @pl.when(gi == pl.num_programs(1) - 1)
    def _():
        inv = 1.0 / l
        den = jnp.concatenate([jnp.broadcast_to(inv[h:h + 1, :], (DH_B, 1)) for h in range(H_B)], axis=0)
        col = jnp.sum(acc_ref[...], axis=1, keepdims=True) * den
        eye = _iota((w, w), 0) == _iota((w, w), 1)
        o_ref[0] = jnp.sum(jnp.where(eye, jnp.broadcast_to(col, (w, w)), 0.0), axis=0, keepdims=True)


def _fox_sample(q, k_new, v_new, f_col, bf_col, cache_kt, cache_vt, cache_lft, page_table, g_pages=32):
    db = q.shape[0]
    w = H_B * DH_B
    n_pages = page_table.shape[1]
    g_pages = min(g_pages, n_pages)
    ng = n_pages // g_pages
    r3 = lambda i, g, pt: (i, 0, 0)

    def page_map(nd):
        return [(lambda i, g, pt, jj=jj: (pt[i * n_pages + (ng - 1 - g) * g_pages + jj],) + (0,) * nd)
                for jj in range(g_pages)]

    in_specs = [pl.BlockSpec((1, 1, w), r3)] * 3 + [pl.BlockSpec((1, H_B, 1), r3),
                                                    pl.BlockSpec(bf_col.shape, lambda i, g, pt: (0, 0))]
    in_specs += [pl.BlockSpec((None, H_B, DH_B, PAGE), mp) for mp in page_map(3)] * 2
    in_specs += [pl.BlockSpec((None, H_B, PAGE), mp) for mp in page_map(2)]
    grid_spec = pltpu.PrefetchScalarGridSpec(
        num_scalar_prefetch=1, grid=(db, ng), in_specs=in_specs,
        out_specs=[pl.BlockSpec((1, 1, w), r3), pl.BlockSpec((1, H_B, 1), r3)],
        scratch_shapes=[pltpu.VMEM((w, LANES), F32), pltpu.VMEM((H_B, LANES), F32), pltpu.VMEM((H_B, LANES), F32),
                        pltpu.VMEM((w, LANES), F32), pltpu.VMEM((H_B, LANES), F32)])
    kern = functools.partial(_fox_sample_kernel, g_pages=g_pages)
    return pl.pallas_call(
        kern, grid_spec=grid_spec,
        out_shape=[jax.ShapeDtypeStruct((db, 1, w), F32), jax.ShapeDtypeStruct((db, H_B, 1), F32)],
        compiler_params=_cparams("parallel", "arbitrary"), name="fox_sample")(
            page_table.reshape(-1), q, k_new, v_new, f_col, bf_col,
            *([cache_kt] * g_pages), *([cache_vt] * g_pages), *([cache_lft] * g_pages))


def _row(v):
    return v.reshape(1, -1).astype(F32)


def _pad_lanes(v, start):
    return jnp.zeros((1, LANES), F32).at[0, start:start + v.shape[0]].set(v)


def _chunk_rows(cols, b, l):
    n = l // CHUNK_A
    return cols.reshape(b, n, CHUNK_A, H_A).transpose(0, 1, 3, 2).reshape(b, n, H_A * CHUNK_A)


def kernel(x_prompt, x_sample, mem_prompt, cache_fox_k, cache_fox_v, cache_fox_logf, page_table, state_gdn, state_gdn_conv, state_cfm_conv, cache_mem_k, cache_mem_v, norm_mix, w_in_e, conv_a, a_log, dt_bias, gnorm_a, b_f, w_out_e, w_pw1, b_pw1, w_dw, b_dw, ln_g, ln_b, w_pw2, norm_mem, norm_x, w_xq, w_xkv, w_xo, norm_f, w_up, w_down, norm_out):
    b, l, d = x_prompt.shape
    db = x_sample.shape[0]
    n_mem = mem_prompt.shape[1]
    depth = norm_mix.shape[0]
    dh_x = d // H_X
    bf = lambda w: w.astype(BF16)

    w_in = w_in_e[0]
    off_aa = W_CONV_A + W_V_A
    off_qb = off_aa + 2 * H_A
    off_fb = off_qb + 3 * W_B
    w_small = jnp.concatenate([w_in[:, off_aa:off_qb], w_in[:, off_fb:]], axis=1)
    w_small = jnp.pad(w_small, ((0, 0), (0, LANES - w_small.shape[1])))
    w_in_r = bf(jnp.concatenate([w_in[:, :off_aa], w_in[:, off_qb:off_fb], w_small], axis=1))
    in_widths = (W_CONV_A, W_V_A, W_B, W_B, W_B, LANES)
    off_kb = off_qb + W_B
    off_vb = off_kb + W_B
    w_in_p = bf(jnp.concatenate([w_in[:, :off_aa], w_in[:, off_qb:off_kb], w_in[:, off_vb:off_fb], w_small], axis=1))
    w_in_t = [bf(w_in[:, off_kb:off_vb].T), bf(w_in[:, off_vb:off_fb].T), bf(w_in[:, off_fb:].T)]
    w_out_a = bf(w_out_e[0][:W_V_A])
    w_out_b = bf(w_out_e[0][W_V_A:])
    w_xq_b = [bf(w_xq[i]) for i in range(depth)]
    w_xo_b = [bf(w_xo[i]) for i in range(depth)]
    w_up_b = [bf(w_up[i]) for i in range(depth)]
    w_down_b = [bf(w_down[i]) for i in range(depth)]
    w_pw1_b = bf(w_pw1[0])
    w_pw2_b = bf(w_pw2[0])
    alog_rows = _row(jnp.repeat(a_log[0], CHUNK_A))
    dt_rows = _row(jnp.repeat(dt_bias[0], CHUNK_A))
    gn = _row(gnorm_a[0])

    memf = mem_prompt.reshape(b * n_mem, d)
    mem_k, mem_v = [], []
    for i in range(depth):
        mk, mv = _linear([memf], [bf(w_xkv[i])], gain=_row(norm_mem[i]), out_widths=(d, d), name="mem_kv")
        mem_k.append(mk.reshape(b, n_mem, d))
        mem_v.append(mv.reshape(b, n_mem, d))
    mem_k_prompt = jnp.stack(mem_k).reshape(depth, b, n_mem, H_X, dh_x)
    mem_v_prompt = jnp.stack(mem_v).reshape(depth, b, n_mem, H_X, dh_x)

    xp = x_prompt.reshape(b * l, d)
    conv_in, z, small, k_t, v_t, qf, kft, vf, lft = _in_proj_prompt(
        xp, w_in_p, w_in_t[0], w_in_t[1], w_in_t[2], _row(norm_mix[0]), _pad_lanes(b_f[0], SM_F),
        b_f[0].reshape(H_B, 1), b, l)
    fox_k_prompt = jnp.transpose(k_t.reshape(b, H_B, DH_B, l), (0, 3, 1, 2))[None]
    fox_v_prompt = jnp.transpose(v_t.reshape(b, H_B, DH_B, l), (0, 3, 1, 2))[None]
    conv_in3 = conv_in.reshape(b, l, W_CONV_A)
    gdn_conv_prompt = conv_in3[:, l - (CONV_A - 1):, :][None]
    oa, s_fin = _gdn_prompt(conv_in3, z.reshape(b, l, W_V_A),
                            _chunk_rows(small[:, SM_A:SM_A + H_A], b, l),
                            _chunk_rows(small[:, SM_B:SM_B + H_A], b, l),
                            conv_a[0], alog_rows, dt_rows, gn)
    gdn_state_prompt = s_fin.reshape(b, DK_A, H_A, DV_A).transpose(0, 2, 1, 3)[None]
    fox_logf_prompt = jnp.transpose(lft, (0, 2, 1))[None]
    ob = _fox_flash(qf, kft, vf)
    x = _xattn_prompt(x_prompt, _row(norm_x[0]), w_xq_b[0], bf(mem_k[0]), bf(mem_v[0]), w_xo_b[0],
                      pre=(oa, ob), pre_w=(w_out_a, w_out_b))
    x = _mlp(x.reshape(b * l, d), _row(norm_f[0]), w_up_b[0], w_down_b[0])
    glu = _linear([x], [w_pw1_b], gain=_row(norm_mix[1]), bias=_row(b_pw1[0]), act="glu", tm=1024,
                  name="pw1_glu")
    glu3 = glu.reshape(b, l, d)
    cfm_conv_prompt = glu3[:, l - (CONV_C - 1):, :][None]
    x = _cfm_prompt(glu3, x.reshape(b, l, d), w_dw[0], _row(b_dw[0]), _row(ln_g[0]), _row(ln_b[0]), w_pw2_b)
    x = _xattn_prompt(x, _row(norm_x[1]), w_xq_b[1], bf(mem_k[1]), bf(mem_v[1]), w_xo_b[1])
    y_prompt = _mlp(x.reshape(b * l, d), _row(norm_f[1]), w_up_b[1], w_down_b[1],
                    final_gain=_row(norm_out)).reshape(b, l, d)

    xs = x_sample.reshape(db, d)
    conv_s, z_s, q_s, k_s, v_s, small_s = _linear([xs], [w_in_r], gain=_row(norm_mix[0]),
                                                  out_widths=in_widths, name="in_proj_s")
    fox_k_sample = k_s.reshape(1, db, 1, H_B, DH_B)
    fox_v_sample = v_s.reshape(1, db, 1, H_B, DH_B)
    full = jnp.concatenate([state_gdn_conv[0], conv_s[:, None, :]], axis=1)
    gdn_conv_sample = full[:, 1:, :][None]
    oa_s, s_new = _gdn_sample(full, conv_a[0], small_s.reshape(db, 1, LANES), _pad_lanes(a_log[0], SM_A),
                              _pad_lanes(dt_bias[0], SM_A), z_s.reshape(db, 1, W_V_A), gn, state_gdn[0])
    gdn_state_sample = s_new[None]
    ob_s, lf_s = _fox_sample(q_s.reshape(db, 1, W_B), k_s.reshape(db, 1, W_B), v_s.reshape(db, 1, W_B),
                             small_s[:, SM_F:SM_F + H_B].reshape(db, H_B, 1), b_f[0].reshape(H_B, 1),
                             jnp.transpose(cache_fox_k[0], (0, 2, 3, 1)), jnp.transpose(cache_fox_v[0], (0, 2, 3, 1)),
                             jnp.transpose(cache_fox_logf[0], (0, 2, 1)), page_table)
    fox_logf_sample = lf_s.reshape(1, db, 1, H_B)
    x = _linear([oa_s.reshape(db, W_V_A), ob_s.reshape(db, W_B)], [w_out_a, w_out_b], res=xs, name="out_proj_s")

    mk_rows = _mem_rows(cache_mem_k)
    mv_rows = _mem_rows(cache_mem_v)

    def xattn_s(x, i):
        qx = _linear([x], [w_xq_b[i]], gain=_row(norm_x[i]), name="xq_s")
        o = _xattn_sample(qx, mk_rows, mv_rows, i)
        return _linear([o], [w_xo_b[i]], res=x, name="xo_s")

    x = xattn_s(x, 0)
    x = _mlp(x, _row(norm_f[0]), w_up_b[0], w_down_b[0])
    glu_s = _linear([x], [w_pw1_b], gain=_row(norm_mix[1]), bias=_row(b_pw1[0]), act="glu", name="pw1_glu_s")
    cfm_conv_sample = jnp.concatenate([state_cfm_conv[0][:, 1:, :], glu_s[:, None, :]], axis=1)[None]
    x = _cfm_sample(state_cfm_conv[0].transpose(1, 0, 2), glu_s, x, w_dw[0], _row(b_dw[0]), _row(ln_g[0]),
                    _row(ln_b[0]), w_pw2_b)
    x = xattn_s(x, 1)
    y_sample = _mlp(x, _row(norm_f[1]), w_up_b[1], w_down_b[1], final_gain=_row(norm_out)).reshape(db, 1, d)

    return (y_prompt, y_sample, fox_k_prompt, fox_v_prompt, fox_logf_prompt, fox_k_sample, fox_v_sample,
            fox_logf_sample, gdn_state_prompt, gdn_conv_prompt, gdn_state_sample, gdn_conv_sample,
            cfm_conv_prompt, cfm_conv_sample, mem_k_prompt, mem_v_prompt)
```
